```python
import math
import jax, jax.numpy as jnp
from jax import lax
import numpy as np

D_MODEL = 4096
BATCH = 4
SEQ = 4096
DEPTH = 1

N_META = 16
Q_BLOCK = 128
D_FF = 11008
EPS = 1e-6

DIFF_HEADS = 8
DIFF_HEAD_DIM = 128
DIFF_V_DIM = 2 * DIFF_HEAD_DIM
DIFF_QK_WIDTH = DIFF_HEADS * 2 * DIFF_HEAD_DIM
DIFF_WIDTH = DIFF_HEADS * DIFF_V_DIM

MLA_HEADS = 16
Q_LORA = 1024
KV_LORA = 512
NOPE_D = 128
ROPE_D = 64
MLA_QK_D = NOPE_D + ROPE_D
MLA_V_D = 128
MLA_WIDTH = MLA_HEADS * MLA_V_D
ROPE_THETA = 10000.0

IN_SIZES = (DIFF_QK_WIDTH, DIFF_QK_WIDTH, DIFF_WIDTH, Q_LORA, KV_LORA, ROPE_D)
IN_WIDTH = sum(IN_SIZES)
IN_SPLITS = tuple(int(v) for v in np.cumsum(IN_SIZES)[:-1])

kernel_name = "hybrid_diffattn_mla_macaron_encoder"


def alibi_slopes(n):
    return np.array([2.0 ** (-8.0 * (h + 1) / n) for h in range(n)], dtype=np.float32)


def rms_norm(x, g):
    xf = x.astype(jnp.float32)
    y = xf * lax.rsqrt(jnp.mean(xf * xf, axis=-1, keepdims=True) + EPS)
    return (y * g.astype(jnp.float32)).astype(x.dtype)


def swiglu(h, w_gate, w_up, w_down):
    return (jax.nn.silu(h @ w_gate) * (h @ w_up)) @ w_down


def apply_rope(x, cos, sin):
    half = ROPE_D // 2
    x1, x2 = x[..., :half], x[..., half:]
    cos = cos.astype(x.dtype)
    sin = sin.astype(x.dtype)
    return jnp.concatenate([x1 * cos - x2 * sin, x1 * sin + x2 * cos], axis=-1)


def setup_inputs(seed: int = 0) -> dict:
    key = jax.random.key(seed)
    ks = iter(jax.random.split(key, 40))
    f32 = jnp.float32

    def w(shape, fan_in):
        return jax.random.normal(next(ks), shape, f32) * (fan_in ** -0.5)

    def gain(shape):
        return 1.0 + 0.02 * jax.random.normal(next(ks), shape, f32)

    L = DEPTH
    return {
        "x": jax.random.normal(next(ks), (BATCH, SEQ, D_MODEL), f32),
        "meta_tokens": jax.random.normal(next(ks), (N_META, D_MODEL), f32),
        "ffn1_norm": gain((L, D_MODEL)),
        "ffn1_w_gate": w((L, D_MODEL, D_FF), D_MODEL),
        "ffn1_w_up": w((L, D_MODEL, D_FF), D_MODEL),
        "ffn1_w_down": w((L, D_FF, D_MODEL), D_FF),
        "mix_norm": gain((L, D_MODEL)),
        "w_in": w((L, D_MODEL, IN_WIDTH), D_MODEL),
        "diff_lambda_q1": 0.1 * jax.random.normal(next(ks), (L, DIFF_HEAD_DIM), f32),
        "diff_lambda_k1": 0.1 * jax.random.normal(next(ks), (L, DIFF_HEAD_DIM), f32),
        "diff_lambda_q2": 0.1 * jax.random.normal(next(ks), (L, DIFF_HEAD_DIM), f32),
        "diff_lambda_k2": 0.1 * jax.random.normal(next(ks), (L, DIFF_HEAD_DIM), f32),
        "diff_subln": gain((L, DIFF_V_DIM)),
        "mla_q_norm": gain((L, Q_LORA)),
        "mla_w_uq": w((L, Q_LORA, MLA_HEADS * MLA_QK_D), Q_LORA),
        "mla_kv_norm": gain((L, KV_LORA)),
        "mla_w_ukv": w((L, KV_LORA, MLA_HEADS * (NOPE_D + MLA_V_D)), KV_LORA),
        "w_gate": w((L, D_MODEL, 2 * D_MODEL), D_MODEL),
        "b_gate": 0.02 * jax.random.normal(next(ks), (L, 2 * D_MODEL), f32),
        "w_branch_diff": w((L, DIFF_WIDTH, D_MODEL), DIFF_WIDTH),
        "w_branch_mla": w((L, MLA_WIDTH, D_MODEL), MLA_WIDTH),
        "w_out": w((L, D_MODEL, D_MODEL), D_MODEL),
        "ffn2_norm": gain((L, D_MODEL)),
        "ffn2_w_gate": w((L, D_MODEL, D_FF), D_MODEL),
        "ffn2_w_up": w((L, D_MODEL, D_FF), D_MODEL),
        "ffn2_w_down": w((L, D_FF, D_MODEL), D_FF),
        "final_norm": gain((D_MODEL,)),
    }


def mixing_sublayer(h, w_in, lq1, lk1, lq2, lk2, subln, q_norm, w_uq, kv_norm, w_ukv,
                    lam_init):
    B, T, _ = h.shape
    S = T - N_META
    proj = h @ w_in
    dq, dk, dv, cq, ckv, k_rope = jnp.split(proj, IN_SPLITS, axis=-1)
    dq = dq.reshape(B, T, DIFF_HEADS, 2, DIFF_HEAD_DIM)
    dk = dk.reshape(B, T, DIFF_HEADS, 2, DIFF_HEAD_DIM)
    dv = dv.reshape(B, T, DIFF_HEADS, DIFF_V_DIM)

    lam = (jnp.exp(jnp.sum(lq1.astype(jnp.float32) * lk1.astype(jnp.float32)))
           - jnp.exp(jnp.sum(lq2.astype(jnp.float32) * lk2.astype(jnp.float32)))
           + lam_init)

    pos = jnp.arange(T)
    inv_freq = 1.0 / (ROPE_THETA ** (jnp.arange(0, ROPE_D, 2, dtype=jnp.float32) / ROPE_D))
    ang = pos.astype(jnp.float32)[:, None] * inv_freq[None, :]
    cos, sin = jnp.cos(ang), jnp.sin(ang)
    q = (rms_norm(cq, q_norm) @ w_uq).reshape(B, T, MLA_HEADS, MLA_QK_D)
    q_nope, q_pe = q[..., :NOPE_D], q[..., NOPE_D:]
    q_pe = apply_rope(q_pe, cos[:, None, :], sin[:, None, :])
    mq = jnp.concatenate([q_nope, q_pe], axis=-1)
    kv = (rms_norm(ckv, kv_norm) @ w_ukv).reshape(B, T, MLA_HEADS, NOPE_D + MLA_V_D)
    k_nope, mv = kv[..., :NOPE_D], kv[..., NOPE_D:]
    k_pe = apply_rope(k_rope, cos, sin)
    mk = jnp.concatenate(
        [k_nope, jnp.broadcast_to(k_pe[:, :, None, :], (B, T, MLA_HEADS, ROPE_D))], axis=-1)

    slopes = jnp.asarray(alibi_slopes(DIFF_HEADS))
    kpos = pos
    diff_scale = DIFF_HEAD_DIM ** -0.5
    mla_scale = MLA_QK_D ** -0.5
    out_scale = 1.0 - lam_init

    def attend_block(start, qn):
        qpos = start + jnp.arange(qn)
        both_real = (qpos[:, None] >= N_META) & (kpos[None, :] >= N_META)
        dist = jnp.abs(qpos[:, None] - kpos[None, :]).astype(jnp.float32)
        alibi = -slopes[:, None, None] * jnp.where(both_real, dist, 0.0)[None]

        qd = lax.dynamic_slice_in_dim(dq, start, qn, axis=1)
        s = jnp.einsum('bqhcd,bkhcd->bhcqk', qd, dk).astype(jnp.float32) * diff_scale
        p = jax.nn.softmax(s + alibi[None, :, None], axis=-1)
        a = p[:, :, 0] - lam * p[:, :, 1]
        od = jnp.einsum('bhqk,bkhe->bqhe', a.astype(dv.dtype), dv)
        od = rms_norm(od, subln) * out_scale

        qm = lax.dynamic_slice_in_dim(mq, start, qn, axis=1)
        s2 = jnp.einsum('bqhd,bkhd->bhqk', qm, mk).astype(jnp.float32) * mla_scale
        p2 = jax.nn.softmax(s2, axis=-1)
        om = jnp.einsum('bhqk,bkhe->bqhe', p2.astype(mv.dtype), mv)
        return od.reshape(B, qn, DIFF_WIDTH), om.reshape(B, qn, MLA_WIDTH)

    od_meta, om_meta = attend_block(0, N_META)
    n_blk = S // Q_BLOCK
    od_blk, om_blk = lax.map(lambda i: attend_block(N_META + i * Q_BLOCK, Q_BLOCK),
                             jnp.arange(n_blk))
    od_real = jnp.moveaxis(od_blk, 0, 1).reshape(B, S, DIFF_WIDTH)
    om_real = jnp.moveaxis(om_blk, 0, 1).reshape(B, S, MLA_WIDTH)
    return (jnp.concatenate([od_meta, od_real], axis=1),
            jnp.concatenate([om_meta, om_real], axis=1))


def reference(x, meta_tokens, ffn1_norm, ffn1_w_gate, ffn1_w_up, ffn1_w_down, mix_norm, w_in,
              diff_lambda_q1, diff_lambda_k1, diff_lambda_q2, diff_lambda_k2, diff_subln,
              mla_q_norm, mla_w_uq, mla_kv_norm, mla_w_ukv, w_gate, b_gate,
              w_branch_diff, w_branch_mla, w_out, ffn2_norm, ffn2_w_gate, ffn2_w_up,
              ffn2_w_down, final_norm):
    B = x.shape[0]
    meta = jnp.broadcast_to(meta_tokens[None].astype(x.dtype), (B, N_META, x.shape[-1]))
    h_stream = jnp.concatenate([meta, x], axis=1)

    for l in range(DEPTH):
        lam_init = 0.8 - 0.6 * math.exp(-0.3 * l)
        h_stream = h_stream + 0.5 * swiglu(rms_norm(h_stream, ffn1_norm[l]),
                                           ffn1_w_gate[l], ffn1_w_up[l], ffn1_w_down[l])
        h = rms_norm(h_stream, mix_norm[l])
        o_diff, o_mla = mixing_sublayer(
            h, w_in[l], diff_lambda_q1[l], diff_lambda_k1[l], diff_lambda_q2[l],
            diff_lambda_k2[l], diff_subln[l], mla_q_norm[l], mla_w_uq[l], mla_kv_norm[l],
            mla_w_ukv[l], lam_init)
        gates = jax.nn.sigmoid(h @ w_gate[l] + b_gate[l])
        g_diff, g_mla = jnp.split(gates, 2, axis=-1)
        merged = g_diff * (o_diff @ w_branch_diff[l]) + g_mla * (o_mla @ w_branch_mla[l])
        h_stream = h_stream + merged @ w_out[l]
        h_stream = h_stream + 0.5 * swiglu(rms_norm(h_stream, ffn2_norm[l]),
                                           ffn2_w_gate[l], ffn2_w_up[l], ffn2_w_down[l])

    return rms_norm(h_stream, final_norm)[:, N_META:]
```

```python
import functools
import math

import numpy as np
import jax
import jax.numpy as jnp
from jax import lax
from jax.experimental import pallas as pl
from jax.experimental.pallas import tpu as pltpu

N_META = 16
EPS = 1e-6
DIFF_HEADS = 8
DIFF_HEAD_DIM = 128
DIFF_V_DIM = 256
MLA_HEADS = 16
Q_LORA = 1024
KV_LORA = 512
NOPE_D = 128
ROPE_D = 64
MLA_V_D = 128
ROPE_THETA = 10000.0

LANES = 128
MLA_SLOT = 256
META_PAD = 128
VMEM_CAP = 60000 * 1024
NEG_BIG = -1e30

F32 = jnp.float32
BF16 = jnp.bfloat16


def _nbytes(shape, dtype):
    return int(np.prod(shape)) * jnp.dtype(dtype).itemsize


def _params(semantics, block_bytes, scratch_bytes, temp_bytes):
    need = 2 * block_bytes + scratch_bytes + temp_bytes
    return pltpu.CompilerParams(dimension_semantics=semantics,
                                vmem_limit_bytes=int(min(VMEM_CAP, max(need, 16 * 2**20))))


def _rms(x, gain):
    return x * lax.rsqrt(jnp.mean(x * x, axis=-1, keepdims=True) + EPS) * gain


def _ffn_kernel(x_ref, g_ref, wg_ref, wu_ref, wd_ref, gf_ref, out_ref, h_scr, *, final_norm):
    f = pl.program_id(1)

    @pl.when(f == 0)
    def _():
        x = x_ref[...]
        h_scr[...] = _rms(x, g_ref[...]).astype(BF16)
        out_ref[...] = x

    h = h_scr[...]
    g = jnp.dot(h, wg_ref[...], preferred_element_type=F32)
    u = jnp.dot(h, wu_ref[...], preferred_element_type=F32)
    a = (0.5 * (g * jax.nn.sigmoid(g)) * u).astype(BF16)
    out_ref[...] += jnp.dot(a, wd_ref[...], preferred_element_type=F32)

    if final_norm:
        @pl.when(f == pl.num_programs(1) - 1)
        def _():
            out_ref[...] = _rms(out_ref[...], gf_ref[...])


def _ffn(x, gain, wg, wu, wd, final_gain, *, tm, tf, final_norm, name):
    m, d = x.shape
    ff = wg.shape[1]
    kern = functools.partial(_ffn_kernel, final_norm=final_norm)
    blocks = (2 * _nbytes((tm, d), F32) + 2 * _nbytes((d, tf), BF16) + _nbytes((tf, d), BF16))
    temps = 3 * _nbytes((tm, tf), F32) + _nbytes((tm, d), F32)
    return pl.pallas_call(
        kern,
        grid=(m // tm, ff // tf),
        in_specs=[
            pl.BlockSpec((tm, d), lambda i, f: (i, 0)),
            pl.BlockSpec((1, d), lambda i, f: (0, 0)),
            pl.BlockSpec((d, tf), lambda i, f: (0, f)),
            pl.BlockSpec((d, tf), lambda i, f: (0, f)),
            pl.BlockSpec((tf, d), lambda i, f: (f, 0)),
            pl.BlockSpec((1, d), lambda i, f: (0, 0)),
        ],
        out_specs=pl.BlockSpec((tm, d), lambda i, f: (i, 0)),
        out_shape=jax.ShapeDtypeStruct((m, d), F32),
        scratch_shapes=[pltpu.VMEM((tm, d), BF16)],
        compiler_params=_params(("parallel", "arbitrary"), blocks, _nbytes((tm, d), BF16), temps),
        name=name,
    )(x, gain, wg, wu, wd, final_gain)


def _norm_kernel(x_ref, g_ref, out_ref):
    out_ref[...] = _rms(x_ref[...], g_ref[...]).astype(out_ref.dtype)


def _norm(x, gain, *, tm, name):
    m, d = x.shape
    blocks = _nbytes((tm, d), F32) + _nbytes((tm, d), BF16)
    return pl.pallas_call(
        _norm_kernel,
        grid=(m // tm,),
        in_specs=[pl.BlockSpec((tm, d), lambda i: (i, 0)),
                  pl.BlockSpec((1, d), lambda i: (0, 0))],
        out_specs=pl.BlockSpec((tm, d), lambda i: (i, 0)),
        out_shape=jax.ShapeDtypeStruct((m, d), BF16),
        compiler_params=_params(("parallel",), blocks, 0, 2 * _nbytes((tm, d), F32)),
        name=name,
    )(x, gain)


def _mm_kernel(*refs, has_scale, has_bias, has_res):
    a_ref, w_ref = refs[0], refs[1]
    out_ref = refs[-1]
    extra = list(refs[2:-1])
    y = jnp.dot(a_ref[...], w_ref[...], preferred_element_type=F32)
    if has_scale:
        y = y * extra.pop(0)[...]
    if has_bias:
        y = jax.nn.sigmoid(y + extra.pop(0)[...])
    if has_res:
        y = y + extra.pop(0)[...]
    out_ref[...] = y.astype(out_ref.dtype)


def _mm(a, w, *, tm, tn, out_dtype, name, scale=None, bias=None, res=None):
    m, k = a.shape
    n = w.shape[1]
    ins = [a, w]
    in_specs = [pl.BlockSpec((tm, k), lambda i, j: (i, 0)),
                pl.BlockSpec((k, tn), lambda i, j: (0, j))]
    blocks = _nbytes((tm, k), a.dtype) + _nbytes((k, tn), w.dtype) + _nbytes((tm, tn), out_dtype)
    for vec in (scale, bias):
        if vec is not None:
            ins.append(vec)
            in_specs.append(pl.BlockSpec((1, tn), lambda i, j: (0, j)))
    if res is not None:
        ins.append(res)
        in_specs.append(pl.BlockSpec((tm, tn), lambda i, j: (i, j)))
        blocks += _nbytes((tm, tn), res.dtype)
    kern = functools.partial(_mm_kernel, has_scale=scale is not None,
                             has_bias=bias is not None, has_res=res is not None)
    return pl.pallas_call(
        kern,
        grid=(m // tm, n // tn),
        in_specs=in_specs,
        out_specs=pl.BlockSpec((tm, tn), lambda i, j: (i, j)),
        out_shape=jax.ShapeDtypeStruct((m, n), out_dtype),
        compiler_params=_params(("parallel", "arbitrary"), blocks, 0, 4 * _nbytes((tm, tn), F32)),
        name=name,
    )(*ins)


def _kpe_kernel(h_ref, w_ref, cos_ref, sin_ref, out_ref):
    y = jnp.dot(h_ref[...], w_ref[...], preferred_element_type=F32)
    out_ref[...] = (y[:, :LANES] * cos_ref[...] + y[:, LANES:] * sin_ref[...]).astype(out_ref.dtype)


def _kpe(h, w, cos, sin, *, tm, name):
    m, d = h.shape
    nt = cos.shape[0] // tm
    blocks = _nbytes((tm, d), BF16) + _nbytes((d, 2 * LANES), BF16) + 3 * _nbytes((tm, LANES), F32)
    return pl.pallas_call(
        _kpe_kernel,
        grid=(m // tm,),
        in_specs=[pl.BlockSpec((tm, d), lambda i: (i, 0)),
                  pl.BlockSpec((d, 2 * LANES), lambda i: (0, 0)),
                  pl.BlockSpec((tm, LANES), lambda i: (i % nt, 0)),
                  pl.BlockSpec((tm, LANES), lambda i: (i % nt, 0))],
        out_specs=pl.BlockSpec((tm, LANES), lambda i: (i, 0)),
        out_shape=jax.ShapeDtypeStruct((m, LANES), BF16),
        compiler_params=_params(("parallel",), blocks, 0, _nbytes((tm, 2 * LANES), F32)),
        name=name,
    )(h, w, cos, sin)


def _mlaq_kernel(c_ref, g_ref, w_ref, cos_ref, sin_ref, out_ref, cn_scr, *, heads, scale):
    @pl.when(pl.program_id(1) == 0)
    def _():
        cn_scr[...] = _rms(c_ref[...].astype(F32), g_ref[...]).astype(BF16)

    y = jnp.dot(cn_scr[...], w_ref[...], preferred_element_type=F32)
    cos = cos_ref[...]
    sin = sin_ref[...]
    rot0 = heads * MLA_SLOT
    for h in range(heads):
        s0 = h * MLA_SLOT
        out_ref[:, s0:s0 + NOPE_D] = (y[:, s0:s0 + NOPE_D] * scale).astype(out_ref.dtype)
        pe = y[:, s0 + NOPE_D:s0 + MLA_SLOT] * cos + y[:, rot0 + h * LANES:rot0 + (h + 1) * LANES] * sin
        out_ref[:, s0 + NOPE_D:s0 + MLA_SLOT] = (pe * scale).astype(out_ref.dtype)


def _mlaq(p, col_block, gain, w, cos, sin, *, tm, heads, scale, name):
    m = p.shape[0]
    k = gain.shape[1]
    wn = heads * (MLA_SLOT + LANES)
    groups = w.shape[1] // wn
    nt = cos.shape[0] // tm
    kern = functools.partial(_mlaq_kernel, heads=heads, scale=scale)
    blocks = (_nbytes((tm, k), BF16) + _nbytes((k, wn), BF16) + 2 * _nbytes((tm, LANES), F32)
              + _nbytes((tm, heads * MLA_SLOT), BF16))
    return pl.pallas_call(
        kern,
        grid=(m // tm, groups),
        in_specs=[pl.BlockSpec((tm, k), lambda i, j: (i, col_block)),
                  pl.BlockSpec((1, k), lambda i, j: (0, 0)),
                  pl.BlockSpec((k, wn), lambda i, j: (0, j)),
                  pl.BlockSpec((tm, LANES), lambda i, j: (i % nt, 0)),
                  pl.BlockSpec((tm, LANES), lambda i, j: (i % nt, 0))],
        out_specs=pl.BlockSpec((tm, heads * MLA_SLOT), lambda i, j: (i, j)),
        out_shape=jax.ShapeDtypeStruct((m, groups * heads * MLA_SLOT), BF16),
        scratch_shapes=[pltpu.VMEM((tm, k), BF16)],
        compiler_params=_params(("parallel", "arbitrary"), blocks, _nbytes((tm, k), BF16),
                                2 * _nbytes((tm, wn), F32)),
        name=name,
    )(p, gain, w, cos, sin)


def _mlakv_kernel(c_ref, g_ref, w_ref, out_ref, cn_scr):
    @pl.when(pl.program_id(1) == 0)
    def _():
        cn_scr[...] = _rms(c_ref[...].astype(F32), g_ref[...]).astype(BF16)

    out_ref[...] = jnp.dot(cn_scr[...], w_ref[...], preferred_element_type=F32).astype(out_ref.dtype)


def _mlakv(p, col_block, gain, w, *, tm, tn, name):
    m = p.shape[0]
    k, n = w.shape
    blocks = _nbytes((tm, k), BF16) + _nbytes((k, tn), BF16) + _nbytes((tm, tn), BF16)
    return pl.pallas_call(
        _mlakv_kernel,
        grid=(m // tm, n // tn),
        in_specs=[pl.BlockSpec((tm, k), lambda i, j: (i, col_block)),
                  pl.BlockSpec((1, k), lambda i, j: (0, 0)),
                  pl.BlockSpec((k, tn), lambda i, j: (0, j))],
        out_specs=pl.BlockSpec((tm, tn), lambda i, j: (i, j)),
        out_shape=jax.ShapeDtypeStruct((m, n), BF16),
        scratch_shapes=[pltpu.VMEM((tm, k), BF16)],
        compiler_params=_params(("parallel", "arbitrary"), blocks, _nbytes((tm, k), BF16),
                                2 * _nbytes((tm, tn), F32)),
        name=name,
    )(p, gain, w)


def _dot_nt(a, b):
    return lax.dot_general(a, b, (((1,), (1,)), ((), ())), preferred_element_type=F32)


def _softmax_step(s, v, m_ref, l_ref, acc_ref):
    m_prev = m_ref[...]
    m_new = jnp.maximum(m_prev, jnp.max(s, axis=-1, keepdims=True))
    alpha = jnp.exp(m_prev - m_new)
    p = jnp.exp(s - m_new)
    l_ref[...] = alpha * l_ref[...] + jnp.sum(p, axis=-1, keepdims=True)
    acc_ref[...] = alpha * acc_ref[...] + jnp.dot(p.astype(BF16), v, preferred_element_type=F32)
    m_ref[...] = m_new


def _softmax_init(m_ref, l_ref, acc_ref):
    m_ref[...] = jnp.full(m_ref.shape, NEG_BIG, F32)
    l_ref[...] = jnp.zeros(l_ref.shape, F32)
    acc_ref[...] = jnp.zeros(acc_ref.shape, F32)


def _meta_mask(s):
    col = lax.broadcasted_iota(jnp.int32, s.shape, 1)
    return jnp.where(col < N_META, s, NEG_BIG)


def _diff_attn_kernel(q_ref, k_ref, v_ref, km_ref, vm_ref, slope_ref, lq1_ref, lk1_ref, lq2_ref,
                      lk2_ref, subln_ref, out_ref, m1, l1, acc1, m2, l2, acc2,
                      *, tk, lam_init):
    tq = q_ref.shape[0]
    hd = DIFF_HEAD_DIM
    q0 = pl.program_id(2) * tq
    q1 = q_ref[:, :hd]
    q2 = q_ref[:, hd:]
    neg_slope = -slope_ref[...]

    _softmax_init(m1, l1, acc1)
    _softmax_init(m2, l2, acc2)

    vm = vm_ref[...]
    _softmax_step(_meta_mask(_dot_nt(q1, km_ref[:, :hd])), vm, m1, l1, acc1)
    _softmax_step(_meta_mask(_dot_nt(q2, km_ref[:, hd:])), vm, m2, l2, acc2)

    rel = (lax.broadcasted_iota(jnp.int32, (tq, tk), 0)
           - lax.broadcasted_iota(jnp.int32, (tq, tk), 1))

    def body(j, carry):
        k0 = pl.multiple_of(j * tk, tk)
        kb = k_ref[pl.ds(k0, tk), :]
        vb = v_ref[pl.ds(k0, tk), :]
        alibi = jnp.abs(rel + (q0 - k0)).astype(F32) * neg_slope
        _softmax_step(_dot_nt(q1, kb[:, :hd]) + alibi, vb, m1, l1, acc1)
        _softmax_step(_dot_nt(q2, kb[:, hd:]) + alibi, vb, m2, l2, acc2)
        return carry

    lax.fori_loop(0, k_ref.shape[0] // tk, body, 0)

    lam = (jnp.exp(jnp.sum(lq1_ref[...] * lk1_ref[...], axis=-1, keepdims=True))
           - jnp.exp(jnp.sum(lq2_ref[...] * lk2_ref[...], axis=-1, keepdims=True)) + lam_init)
    o = acc1[...] / l1[...] - lam * (acc2[...] / l2[...])
    out_ref[...] = (_rms(o, subln_ref[...]) * (1.0 - lam_init)).astype(out_ref.dtype)


def _diff_attn(p, pm, slopes, lq1, lk1, lq2, lk2, subln, *, batch, seq, tq, tk, lam_init, name):
    m = p.shape[0]
    hw = 2 * DIFF_HEAD_DIM
    nq = seq // tq
    koff = DIFF_HEADS
    voff = 2 * DIFF_HEADS
    kern = functools.partial(_diff_attn_kernel, tk=tk, lam_init=lam_init)
    vec = pl.BlockSpec((1, DIFF_HEAD_DIM), lambda b, h, i: (0, 0))
    blocks = (2 * _nbytes((tq, hw), BF16) + 2 * _nbytes((seq, hw), BF16)
              + 2 * _nbytes((META_PAD, hw), BF16))
    scratch = 2 * (2 * _nbytes((tq, LANES), F32) + _nbytes((tq, hw), F32))
    return pl.pallas_call(
        kern,
        grid=(batch, DIFF_HEADS, nq),
        in_specs=[pl.BlockSpec((tq, hw), lambda b, h, i: (b * nq + i, h)),
                  pl.BlockSpec((seq, hw), lambda b, h, i: (b, koff + h)),
                  pl.BlockSpec((seq, hw), lambda b, h, i: (b, voff + h)),
                  pl.BlockSpec((META_PAD, hw), lambda b, h, i: (0, koff + h)),
                  pl.BlockSpec((META_PAD, hw), lambda b, h, i: (0, voff + h)),
                  pl.BlockSpec((None, 1, 1), lambda b, h, i: (h, 0, 0)),
                  vec, vec, vec, vec,
                  pl.BlockSpec((1, hw), lambda b, h, i: (0, 0))],
        out_specs=pl.BlockSpec((tq, hw), lambda b, h, i: (b * nq + i, h)),
        out_shape=jax.ShapeDtypeStruct((m, DIFF_HEADS * hw), BF16),
        scratch_shapes=[pltpu.VMEM((tq, 1), F32), pltpu.VMEM((tq, 1), F32), pltpu.VMEM((tq, hw), F32),
                        pltpu.VMEM((tq, 1), F32), pltpu.VMEM((tq, 1), F32), pltpu.VMEM((tq, hw), F32)],
        compiler_params=_params(("parallel", "parallel", "arbitrary"), blocks, scratch,
                                8 * _nbytes((tq, tk), F32)),
        name=name,
    )(p, p, p, pm, pm, slopes, lq1, lk1, lq2, lk2, subln)


def _mla_attn_kernel(q_ref, kv_ref, kpe_ref, kvm_ref, kpem_ref, out_ref, m, l, acc, *, tk):
    qn = q_ref[:, :NOPE_D]
    qp = q_ref[:, NOPE_D:]
    _softmax_init(m, l, acc)

    s = _dot_nt(qn, kvm_ref[:, :NOPE_D]) + _dot_nt(qp, kpem_ref[...])
    _softmax_step(_meta_mask(s), kvm_ref[:, NOPE_D:], m, l, acc)

    def body(j, carry):
        k0 = pl.multiple_of(j * tk, tk)
        kvb = kv_ref[pl.ds(k0, tk), :]
        s = _dot_nt(qn, kvb[:, :NOPE_D]) + _dot_nt(qp, kpe_ref[pl.ds(k0, tk), :])
        _softmax_step(s, kvb[:, NOPE_D:], m, l, acc)
        return carry

    lax.fori_loop(0, kv_ref.shape[0] // tk, body, 0)
    out_ref[...] = (acc[...] / l[...]).astype(out_ref.dtype)


def _mla_attn(q, kv, kpe, kvm, kpem, *, batch, seq, tq, tk, name):
    mrows = q.shape[0]
    nq = seq // tq
    kern = functools.partial(_mla_attn_kernel, tk=tk)
    blocks = (_nbytes((tq, MLA_SLOT), BF16) + _nbytes((seq, MLA_SLOT), BF16)
              + _nbytes((seq, LANES), BF16) + _nbytes((META_PAD, MLA_SLOT + LANES), BF16)
              + _nbytes((tq, MLA_V_D), BF16))
    scratch = 2 * _nbytes((tq, LANES), F32) + _nbytes((tq, MLA_V_D), F32)
    return pl.pallas_call(
        kern,
        grid=(batch, MLA_HEADS, nq),
        in_specs=[pl.BlockSpec((tq, MLA_SLOT), lambda b, h, i: (b * nq + i, h)),
                  pl.BlockSpec((seq, MLA_SLOT), lambda b, h, i: (b, h)),
                  pl.BlockSpec((seq, LANES), lambda b, h, i: (b, 0)),
                  pl.BlockSpec((META_PAD, MLA_SLOT), lambda b, h, i: (0, h)),
                  pl.BlockSpec((META_PAD, LANES), lambda b, h, i: (0, 0))],
        out_specs=pl.BlockSpec((tq, MLA_V_D), lambda b, h, i: (b * nq + i, h)),
        out_shape=jax.ShapeDtypeStruct((mrows, MLA_HEADS * MLA_V_D), BF16),
        scratch_shapes=[pltpu.VMEM((tq, 1), F32), pltpu.VMEM((tq, 1), F32),
                        pltpu.VMEM((tq, MLA_V_D), F32)],
        compiler_params=_params(("parallel", "parallel", "arbitrary"), blocks, scratch,
                                6 * _nbytes((tq, tk), F32)),
        name=name,
    )(q, kv, kpe, kvm, kpem)


def _merge_kernel(od_ref, om_ref, wa_ref, wb_ref, gd_ref, gm_ref, out_ref):
    ya = jnp.dot(od_ref[...], wa_ref[...], preferred_element_type=F32)
    yb = jnp.dot(om_ref[...], wb_ref[...], preferred_element_type=F32)
    out_ref[...] = (gd_ref[...].astype(F32) * ya + gm_ref[...].astype(F32) * yb).astype(out_ref.dtype)


def _merge(od, om, wa, wb, gates, *, tm, tn, name):
    m, k = od.shape
    n = wa.shape[1]
    nj = n // tn
    blocks = 2 * _nbytes((tm, k), BF16) + 2 * _nbytes((k, tn), BF16) + 3 * _nbytes((tm, tn), BF16)
    return pl.pallas_call(
        _merge_kernel,
        grid=(m // tm, nj),
        in_specs=[pl.BlockSpec((tm, k), lambda i, j: (i, 0)),
                  pl.BlockSpec((tm, k), lambda i, j: (i, 0)),
                  pl.BlockSpec((k, tn), lambda i, j: (0, j)),
                  pl.BlockSpec((k, tn), lambda i, j: (0, j)),
                  pl.BlockSpec((tm, tn), lambda i, j: (i, j)),
                  pl.BlockSpec((tm, tn), lambda i, j: (i, nj + j))],
        out_specs=pl.BlockSpec((tm, tn), lambda i, j: (i, j)),
        out_shape=jax.ShapeDtypeStruct((m, n), BF16),
        compiler_params=_params(("parallel", "arbitrary"), blocks, 0, 4 * _nbytes((tm, tn), F32)),
        name=name,
    )(od, om, wa, wb, gates, gates)


def _rope_tables(t):
    inv_freq = 1.0 / (ROPE_THETA ** (jnp.arange(0, ROPE_D, 2, dtype=F32) / ROPE_D))
    ang = jnp.arange(t).astype(F32)[:, None] * inv_freq[None, :]
    pad = jnp.zeros((t, LANES - ROPE_D), F32)
    cos = jnp.concatenate([jnp.cos(ang), jnp.cos(ang), pad], axis=-1)
    sin = jnp.concatenate([jnp.sin(ang), jnp.sin(ang), pad], axis=-1)
    return cos, sin


def _rotate_cols(w):
    half = ROPE_D // 2
    return jnp.concatenate([-w[..., half:], w[..., :half]], axis=-1)


def kernel(x, meta_tokens, ffn1_norm, ffn1_w_gate, ffn1_w_up, ffn1_w_down, mix_norm, w_in, diff_lambda_q1, diff_lambda_k1, diff_lambda_q2, diff_lambda_k2, diff_subln, mla_q_norm, mla_w_uq, mla_kv_norm, mla_w_ukv, w_gate, b_gate, w_branch_diff, w_branch_mla, w_out, ffn2_norm, ffn2_w_gate, ffn2_w_up, ffn2_w_down, final_norm):
    batch, seq, d = x.shape
    depth = ffn1_norm.shape[0]
    m = batch * seq
    dq_w = DIFF_HEADS * 2 * DIFF_HEAD_DIM
    dv_w = DIFF_HEADS * DIFF_V_DIM
    main_w = 2 * dq_w + dv_w + Q_LORA + KV_LORA
    cq_block = (2 * dq_w + dv_w) // Q_LORA
    ckv_block = (2 * dq_w + dv_w + Q_LORA) // KV_LORA
    q_heads = 4

    cos, sin = _rope_tables(N_META + seq)
    cos_m, sin_m, cos_r, sin_r = cos[:N_META], sin[:N_META], cos[N_META:], sin[N_META:]
    slopes = jnp.asarray([2.0 ** (-8.0 * (h + 1) / DIFF_HEADS) for h in range(DIFF_HEADS)],
                         F32).reshape(DIFF_HEADS, 1, 1)
    col_scale = jnp.concatenate([jnp.full((1, dq_w), DIFF_HEAD_DIM ** -0.5, F32),
                                 jnp.ones((1, main_w - dq_w), F32)], axis=-1)
    row = lambda v: v.reshape(1, -1).astype(F32)

    xs = x.reshape(m, d)
    xm = meta_tokens.astype(x.dtype)

    for l in range(depth):
        lam_init = 0.8 - 0.6 * math.exp(-0.3 * l)
        wg1, wu1, wd1 = (w[l].astype(BF16) for w in (ffn1_w_gate, ffn1_w_up, ffn1_w_down))
        wg2, wu2, wd2 = (w[l].astype(BF16) for w in (ffn2_w_gate, ffn2_w_up, ffn2_w_down))
        w_main = w_in[l][:, :main_w].astype(BF16)
        w_kr = w_in[l][:, main_w:]
        zpad = jnp.zeros((d, LANES - ROPE_D), F32)
        w_kpe = jnp.concatenate([w_kr, zpad, _rotate_cols(w_kr), zpad], axis=-1).astype(BF16)

        wuq = mla_w_uq[l].reshape(Q_LORA, MLA_HEADS, NOPE_D + ROPE_D)
        zq = jnp.zeros((Q_LORA, MLA_HEADS, MLA_SLOT - NOPE_D - ROPE_D), F32)
        w_a = jnp.concatenate([wuq, zq], axis=-1).reshape(Q_LORA, MLA_HEADS // q_heads, q_heads * MLA_SLOT)
        w_b = jnp.concatenate([_rotate_cols(wuq[..., NOPE_D:]),
                               jnp.zeros((Q_LORA, MLA_HEADS, LANES - ROPE_D), F32)], axis=-1)
        w_b = w_b.reshape(Q_LORA, MLA_HEADS // q_heads, q_heads * LANES)
        w_q = jnp.concatenate([w_a, w_b], axis=-1).reshape(Q_LORA, -1).astype(BF16)
        w_ukv = mla_w_ukv[l].astype(BF16)
        w_g = w_gate[l].astype(BF16)
        w_pa = w_branch_diff[l].astype(BF16)
        w_pb = w_branch_mla[l].astype(BF16)
        w_o = w_out[l].astype(BF16)

        xs = _ffn(xs, row(ffn1_norm[l]), wg1, wu1, wd1, row(final_norm),
                  tm=512, tf=256, final_norm=False, name="ffn1")
        xm = _ffn(xm, row(ffn1_norm[l]), wg1, wu1, wd1, row(final_norm),
                  tm=N_META, tf=256, final_norm=False, name="ffn1_meta")

        hs = _norm(xs, row(mix_norm[l]), tm=512, name="mix_norm")
        hm = _norm(xm, row(mix_norm[l]), tm=N_META, name="mix_norm_meta")

        ps = _mm(hs, w_main, tm=1024, tn=768, out_dtype=BF16, scale=col_scale, name="proj")
        pm = _mm(hm, w_main, tm=N_META, tn=768, out_dtype=BF16, scale=col_scale, name="proj_meta")
        gates = _mm(hs, w_g, tm=1024, tn=512, out_dtype=BF16, bias=row(b_gate[l]), name="gates")
        kpe_s = _kpe(hs, w_kpe, cos_r, sin_r, tm=1024, name="kpe")
        kpe_m = _kpe(hm, w_kpe, cos_m, sin_m, tm=N_META, name="kpe_meta")

        q_mla = _mlaq(ps, cq_block, row(mla_q_norm[l]), w_q, cos_r, sin_r, tm=512, heads=q_heads,
                      scale=(NOPE_D + ROPE_D) ** -0.5, name="mla_q")
        kv_s = _mlakv(ps, ckv_block, row(mla_kv_norm[l]), w_ukv, tm=1024, tn=1024, name="mla_kv")
        kv_m = _mlakv(pm, ckv_block, row(mla_kv_norm[l]), w_ukv, tm=N_META, tn=1024, name="mla_kv_meta")

        pad_rows = lambda a: jnp.pad(a, ((0, META_PAD - N_META), (0, 0)))
        o_diff = _diff_attn(ps, pad_rows(pm), slopes, row(diff_lambda_q1[l]), row(diff_lambda_k1[l]),
                            row(diff_lambda_q2[l]), row(diff_lambda_k2[l]), row(diff_subln[l]),
                            batch=batch, seq=seq, tq=512, tk=512, lam_init=lam_init, name="diff_attn")
        o_mla = _mla_attn(q_mla, kv_s, kpe_s, pad_rows(kv_m), pad_rows(kpe_m),
                          batch=batch, seq=seq, tq=512, tk=512, name="mla_attn")

        merged = _merge(o_diff, o_mla, w_pa, w_pb, gates, tm=1024, tn=512, name="merge")
        xs = _mm(merged, w_o, tm=1024, tn=512, out_dtype=F32, res=xs, name="out_proj")

        last = l == depth - 1
        xs = _ffn(xs, row(ffn2_norm[l]), wg2, wu2, wd2, row(final_norm),
                  tm=512, tf=256, final_norm=last, name="ffn2")
        if not last:
            raise NotImplementedError("DEPTH > 1 needs the meta-row query path")

    return xs.reshape(batch, seq, d)
```

```python
import functools
import math

import numpy as np
import jax
import jax.numpy as jnp
from jax import lax
from jax.experimental import pallas as pl
from jax.experimental.pallas import tpu as pltpu

N_META = 16
EPS = 1e-6
DIFF_HEADS = 8
DIFF_HEAD_DIM = 128
DIFF_V_DIM = 256
MLA_HEADS = 16
Q_LORA = 1024
KV_LORA = 512
NOPE_D = 128
ROPE_D = 64
MLA_V_D = 128
ROPE_THETA = 10000.0

LANES = 128
MLA_SLOT = 256
META_PAD = 128
VMEM_CAP = 60000 * 1024
NEG_BIG = -1e30
LOG2E = math.log2(math.e)

F32 = jnp.float32
BF16 = jnp.bfloat16


def _nbytes(shape, dtype):
    return int(np.prod(shape)) * jnp.dtype(dtype).itemsize


def _params(semantics, block_bytes, scratch_bytes, temp_bytes):
    need = 2 * block_bytes + scratch_bytes + temp_bytes
    return pltpu.CompilerParams(dimension_semantics=semantics,
                                vmem_limit_bytes=int(min(VMEM_CAP, max(need, 16 * 2**20))))


def _rms(x, gain):
    return x * lax.rsqrt(jnp.mean(x * x, axis=-1, keepdims=True) + EPS) * gain


def _ffn_kernel(x_ref, g_ref, wg_ref, wu_ref, wd_ref, gf_ref, out_ref, h_scr, *, final_norm):
    f = pl.program_id(1)

    @pl.when(f == 0)
    def _():
        x = x_ref[...]
        h_scr[...] = _rms(x, g_ref[...]).astype(BF16)
        out_ref[...] = x

    h = h_scr[...]
    g = jnp.dot(h, wg_ref[...], preferred_element_type=F32)
    u = jnp.dot(h, wu_ref[...], preferred_element_type=F32)
    a = (0.5 * (g * jax.nn.sigmoid(g)) * u).astype(BF16)
    out_ref[...] += jnp.dot(a, wd_ref[...], preferred_element_type=F32)

    if final_norm:
        @pl.when(f == pl.num_programs(1) - 1)
        def _():
            out_ref[...] = _rms(out_ref[...], gf_ref[...])


def _ffn(x, gain, wg, wu, wd, final_gain, *, tm, tf, final_norm, name):
    m, d = x.shape
    ff = wg.shape[1]
    kern = functools.partial(_ffn_kernel, final_norm=final_norm)
    blocks = (2 * _nbytes((tm, d), F32) + 2 * _nbytes((d, tf), BF16) + _nbytes((tf, d), BF16))
    temps = 3 * _nbytes((tm, tf), F32) + _nbytes((tm, d), F32)
    return pl.pallas_call(
        kern,
        grid=(m // tm, ff // tf),
        in_specs=[
            pl.BlockSpec((tm, d), lambda i, f: (i, 0)),
            pl.BlockSpec((1, d), lambda i, f: (0, 0)),
            pl.BlockSpec((d, tf), lambda i, f: (0, f)),
            pl.BlockSpec((d, tf), lambda i, f: (0, f)),
            pl.BlockSpec((tf, d), lambda i, f: (f, 0)),
            pl.BlockSpec((1, d), lambda i, f: (0, 0)),
        ],
        out_specs=pl.BlockSpec((tm, d), lambda i, f: (i, 0)),
        out_shape=jax.ShapeDtypeStruct((m, d), F32),
        scratch_shapes=[pltpu.VMEM((tm, d), BF16)],
        compiler_params=_params(("parallel", "arbitrary"), blocks, _nbytes((tm, d), BF16), temps),
        name=name,
    )(x, gain, wg, wu, wd, final_gain)


def _norm_kernel(x_ref, g_ref, out_ref):
    out_ref[...] = _rms(x_ref[...], g_ref[...]).astype(out_ref.dtype)


def _norm(x, gain, *, tm, name):
    m, d = x.shape
    blocks = _nbytes((tm, d), F32) + _nbytes((tm, d), BF16)
    return pl.pallas_call(
        _norm_kernel,
        grid=(m // tm,),
        in_specs=[pl.BlockSpec((tm, d), lambda i: (i, 0)),
                  pl.BlockSpec((1, d), lambda i: (0, 0))],
        out_specs=pl.BlockSpec((tm, d), lambda i: (i, 0)),
        out_shape=jax.ShapeDtypeStruct((m, d), BF16),
        compiler_params=_params(("parallel",), blocks, 0, 2 * _nbytes((tm, d), F32)),
        name=name,
    )(x, gain)


def _mm_kernel(*refs, has_scale, has_bias, has_res):
    a_ref, w_ref = refs[0], refs[1]
    out_ref = refs[-1]
    extra = list(refs[2:-1])
    y = jnp.dot(a_ref[...], w_ref[...], preferred_element_type=F32)
    if has_scale:
        y = y * extra.pop(0)[...]
    if has_bias:
        y = jax.nn.sigmoid(y + extra.pop(0)[...])
    if has_res:
        y = y + extra.pop(0)[...]
    out_ref[...] = y.astype(out_ref.dtype)


def _mm(a, w, *, tm, tn, out_dtype, name, scale=None, bias=None, res=None):
    m, k = a.shape
    n = w.shape[1]
    ins = [a, w]
    in_specs = [pl.BlockSpec((tm, k), lambda i, j: (i, 0)),
                pl.BlockSpec((k, tn), lambda i, j: (0, j))]
    blocks = _nbytes((tm, k), a.dtype) + _nbytes((k, tn), w.dtype) + _nbytes((tm, tn), out_dtype)
    for vec in (scale, bias):
        if vec is not None:
            ins.append(vec)
            in_specs.append(pl.BlockSpec((1, tn), lambda i, j: (0, j)))
    if res is not None:
        ins.append(res)
        in_specs.append(pl.BlockSpec((tm, tn), lambda i, j: (i, j)))
        blocks += _nbytes((tm, tn), res.dtype)
    kern = functools.partial(_mm_kernel, has_scale=scale is not None,
                             has_bias=bias is not None, has_res=res is not None)
    return pl.pallas_call(
        kern,
        grid=(m // tm, n // tn),
        in_specs=in_specs,
        out_specs=pl.BlockSpec((tm, tn), lambda i, j: (i, j)),
        out_shape=jax.ShapeDtypeStruct((m, n), out_dtype),
        compiler_params=_params(("parallel", "arbitrary"), blocks, 0, 4 * _nbytes((tm, tn), F32)),
        name=name,
    )(*ins)


def _kpe_kernel(h_ref, w_ref, cos_ref, sin_ref, out_ref):
    y = jnp.dot(h_ref[...], w_ref[...], preferred_element_type=F32)
    out_ref[...] = (y[:, :LANES] * cos_ref[...] + y[:, LANES:] * sin_ref[...]).astype(out_ref.dtype)


def _kpe(h, w, cos, sin, *, tm, name):
    m, d = h.shape
    nt = cos.shape[0] // tm
    blocks = _nbytes((tm, d), BF16) + _nbytes((d, 2 * LANES), BF16) + 3 * _nbytes((tm, LANES), F32)
    return pl.pallas_call(
        _kpe_kernel,
        grid=(m // tm,),
        in_specs=[pl.BlockSpec((tm, d), lambda i: (i, 0)),
                  pl.BlockSpec((d, 2 * LANES), lambda i: (0, 0)),
                  pl.BlockSpec((tm, LANES), lambda i: (i % nt, 0)),
                  pl.BlockSpec((tm, LANES), lambda i: (i % nt, 0))],
        out_specs=pl.BlockSpec((tm, LANES), lambda i: (i, 0)),
        out_shape=jax.ShapeDtypeStruct((m, LANES), BF16),
        compiler_params=_params(("parallel",), blocks, 0, _nbytes((tm, 2 * LANES), F32)),
        name=name,
    )(h, w, cos, sin)


def _mlaq_kernel(c_ref, g_ref, w_ref, cos_ref, sin_ref, out_ref, cn_scr, *, heads, scale):
    @pl.when(pl.program_id(1) == 0)
    def _():
        cn_scr[...] = _rms(c_ref[...].astype(F32), g_ref[...]).astype(BF16)

    y = jnp.dot(cn_scr[...], w_ref[...], preferred_element_type=F32)
    cos = cos_ref[...]
    sin = sin_ref[...]
    rot0 = heads * MLA_SLOT
    for h in range(heads):
        s0 = h * MLA_SLOT
        out_ref[:, s0:s0 + NOPE_D] = (y[:, s0:s0 + NOPE_D] * scale).astype(out_ref.dtype)
        pe = y[:, s0 + NOPE_D:s0 + MLA_SLOT] * cos + y[:, rot0 + h * LANES:rot0 + (h + 1) * LANES] * sin
        out_ref[:, s0 + NOPE_D:s0 + MLA_SLOT] = (pe * scale).astype(out_ref.dtype)


def _mlaq(p, col_block, gain, w, cos, sin, *, tm, heads, scale, name):
    m = p.shape[0]
    k = gain.shape[1]
    wn = heads * (MLA_SLOT + LANES)
    groups = w.shape[1] // wn
    nt = cos.shape[0] // tm
    kern = functools.partial(_mlaq_kernel, heads=heads, scale=scale)
    blocks = (_nbytes((tm, k), BF16) + _nbytes((k, wn), BF16) + 2 * _nbytes((tm, LANES), F32)
              + _nbytes((tm, heads * MLA_SLOT), BF16))
    return pl.pallas_call(
        kern,
        grid=(m // tm, groups),
        in_specs=[pl.BlockSpec((tm, k), lambda i, j: (i, col_block)),
                  pl.BlockSpec((1, k), lambda i, j: (0, 0)),
                  pl.BlockSpec((k, wn), lambda i, j: (0, j)),
                  pl.BlockSpec((tm, LANES), lambda i, j: (i % nt, 0)),
                  pl.BlockSpec((tm, LANES), lambda i, j: (i % nt, 0))],
        out_specs=pl.BlockSpec((tm, heads * MLA_SLOT), lambda i, j: (i, j)),
        out_shape=jax.ShapeDtypeStruct((m, groups * heads * MLA_SLOT), BF16),
        scratch_shapes=[pltpu.VMEM((tm, k), BF16)],
        compiler_params=_params(("parallel", "arbitrary"), blocks, _nbytes((tm, k), BF16),
                                2 * _nbytes((tm, wn), F32)),
        name=name,
    )(p, gain, w, cos, sin)


def _mlakv_kernel(c_ref, g_ref, w_ref, out_ref, cn_scr):
    @pl.when(pl.program_id(1) == 0)
    def _():
        cn_scr[...] = _rms(c_ref[...].astype(F32), g_ref[...]).astype(BF16)

    out_ref[...] = jnp.dot(cn_scr[...], w_ref[...], preferred_element_type=F32).astype(out_ref.dtype)


def _mlakv(p, col_block, gain, w, *, tm, tn, name):
    m = p.shape[0]
    k, n = w.shape
    blocks = _nbytes((tm, k), BF16) + _nbytes((k, tn), BF16) + _nbytes((tm, tn), BF16)
    return pl.pallas_call(
        _mlakv_kernel,
        grid=(m // tm, n // tn),
        in_specs=[pl.BlockSpec((tm, k), lambda i, j: (i, col_block)),
                  pl.BlockSpec((1, k), lambda i, j: (0, 0)),
                  pl.BlockSpec((k, tn), lambda i, j: (0, j))],
        out_specs=pl.BlockSpec((tm, tn), lambda i, j: (i, j)),
        out_shape=jax.ShapeDtypeStruct((m, n), BF16),
        scratch_shapes=[pltpu.VMEM((tm, k), BF16)],
        compiler_params=_params(("parallel", "arbitrary"), blocks, _nbytes((tm, k), BF16),
                                2 * _nbytes((tm, tn), F32)),
        name=name,
    )(p, gain, w)


def _dot_nt(a, b):
    return lax.dot_general(a, b, (((1,), (1,)), ((), ())), preferred_element_type=F32)


def _transpose_bf16(x):
    return x.astype(F32).T.astype(BF16)


def _softmax_first_t(s_t, v_t):
    m = jnp.max(s_t, axis=0, keepdims=True)
    p = jnp.exp2(s_t - m)
    return m, jnp.sum(p, axis=0, keepdims=True), jnp.dot(v_t, p.astype(BF16), preferred_element_type=F32)


def _softmax_next_t(s_t, shift, v_t, m, l, acc):
    m_new = jnp.maximum(m, jnp.max(s_t, axis=0, keepdims=True) + shift)
    alpha = jnp.exp2(m - m_new)
    p = jnp.exp2(s_t - (m_new - shift))
    l = alpha * l + jnp.sum(p, axis=0, keepdims=True)
    acc = alpha * acc + jnp.dot(v_t, p.astype(BF16), preferred_element_type=F32)
    return m_new, l, acc


def _meta_mask_t(s_t):
    row = lax.broadcasted_iota(jnp.int32, s_t.shape, 0)
    return jnp.where(row < N_META, s_t, NEG_BIG)


def _diff_attn_kernel(q_ref, k_ref, v_ref, km_ref, vm_ref, slope_ref, lq1_ref, lk1_ref, lq2_ref,
                      lk2_ref, subln_ref, out_ref, vt_scr, bias_scr, *, lam_init):
    tq = q_ref.shape[0]
    tk = tq
    nk = k_ref.shape[0] // tk
    hd = DIFF_HEAD_DIM
    qi = pl.program_id(2)
    slope2 = slope_ref[...] * LOG2E

    @pl.when(qi == 0)
    def _():
        vt_scr[:, :META_PAD] = _transpose_bf16(vm_ref[...])
        for c in range(nk):
            vt_scr[:, META_PAD + c * tk:META_PAD + (c + 1) * tk] = _transpose_bf16(
                v_ref[c * tk:(c + 1) * tk, :])
        d = (lax.broadcasted_iota(jnp.int32, (tk, tq), 1)
             - lax.broadcasted_iota(jnp.int32, (tk, tq), 0)).astype(F32) * slope2
        bias_scr[0] = -d
        bias_scr[1] = d
        bias_scr[2] = -jnp.abs(d)

    q1 = q_ref[:, :hd]
    q2 = q_ref[:, hd:]

    def qk(j):
        kb = k_ref[j * tk:(j + 1) * tk, :]
        return _dot_nt(kb[:, :hd], q1), _dot_nt(kb[:, hd:], q2)

    vtm = vt_scr[:, :META_PAD]
    st1 = _softmax_first_t(_meta_mask_t(_dot_nt(km_ref[:, :hd], q1)), vtm)
    st2 = _softmax_first_t(_meta_mask_t(_dot_nt(km_ref[:, hd:], q2)), vtm)

    s1, s2 = qk(0)
    for j in range(nk):
        nxt = qk(j + 1) if j + 1 < nk else None
        bias = bias_scr[jnp.where(j < qi, 0, jnp.where(j == qi, 2, 1))]
        shift = -jnp.abs(qi * tq - j * tk).astype(F32) * slope2
        vt = vt_scr[:, META_PAD + j * tk:META_PAD + (j + 1) * tk]
        st1 = _softmax_next_t(s1 + bias, shift, vt, *st1)
        st2 = _softmax_next_t(s2 + bias, shift, vt, *st2)
        if nxt is not None:
            s1, s2 = nxt

    lam = (jnp.exp(jnp.sum(lq1_ref[...] * lk1_ref[...], axis=-1, keepdims=True))
           - jnp.exp(jnp.sum(lq2_ref[...] * lk2_ref[...], axis=-1, keepdims=True)) + lam_init)
    o_t = st1[2] / st1[1] - lam * (st2[2] / st2[1])
    out_ref[...] = (_rms(o_t.T, subln_ref[...]) * (1.0 - lam_init)).astype(out_ref.dtype)


def _diff_attn(p, pm, slopes, lq1, lk1, lq2, lk2, subln, *, batch, seq, tq, lam_init, name):
    m = p.shape[0]
    hw = 2 * DIFF_HEAD_DIM
    nq = seq // tq
    koff = DIFF_HEADS
    voff = 2 * DIFF_HEADS
    kern = functools.partial(_diff_attn_kernel, lam_init=lam_init)
    vec = pl.BlockSpec((1, DIFF_HEAD_DIM), lambda b, h, i: (0, 0))
    blocks = (2 * _nbytes((tq, hw), BF16) + 2 * _nbytes((seq, hw), BF16)
              + 2 * _nbytes((META_PAD, hw), BF16))
    scratch = _nbytes((hw, seq + META_PAD), BF16) + 3 * _nbytes((tq, tq), F32)
    return pl.pallas_call(
        kern,
        grid=(batch, DIFF_HEADS, nq),
        in_specs=[pl.BlockSpec((tq, hw), lambda b, h, i: (b * nq + i, h)),
                  pl.BlockSpec((seq, hw), lambda b, h, i: (b, koff + h)),
                  pl.BlockSpec((seq, hw), lambda b, h, i: (b, voff + h)),
                  pl.BlockSpec((META_PAD, hw), lambda b, h, i: (0, koff + h)),
                  pl.BlockSpec((META_PAD, hw), lambda b, h, i: (0, voff + h)),
                  pl.BlockSpec((None, 1, 1), lambda b, h, i: (h, 0, 0)),
                  vec, vec, vec, vec,
                  pl.BlockSpec((1, hw), lambda b, h, i: (0, 0))],
        out_specs=pl.BlockSpec((tq, hw), lambda b, h, i: (b * nq + i, h)),
        out_shape=jax.ShapeDtypeStruct((m, DIFF_HEADS * hw), BF16),
        scratch_shapes=[pltpu.VMEM((hw, seq + META_PAD), BF16),
                        pltpu.VMEM((3, tq, tq), F32)],
        compiler_params=_params(("parallel", "parallel", "arbitrary"), blocks, scratch,
                                12 * _nbytes((tq, tq), F32)),
        name=name,
    )(p, p, p, pm, pm, slopes, lq1, lk1, lq2, lk2, subln)


def _mla_attn_kernel(q_ref, kv_ref, kpe_ref, kvm_ref, kpem_ref, out_ref, k_scr, vt_scr, *, tk):
    nk = kv_ref.shape[0] // tk

    @pl.when(pl.program_id(2) == 0)
    def _():
        k_scr[:META_PAD, :NOPE_D] = kvm_ref[:, :NOPE_D]
        k_scr[:META_PAD, NOPE_D:] = kpem_ref[...]
        k_scr[META_PAD:, :NOPE_D] = kv_ref[:, :NOPE_D]
        k_scr[META_PAD:, NOPE_D:] = kpe_ref[...]
        vt_scr[:, :META_PAD] = _transpose_bf16(kvm_ref[:, NOPE_D:])
        for c in range(nk):
            vt_scr[:, META_PAD + c * tk:META_PAD + (c + 1) * tk] = _transpose_bf16(
                kv_ref[c * tk:(c + 1) * tk, NOPE_D:])

    q = q_ref[...]

    def qk(j):
        return _dot_nt(k_scr[META_PAD + j * tk:META_PAD + (j + 1) * tk, :], q)

    st = _softmax_first_t(_meta_mask_t(_dot_nt(k_scr[:META_PAD, :], q)), vt_scr[:, :META_PAD])

    s = qk(0)
    for j in range(nk):
        nxt = qk(j + 1) if j + 1 < nk else None
        st = _softmax_next_t(s, 0.0, vt_scr[:, META_PAD + j * tk:META_PAD + (j + 1) * tk], *st)
        s = nxt

    out_ref[...] = (st[2] / st[1]).T.astype(out_ref.dtype)


def _mla_attn(q, kv, kpe, kvm, kpem, *, batch, seq, tq, tk, name):
    mrows = q.shape[0]
    nq = seq // tq
    kern = functools.partial(_mla_attn_kernel, tk=tk)
    blocks = (_nbytes((tq, MLA_SLOT), BF16) + _nbytes((seq, MLA_SLOT), BF16)
              + _nbytes((seq, LANES), BF16) + _nbytes((META_PAD, MLA_SLOT + LANES), BF16)
              + _nbytes((tq, MLA_V_D), BF16))
    scratch = _nbytes((seq + META_PAD, MLA_SLOT), BF16) + _nbytes((MLA_V_D, seq + META_PAD), BF16)
    return pl.pallas_call(
        kern,
        grid=(batch, MLA_HEADS, nq),
        in_specs=[pl.BlockSpec((tq, MLA_SLOT), lambda b, h, i: (b * nq + i, h)),
                  pl.BlockSpec((seq, MLA_SLOT), lambda b, h, i: (b, h)),
                  pl.BlockSpec((seq, LANES), lambda b, h, i: (b, 0)),
                  pl.BlockSpec((META_PAD, MLA_SLOT), lambda b, h, i: (0, h)),
                  pl.BlockSpec((META_PAD, LANES), lambda b, h, i: (0, 0))],
        out_specs=pl.BlockSpec((tq, MLA_V_D), lambda b, h, i: (b * nq + i, h)),
        out_shape=jax.ShapeDtypeStruct((mrows, MLA_HEADS * MLA_V_D), BF16),
        scratch_shapes=[pltpu.VMEM((seq + META_PAD, MLA_SLOT), BF16),
                        pltpu.VMEM((MLA_V_D, seq + META_PAD), BF16)],
        compiler_params=_params(("parallel", "parallel", "arbitrary"), blocks, scratch,
                                6 * _nbytes((tk, tq), F32)),
        name=name,
    )(q, kv, kpe, kvm, kpem)


def _merge_kernel(od_ref, om_ref, wa_ref, wb_ref, gd_ref, gm_ref, out_ref):
    ya = jnp.dot(od_ref[...], wa_ref[...], preferred_element_type=F32)
    yb = jnp.dot(om_ref[...], wb_ref[...], preferred_element_type=F32)
    out_ref[...] = (gd_ref[...].astype(F32) * ya + gm_ref[...].astype(F32) * yb).astype(out_ref.dtype)


def _merge(od, om, wa, wb, gates, *, tm, tn, name):
    m, k = od.shape
    n = wa.shape[1]
    nj = n // tn
    blocks = 2 * _nbytes((tm, k), BF16) + 2 * _nbytes((k, tn), BF16) + 3 * _nbytes((tm, tn), BF16)
    return pl.pallas_call(
        _merge_kernel,
        grid=(m // tm, nj),
        in_specs=[pl.BlockSpec((tm, k), lambda i, j: (i, 0)),
                  pl.BlockSpec((tm, k), lambda i, j: (i, 0)),
                  pl.BlockSpec((k, tn), lambda i, j: (0, j)),
                  pl.BlockSpec((k, tn), lambda i, j: (0, j)),
                  pl.BlockSpec((tm, tn), lambda i, j: (i, j)),
                  pl.BlockSpec((tm, tn), lambda i, j: (i, nj + j))],
        out_specs=pl.BlockSpec((tm, tn), lambda i, j: (i, j)),
        out_shape=jax.ShapeDtypeStruct((m, n), BF16),
        compiler_params=_params(("parallel", "arbitrary"), blocks, 0, 4 * _nbytes((tm, tn), F32)),
        name=name,
    )(od, om, wa, wb, gates, gates)


def _rope_tables(t):
    inv_freq = 1.0 / (ROPE_THETA ** (jnp.arange(0, ROPE_D, 2, dtype=F32) / ROPE_D))
    ang = jnp.arange(t).astype(F32)[:, None] * inv_freq[None, :]
    pad = jnp.zeros((t, LANES - ROPE_D), F32)
    cos = jnp.concatenate([jnp.cos(ang), jnp.cos(ang), pad], axis=-1)
    sin = jnp.concatenate([jnp.sin(ang), jnp.sin(ang), pad], axis=-1)
    return cos, sin


def _rotate_cols(w):
    half = ROPE_D // 2
    return jnp.concatenate([-w[..., half:], w[..., :half]], axis=-1)


def kernel(x, meta_tokens, ffn1_norm, ffn1_w_gate, ffn1_w_up, ffn1_w_down, mix_norm, w_in, diff_lambda_q1, diff_lambda_k1, diff_lambda_q2, diff_lambda_k2, diff_subln, mla_q_norm, mla_w_uq, mla_kv_norm, mla_w_ukv, w_gate, b_gate, w_branch_diff, w_branch_mla, w_out, ffn2_norm, ffn2_w_gate, ffn2_w_up, ffn2_w_down, final_norm):
    batch, seq, d = x.shape
    depth = ffn1_norm.shape[0]
    m = batch * seq
    dq_w = DIFF_HEADS * 2 * DIFF_HEAD_DIM
    dv_w = DIFF_HEADS * DIFF_V_DIM
    main_w = 2 * dq_w + dv_w + Q_LORA + KV_LORA
    cq_block = (2 * dq_w + dv_w) // Q_LORA
    ckv_block = (2 * dq_w + dv_w + Q_LORA) // KV_LORA
    q_heads = 4

    cos, sin = _rope_tables(N_META + seq)
    cos_m, sin_m, cos_r, sin_r = cos[:N_META], sin[:N_META], cos[N_META:], sin[N_META:]
    slopes = jnp.asarray([2.0 ** (-8.0 * (h + 1) / DIFF_HEADS) for h in range(DIFF_HEADS)],
                         F32).reshape(DIFF_HEADS, 1, 1)
    col_scale = jnp.concatenate([jnp.full((1, dq_w), DIFF_HEAD_DIM ** -0.5 * LOG2E, F32),
                                 jnp.ones((1, main_w - dq_w), F32)], axis=-1)
    row = lambda v: v.reshape(1, -1).astype(F32)

    xs = x.reshape(m, d)
    xm = meta_tokens.astype(x.dtype)

    for l in range(depth):
        lam_init = 0.8 - 0.6 * math.exp(-0.3 * l)
        wg1, wu1, wd1 = (w[l].astype(BF16) for w in (ffn1_w_gate, ffn1_w_up, ffn1_w_down))
        wg2, wu2, wd2 = (w[l].astype(BF16) for w in (ffn2_w_gate, ffn2_w_up, ffn2_w_down))
        w_main = w_in[l][:, :main_w].astype(BF16)
        w_kr = w_in[l][:, main_w:]
        zpad = jnp.zeros((d, LANES - ROPE_D), F32)
        w_kpe = jnp.concatenate([w_kr, zpad, _rotate_cols(w_kr), zpad], axis=-1).astype(BF16)

        wuq = mla_w_uq[l].reshape(Q_LORA, MLA_HEADS, NOPE_D + ROPE_D)
        zq = jnp.zeros((Q_LORA, MLA_HEADS, MLA_SLOT - NOPE_D - ROPE_D), F32)
        w_a = jnp.concatenate([wuq, zq], axis=-1).reshape(Q_LORA, MLA_HEADS // q_heads, q_heads * MLA_SLOT)
        w_b = jnp.concatenate([_rotate_cols(wuq[..., NOPE_D:]),
                               jnp.zeros((Q_LORA, MLA_HEADS, LANES - ROPE_D), F32)], axis=-1)
        w_b = w_b.reshape(Q_LORA, MLA_HEADS // q_heads, q_heads * LANES)
        w_q = jnp.concatenate([w_a, w_b], axis=-1).reshape(Q_LORA, -1).astype(BF16)
        w_ukv = mla_w_ukv[l].astype(BF16)
        w_g = w_gate[l].astype(BF16)
        w_pa = w_branch_diff[l].astype(BF16)
        w_pb = w_branch_mla[l].astype(BF16)
        w_o = w_out[l].astype(BF16)

        xs = _ffn(xs, row(ffn1_norm[l]), wg1, wu1, wd1, row(final_norm),
                  tm=512, tf=256, final_norm=False, name="ffn1")
        xm = _ffn(xm, row(ffn1_norm[l]), wg1, wu1, wd1, row(final_norm),
                  tm=N_META, tf=256, final_norm=False, name="ffn1_meta")

        hs = _norm(xs, row(mix_norm[l]), tm=512, name="mix_norm")
        hm = _norm(xm, row(mix_norm[l]), tm=N_META, name="mix_norm_meta")

        ps = _mm(hs, w_main, tm=1024, tn=768, out_dtype=BF16, scale=col_scale, name="proj")
        pm = _mm(hm, w_main, tm=N_META, tn=768, out_dtype=BF16, scale=col_scale, name="proj_meta")
        gates = _mm(hs, w_g, tm=1024, tn=512, out_dtype=BF16, bias=row(b_gate[l]), name="gates")
        kpe_s = _kpe(hs, w_kpe, cos_r, sin_r, tm=1024, name="kpe")
        kpe_m = _kpe(hm, w_kpe, cos_m, sin_m, tm=N_META, name="kpe_meta")

        q_mla = _mlaq(ps, cq_block, row(mla_q_norm[l]), w_q, cos_r, sin_r, tm=512, heads=q_heads,
                      scale=(NOPE_D + ROPE_D) ** -0.5 * LOG2E, name="mla_q")
        kv_s = _mlakv(ps, ckv_block, row(mla_kv_norm[l]), w_ukv, tm=1024, tn=1024, name="mla_kv")
        kv_m = _mlakv(pm, ckv_block, row(mla_kv_norm[l]), w_ukv, tm=N_META, tn=1024, name="mla_kv_meta")

        pad_rows = lambda a: jnp.pad(a, ((0, META_PAD - N_META), (0, 0)))
        o_diff = _diff_attn(ps, pad_rows(pm), slopes, row(diff_lambda_q1[l]), row(diff_lambda_k1[l]),
                            row(diff_lambda_q2[l]), row(diff_lambda_k2[l]), row(diff_subln[l]),
                            batch=batch, seq=seq, tq=512, lam_init=lam_init, name="diff_attn")
        o_mla = _mla_attn(q_mla, kv_s, kpe_s, pad_rows(kv_m), pad_rows(kpe_m),
                          batch=batch, seq=seq, tq=1024, tk=512, name="mla_attn")

        merged = _merge(o_diff, o_mla, w_pa, w_pb, gates, tm=1024, tn=512, name="merge")
        xs = _mm(merged, w_o, tm=1024, tn=512, out_dtype=F32, res=xs, name="out_proj")

        last = l == depth - 1
        xs = _ffn(xs, row(ffn2_norm[l]), wg2, wu2, wd2, row(final_norm),
                  tm=512, tf=256, final_norm=last, name="ffn2")
        if not last:
            raise NotImplementedError("DEPTH > 1 needs the meta-row query path")

    return xs.reshape(batch, seq, d)
```

```python
import functools
import math

import numpy as np
import jax
import jax.numpy as jnp
from jax import lax
from jax.experimental import pallas as pl
from jax.experimental.pallas import tpu as pltpu

N_META = 16
EPS = 1e-6
DIFF_HEADS = 8
DIFF_HEAD_DIM = 128
DIFF_V_DIM = 256
MLA_HEADS = 16
Q_LORA = 1024
KV_LORA = 512
NOPE_D = 128
ROPE_D = 64
MLA_V_D = 128
ROPE_THETA = 10000.0

LANES = 128
MLA_SLOT = 256
META_PAD = 128
VMEM_CAP = 60000 * 1024
NEG_BIG = -1e30
LOG2E = math.log2(math.e)

F32 = jnp.float32
BF16 = jnp.bfloat16


def _nbytes(shape, dtype):
    return int(np.prod(shape)) * jnp.dtype(dtype).itemsize


def _params(semantics, block_bytes, scratch_bytes, temp_bytes):
    need = 2 * block_bytes + scratch_bytes + temp_bytes
    return pltpu.CompilerParams(dimension_semantics=semantics,
                                vmem_limit_bytes=int(min(VMEM_CAP, max(need, 16 * 2**20))))


def _rms(x, gain):
    return x * lax.rsqrt(jnp.mean(x * x, axis=-1, keepdims=True) + EPS) * gain


def _ffn_kernel(x_ref, g_ref, wg_ref, wu_ref, wd_ref, gf_ref, out_ref, h_scr, a_scr, *, final_norm):
    f = pl.program_id(1)
    nf = pl.num_programs(1) - 1

    def gate_up():
        h = h_scr[...]
        g = jnp.dot(h, wg_ref[...], preferred_element_type=F32)
        u = jnp.dot(h, wu_ref[...], preferred_element_type=F32)
        return (0.5 * (g * jax.nn.sigmoid(g)) * u).astype(BF16)

    def down():
        out_ref[...] += jnp.dot(a_scr[(f + 1) % 2], wd_ref[...], preferred_element_type=F32)

    @pl.when(f == 0)
    def _():
        x = x_ref[...]
        h_scr[...] = _rms(x, g_ref[...]).astype(BF16)
        out_ref[...] = x
        a_scr[0] = gate_up()

    @pl.when((f > 0) & (f < nf))
    def _():
        a_new = gate_up()
        down()
        a_scr[f % 2] = a_new

    @pl.when(f == nf)
    def _():
        down()
        if final_norm:
            out_ref[...] = _rms(out_ref[...], gf_ref[...])


def _ffn(x, gain, wg, wu, wd, final_gain, *, tm, tf, final_norm, name):
    m, d = x.shape
    nf = wg.shape[1] // tf
    kern = functools.partial(_ffn_kernel, final_norm=final_norm)
    blocks = (2 * _nbytes((tm, d), F32) + 2 * _nbytes((d, tf), BF16) + _nbytes((tf, d), BF16))
    scratch = _nbytes((tm, d), BF16) + 2 * _nbytes((tm, tf), BF16)
    temps = 3 * _nbytes((tm, tf), F32) + _nbytes((tm, d), F32)
    return pl.pallas_call(
        kern,
        grid=(m // tm, nf + 1),
        in_specs=[
            pl.BlockSpec((tm, d), lambda i, f: (i, 0)),
            pl.BlockSpec((1, d), lambda i, f: (0, 0)),
            pl.BlockSpec((d, tf), lambda i, f: (0, jnp.minimum(f, nf - 1))),
            pl.BlockSpec((d, tf), lambda i, f: (0, jnp.minimum(f, nf - 1))),
            pl.BlockSpec((tf, d), lambda i, f: (jnp.maximum(f - 1, 0), 0)),
            pl.BlockSpec((1, d), lambda i, f: (0, 0)),
        ],
        out_specs=pl.BlockSpec((tm, d), lambda i, f: (i, 0)),
        out_shape=jax.ShapeDtypeStruct((m, d), F32),
        scratch_shapes=[pltpu.VMEM((tm, d), BF16), pltpu.VMEM((2, tm, tf), BF16)],
        compiler_params=_params(("parallel", "arbitrary"), blocks, scratch, temps),
        name=name,
    )(x, gain, wg, wu, wd, final_gain)


def _norm_kernel(x_ref, g_ref, out_ref):
    out_ref[...] = _rms(x_ref[...], g_ref[...]).astype(out_ref.dtype)


def _norm(x, gain, *, tm, name):
    m, d = x.shape
    blocks = _nbytes((tm, d), F32) + _nbytes((tm, d), BF16)
    return pl.pallas_call(
        _norm_kernel,
        grid=(m // tm,),
        in_specs=[pl.BlockSpec((tm, d), lambda i: (i, 0)),
                  pl.BlockSpec((1, d), lambda i: (0, 0))],
        out_specs=pl.BlockSpec((tm, d), lambda i: (i, 0)),
        out_shape=jax.ShapeDtypeStruct((m, d), BF16),
        compiler_params=_params(("parallel",), blocks, 0, 2 * _nbytes((tm, d), F32)),
        name=name,
    )(x, gain)


def _mm_kernel(*refs, has_scale, has_bias, has_res):
    a_ref, w_ref = refs[0], refs[1]
    out_ref = refs[-1]
    extra = list(refs[2:-1])
    y = jnp.dot(a_ref[...], w_ref[...], preferred_element_type=F32)
    if has_scale:
        y = y * extra.pop(0)[...]
    if has_bias:
        y = jax.nn.sigmoid(y + extra.pop(0)[...])
    if has_res:
        y = y + extra.pop(0)[...]
    out_ref[...] = y.astype(out_ref.dtype)


def _mm(a, w, *, tm, tn, out_dtype, name, scale=None, bias=None, res=None):
    m, k = a.shape
    n = w.shape[1]
    ins = [a, w]
    in_specs = [pl.BlockSpec((tm, k), lambda i, j: (i, 0)),
                pl.BlockSpec((k, tn), lambda i, j: (0, j))]
    blocks = _nbytes((tm, k), a.dtype) + _nbytes((k, tn), w.dtype) + _nbytes((tm, tn), out_dtype)
    for vec in (scale, bias):
        if vec is not None:
            ins.append(vec)
            in_specs.append(pl.BlockSpec((1, tn), lambda i, j: (0, j)))
    if res is not None:
        ins.append(res)
        in_specs.append(pl.BlockSpec((tm, tn), lambda i, j: (i, j)))
        blocks += _nbytes((tm, tn), res.dtype)
    kern = functools.partial(_mm_kernel, has_scale=scale is not None,
                             has_bias=bias is not None, has_res=res is not None)
    return pl.pallas_call(
        kern,
        grid=(m // tm, n // tn),
        in_specs=in_specs,
        out_specs=pl.BlockSpec((tm, tn), lambda i, j: (i, j)),
        out_shape=jax.ShapeDtypeStruct((m, n), out_dtype),
        compiler_params=_params(("parallel", "arbitrary"), blocks, 0, 4 * _nbytes((tm, tn), F32)),
        name=name,
    )(*ins)


def _kpe_kernel(h_ref, w_ref, cos_ref, sin_ref, out_ref):
    y = jnp.dot(h_ref[...], w_ref[...], preferred_element_type=F32)
    out_ref[...] = (y[:, :LANES] * cos_ref[...] + y[:, LANES:] * sin_ref[...]).astype(out_ref.dtype)


def _kpe(h, w, cos, sin, *, tm, name):
    m, d = h.shape
    nt = cos.shape[0] // tm
    blocks = _nbytes((tm, d), BF16) + _nbytes((d, 2 * LANES), BF16) + 3 * _nbytes((tm, LANES), F32)
    return pl.pallas_call(
        _kpe_kernel,
        grid=(m // tm,),
        in_specs=[pl.BlockSpec((tm, d), lambda i: (i, 0)),
                  pl.BlockSpec((d, 2 * LANES), lambda i: (0, 0)),
                  pl.BlockSpec((tm, LANES), lambda i: (i % nt, 0)),
                  pl.BlockSpec((tm, LANES), lambda i: (i % nt, 0))],
        out_specs=pl.BlockSpec((tm, LANES), lambda i: (i, 0)),
        out_shape=jax.ShapeDtypeStruct((m, LANES), BF16),
        compiler_params=_params(("parallel",), blocks, 0, _nbytes((tm, 2 * LANES), F32)),
        name=name,
    )(h, w, cos, sin)


def _mlaq_kernel(c_ref, g_ref, w_ref, cos_ref, sin_ref, out_ref, cn_scr, *, heads, scale):
    @pl.when(pl.program_id(1) == 0)
    def _():
        cn_scr[...] = _rms(c_ref[...].astype(F32), g_ref[...]).astype(BF16)

    y = jnp.dot(cn_scr[...], w_ref[...], preferred_element_type=F32)
    cos = cos_ref[...]
    sin = sin_ref[...]
    rot0 = heads * MLA_SLOT
    for h in range(heads):
        s0 = h * MLA_SLOT
        out_ref[:, s0:s0 + NOPE_D] = (y[:, s0:s0 + NOPE_D] * scale).astype(out_ref.dtype)
        pe = y[:, s0 + NOPE_D:s0 + MLA_SLOT] * cos + y[:, rot0 + h * LANES:rot0 + (h + 1) * LANES] * sin
        out_ref[:, s0 + NOPE_D:s0 + MLA_SLOT] = (pe * scale).astype(out_ref.dtype)


def _mlaq(p, col_block, gain, w, cos, sin, *, tm, heads, scale, name):
    m = p.shape[0]
    k = gain.shape[1]
    wn = heads * (MLA_SLOT + LANES)
    groups = w.shape[1] // wn
    nt = cos.shape[0] // tm
    kern = functools.partial(_mlaq_kernel, heads=heads, scale=scale)
    blocks = (_nbytes((tm, k), BF16) + _nbytes((k, wn), BF16) + 2 * _nbytes((tm, LANES), F32)
              + _nbytes((tm, heads * MLA_SLOT), BF16))
    return pl.pallas_call(
        kern,
        grid=(m // tm, groups),
        in_specs=[pl.BlockSpec((tm, k), lambda i, j: (i, col_block)),
                  pl.BlockSpec((1, k), lambda i, j: (0, 0)),
                  pl.BlockSpec((k, wn), lambda i, j: (0, j)),
                  pl.BlockSpec((tm, LANES), lambda i, j: (i % nt, 0)),
                  pl.BlockSpec((tm, LANES), lambda i, j: (i % nt, 0))],
        out_specs=pl.BlockSpec((tm, heads * MLA_SLOT), lambda i, j: (i, j)),
        out_shape=jax.ShapeDtypeStruct((m, groups * heads * MLA_SLOT), BF16),
        scratch_shapes=[pltpu.VMEM((tm, k), BF16)],
        compiler_params=_params(("parallel", "arbitrary"), blocks, _nbytes((tm, k), BF16),
                                2 * _nbytes((tm, wn), F32)),
        name=name,
    )(p, gain, w, cos, sin)


def _mlakv_kernel(c_ref, g_ref, w_ref, out_ref, cn_scr):
    @pl.when(pl.program_id(1) == 0)
    def _():
        cn_scr[...] = _rms(c_ref[...].astype(F32), g_ref[...]).astype(BF16)

    out_ref[...] = jnp.dot(cn_scr[...], w_ref[...], preferred_element_type=F32).astype(out_ref.dtype)


def _mlakv(p, col_block, gain, w, *, tm, tn, name):
    m = p.shape[0]
    k, n = w.shape
    blocks = _nbytes((tm, k), BF16) + _nbytes((k, tn), BF16) + _nbytes((tm, tn), BF16)
    return pl.pallas_call(
        _mlakv_kernel,
        grid=(m // tm, n // tn),
        in_specs=[pl.BlockSpec((tm, k), lambda i, j: (i, col_block)),
                  pl.BlockSpec((1, k), lambda i, j: (0, 0)),
                  pl.BlockSpec((k, tn), lambda i, j: (0, j))],
        out_specs=pl.BlockSpec((tm, tn), lambda i, j: (i, j)),
        out_shape=jax.ShapeDtypeStruct((m, n), BF16),
        scratch_shapes=[pltpu.VMEM((tm, k), BF16)],
        compiler_params=_params(("parallel", "arbitrary"), blocks, _nbytes((tm, k), BF16),
                                2 * _nbytes((tm, tn), F32)),
        name=name,
    )(p, gain, w)


def _dot_nt(a, b):
    return lax.dot_general(a, b, (((1,), (1,)), ((), ())), preferred_element_type=F32)


def _transpose_bf16(x):
    return x.astype(F32).T.astype(BF16)


def _softmax_first_t(s_t, v_t):
    m = jnp.max(s_t, axis=0, keepdims=True)
    p = jnp.exp2(s_t - m)
    return m, jnp.sum(p, axis=0, keepdims=True), jnp.dot(v_t, p.astype(BF16), preferred_element_type=F32)


def _softmax_next_t(s_t, shift, v_t, m, l, acc):
    m_new = jnp.maximum(m, jnp.max(s_t, axis=0, keepdims=True) + shift)
    alpha = jnp.exp2(m - m_new)
    p = jnp.exp2(s_t - (m_new - shift))
    l = alpha * l + jnp.sum(p, axis=0, keepdims=True)
    acc = alpha * acc + jnp.dot(v_t, p.astype(BF16), preferred_element_type=F32)
    return m_new, l, acc


def _meta_mask_t(s_t):
    row = lax.broadcasted_iota(jnp.int32, s_t.shape, 0)
    return jnp.where(row < N_META, s_t, NEG_BIG)


def _diff_attn_kernel(q_ref, k_ref, v_ref, km_ref, vm_ref, slope_ref, lq1_ref, lk1_ref, lq2_ref,
                      lk2_ref, subln_ref, out_ref, vt_scr, bias_scr, *, lam_init):
    tq = q_ref.shape[0]
    tk = tq
    nk = k_ref.shape[0] // tk
    hd = DIFF_HEAD_DIM
    qi = pl.program_id(2)
    slope2 = slope_ref[...] * LOG2E

    @pl.when(qi == 0)
    def _():
        vt_scr[:, :META_PAD] = _transpose_bf16(vm_ref[...])
        for c in range(nk):
            vt_scr[:, META_PAD + c * tk:META_PAD + (c + 1) * tk] = _transpose_bf16(
                v_ref[c * tk:(c + 1) * tk, :])
        d = (lax.broadcasted_iota(jnp.int32, (tk, tq), 1)
             - lax.broadcasted_iota(jnp.int32, (tk, tq), 0)).astype(F32) * slope2
        bias_scr[0] = -d
        bias_scr[1] = d
        bias_scr[2] = -jnp.abs(d)

    q1 = q_ref[:, :hd]
    q2 = q_ref[:, hd:]

    def qk(j):
        kb = k_ref[j * tk:(j + 1) * tk, :]
        return _dot_nt(kb[:, :hd], q1), _dot_nt(kb[:, hd:], q2)

    vtm = vt_scr[:, :META_PAD]
    st1 = _softmax_first_t(_meta_mask_t(_dot_nt(km_ref[:, :hd], q1)), vtm)
    st2 = _softmax_first_t(_meta_mask_t(_dot_nt(km_ref[:, hd:], q2)), vtm)

    s1, s2 = qk(0)
    for j in range(nk):
        nxt = qk(j + 1) if j + 1 < nk else None
        bias = bias_scr[jnp.where(j < qi, 0, jnp.where(j == qi, 2, 1))]
        shift = -jnp.abs(qi * tq - j * tk).astype(F32) * slope2
        vt = vt_scr[:, META_PAD + j * tk:META_PAD + (j + 1) * tk]
        st1 = _softmax_next_t(s1 + bias, shift, vt, *st1)
        st2 = _softmax_next_t(s2 + bias, shift, vt, *st2)
        if nxt is not None:
            s1, s2 = nxt

    lam = (jnp.exp(jnp.sum(lq1_ref[...] * lk1_ref[...], axis=-1, keepdims=True))
           - jnp.exp(jnp.sum(lq2_ref[...] * lk2_ref[...], axis=-1, keepdims=True)) + lam_init)
    o_t = st1[2] / st1[1] - lam * (st2[2] / st2[1])
    out_ref[...] = (_rms(o_t.T, subln_ref[...]) * (1.0 - lam_init)).astype(out_ref.dtype)


def _diff_attn(p, pm, slopes, lq1, lk1, lq2, lk2, subln, *, batch, seq, tq, lam_init, name):
    m = p.shape[0]
    hw = 2 * DIFF_HEAD_DIM
    nq = seq // tq
    koff = DIFF_HEADS
    voff = 2 * DIFF_HEADS
    kern = functools.partial(_diff_attn_kernel, lam_init=lam_init)
    vec = pl.BlockSpec((1, DIFF_HEAD_DIM), lambda b, h, i: (0, 0))
    blocks = (2 * _nbytes((tq, hw), BF16) + 2 * _nbytes((seq, hw), BF16)
              + 2 * _nbytes((META_PAD, hw), BF16))
    scratch = _nbytes((hw, seq + META_PAD), BF16) + 3 * _nbytes((tq, tq), F32)
    return pl.pallas_call(
        kern,
        grid=(batch, DIFF_HEADS, nq),
        in_specs=[pl.BlockSpec((tq, hw), lambda b, h, i: (b * nq + i, h)),
                  pl.BlockSpec((seq, hw), lambda b, h, i: (b, koff + h)),
                  pl.BlockSpec((seq, hw), lambda b, h, i: (b, voff + h)),
                  pl.BlockSpec((META_PAD, hw), lambda b, h, i: (0, koff + h)),
                  pl.BlockSpec((META_PAD, hw), lambda b, h, i: (0, voff + h)),
                  pl.BlockSpec((None, 1, 1), lambda b, h, i: (h, 0, 0)),
                  vec, vec, vec, vec,
                  pl.BlockSpec((1, hw), lambda b, h, i: (0, 0))],
        out_specs=pl.BlockSpec((tq, hw), lambda b, h, i: (b * nq + i, h)),
        out_shape=jax.ShapeDtypeStruct((m, DIFF_HEADS * hw), BF16),
        scratch_shapes=[pltpu.VMEM((hw, seq + META_PAD), BF16),
                        pltpu.VMEM((3, tq, tq), F32)],
        compiler_params=_params(("parallel", "parallel", "arbitrary"), blocks, scratch,
                                12 * _nbytes((tq, tq), F32)),
        name=name,
    )(p, p, p, pm, pm, slopes, lq1, lk1, lq2, lk2, subln)


def _mla_attn_kernel(q_ref, kv_ref, kpe_ref, kvm_ref, kpem_ref, out_ref, k_scr, vt_scr, *, tk):
    nk = kv_ref.shape[0] // tk

    @pl.when(pl.program_id(2) == 0)
    def _():
        k_scr[:META_PAD, :NOPE_D] = kvm_ref[:, :NOPE_D]
        k_scr[:META_PAD, NOPE_D:] = kpem_ref[...]
        k_scr[META_PAD:, :NOPE_D] = kv_ref[:, :NOPE_D]
        k_scr[META_PAD:, NOPE_D:] = kpe_ref[...]
        vt_scr[:, :META_PAD] = _transpose_bf16(kvm_ref[:, NOPE_D:])
        for c in range(nk):
            vt_scr[:, META_PAD + c * tk:META_PAD + (c + 1) * tk] = _transpose_bf16(
                kv_ref[c * tk:(c + 1) * tk, NOPE_D:])

    q = q_ref[...]

    def qk(j):
        return _dot_nt(k_scr[META_PAD + j * tk:META_PAD + (j + 1) * tk, :], q)

    st = _softmax_first_t(_meta_mask_t(_dot_nt(k_scr[:META_PAD, :], q)), vt_scr[:, :META_PAD])

    s = qk(0)
    for j in range(nk):
        nxt = qk(j + 1) if j + 1 < nk else None
        st = _softmax_next_t(s, 0.0, vt_scr[:, META_PAD + j * tk:META_PAD + (j + 1) * tk], *st)
        s = nxt

    out_ref[...] = (st[2] / st[1]).T.astype(out_ref.dtype)


def _mla_attn(q, kv, kpe, kvm, kpem, *, batch, seq, tq, tk, name):
    mrows = q.shape[0]
    nq = seq // tq
    kern = functools.partial(_mla_attn_kernel, tk=tk)
    blocks = (_nbytes((tq, MLA_SLOT), BF16) + _nbytes((seq, MLA_SLOT), BF16)
              + _nbytes((seq, LANES), BF16) + _nbytes((META_PAD, MLA_SLOT + LANES), BF16)
              + _nbytes((tq, MLA_V_D), BF16))
    scratch = _nbytes((seq + META_PAD, MLA_SLOT), BF16) + _nbytes((MLA_V_D, seq + META_PAD), BF16)
    return pl.pallas_call(
        kern,
        grid=(batch, MLA_HEADS, nq),
        in_specs=[pl.BlockSpec((tq, MLA_SLOT), lambda b, h, i: (b * nq + i, h)),
                  pl.BlockSpec((seq, MLA_SLOT), lambda b, h, i: (b, h)),
                  pl.BlockSpec((seq, LANES), lambda b, h, i: (b, 0)),
                  pl.BlockSpec((META_PAD, MLA_SLOT), lambda b, h, i: (0, h)),
                  pl.BlockSpec((META_PAD, LANES), lambda b, h, i: (0, 0))],
        out_specs=pl.BlockSpec((tq, MLA_V_D), lambda b, h, i: (b * nq + i, h)),
        out_shape=jax.ShapeDtypeStruct((mrows, MLA_HEADS * MLA_V_D), BF16),
        scratch_shapes=[pltpu.VMEM((seq + META_PAD, MLA_SLOT), BF16),
                        pltpu.VMEM((MLA_V_D, seq + META_PAD), BF16)],
        compiler_params=_params(("parallel", "parallel", "arbitrary"), blocks, scratch,
                                6 * _nbytes((tk, tq), F32)),
        name=name,
    )(q, kv, kpe, kvm, kpem)


def _merge_kernel(od_ref, om_ref, wa_ref, wb_ref, gd_ref, gm_ref, out_ref):
    ya = jnp.dot(od_ref[...], wa_ref[...], preferred_element_type=F32)
    yb = jnp.dot(om_ref[...], wb_ref[...], preferred_element_type=F32)
    out_ref[...] = (gd_ref[...].astype(F32) * ya + gm_ref[...].astype(F32) * yb).astype(out_ref.dtype)


def _merge(od, om, wa, wb, gates, *, tm, tn, name):
    m, k = od.shape
    n = wa.shape[1]
    nj = n // tn
    blocks = 2 * _nbytes((tm, k), BF16) + 2 * _nbytes((k, tn), BF16) + 3 * _nbytes((tm, tn), BF16)
    return pl.pallas_call(
        _merge_kernel,
        grid=(m // tm, nj),
        in_specs=[pl.BlockSpec((tm, k), lambda i, j: (i, 0)),
                  pl.BlockSpec((tm, k), lambda i, j: (i, 0)),
                  pl.BlockSpec((k, tn), lambda i, j: (0, j)),
                  pl.BlockSpec((k, tn), lambda i, j: (0, j)),
                  pl.BlockSpec((tm, tn), lambda i, j: (i, j)),
                  pl.BlockSpec((tm, tn), lambda i, j: (i, nj + j))],
        out_specs=pl.BlockSpec((tm, tn), lambda i, j: (i, j)),
        out_shape=jax.ShapeDtypeStruct((m, n), BF16),
        compiler_params=_params(("parallel", "arbitrary"), blocks, 0, 4 * _nbytes((tm, tn), F32)),
        name=name,
    )(od, om, wa, wb, gates, gates)


def _rope_tables(t):
    inv_freq = 1.0 / (ROPE_THETA ** (jnp.arange(0, ROPE_D, 2, dtype=F32) / ROPE_D))
    ang = jnp.arange(t).astype(F32)[:, None] * inv_freq[None, :]
    pad = jnp.zeros((t, LANES - ROPE_D), F32)
    cos = jnp.concatenate([jnp.cos(ang), jnp.cos(ang), pad], axis=-1)
    sin = jnp.concatenate([jnp.sin(ang), jnp.sin(ang), pad], axis=-1)
    return cos, sin


def _rotate_cols(w):
    half = ROPE_D // 2
    return jnp.concatenate([-w[..., half:], w[..., :half]], axis=-1)


def kernel(x, meta_tokens, ffn1_norm, ffn1_w_gate, ffn1_w_up, ffn1_w_down, mix_norm, w_in, diff_lambda_q1, diff_lambda_k1, diff_lambda_q2, diff_lambda_k2, diff_subln, mla_q_norm, mla_w_uq, mla_kv_norm, mla_w_ukv, w_gate, b_gate, w_branch_diff, w_branch_mla, w_out, ffn2_norm, ffn2_w_gate, ffn2_w_up, ffn2_w_down, final_norm):
    batch, seq, d = x.shape
    depth = ffn1_norm.shape[0]
    m = batch * seq
    dq_w = DIFF_HEADS * 2 * DIFF_HEAD_DIM
    dv_w = DIFF_HEADS * DIFF_V_DIM
    main_w = 2 * dq_w + dv_w + Q_LORA + KV_LORA
    cq_block = (2 * dq_w + dv_w) // Q_LORA
    ckv_block = (2 * dq_w + dv_w + Q_LORA) // KV_LORA
    q_heads = 4

    cos, sin = _rope_tables(N_META + seq)
    cos_m, sin_m, cos_r, sin_r = cos[:N_META], sin[:N_META], cos[N_META:], sin[N_META:]
    slopes = jnp.asarray([2.0 ** (-8.0 * (h + 1) / DIFF_HEADS) for h in range(DIFF_HEADS)],
                         F32).reshape(DIFF_HEADS, 1, 1)
    col_scale = jnp.concatenate([jnp.full((1, dq_w), DIFF_HEAD_DIM ** -0.5 * LOG2E, F32),
                                 jnp.ones((1, main_w - dq_w), F32)], axis=-1)
    row = lambda v: v.reshape(1, -1).astype(F32)

    xs = x.reshape(m, d)
    xm = meta_tokens.astype(x.dtype)

    for l in range(depth):
        lam_init = 0.8 - 0.6 * math.exp(-0.3 * l)
        wg1, wu1, wd1 = (w[l].astype(BF16) for w in (ffn1_w_gate, ffn1_w_up, ffn1_w_down))
        wg2, wu2, wd2 = (w[l].astype(BF16) for w in (ffn2_w_gate, ffn2_w_up, ffn2_w_down))
        w_main = w_in[l][:, :main_w].astype(BF16)
        w_kr = w_in[l][:, main_w:]
        zpad = jnp.zeros((d, LANES - ROPE_D), F32)
        w_kpe = jnp.concatenate([w_kr, zpad, _rotate_cols(w_kr), zpad], axis=-1).astype(BF16)

        wuq = mla_w_uq[l].reshape(Q_LORA, MLA_HEADS, NOPE_D + ROPE_D)
        zq = jnp.zeros((Q_LORA, MLA_HEADS, MLA_SLOT - NOPE_D - ROPE_D), F32)
        w_a = jnp.concatenate([wuq, zq], axis=-1).reshape(Q_LORA, MLA_HEADS // q_heads, q_heads * MLA_SLOT)
        w_b = jnp.concatenate([_rotate_cols(wuq[..., NOPE_D:]),
                               jnp.zeros((Q_LORA, MLA_HEADS, LANES - ROPE_D), F32)], axis=-1)
        w_b = w_b.reshape(Q_LORA, MLA_HEADS // q_heads, q_heads * LANES)
        w_q = jnp.concatenate([w_a, w_b], axis=-1).reshape(Q_LORA, -1).astype(BF16)
        w_ukv = mla_w_ukv[l].astype(BF16)
        w_g = w_gate[l].astype(BF16)
        w_pa = w_branch_diff[l].astype(BF16)
        w_pb = w_branch_mla[l].astype(BF16)
        w_o = w_out[l].astype(BF16)

        xs = _ffn(xs, row(ffn1_norm[l]), wg1, wu1, wd1, row(final_norm),
                  tm=512, tf=256, final_norm=False, name="ffn1")
        xm = _ffn(xm, row(ffn1_norm[l]), wg1, wu1, wd1, row(final_norm),
                  tm=N_META, tf=256, final_norm=False, name="ffn1_meta")

        hs = _norm(xs, row(mix_norm[l]), tm=512, name="mix_norm")
        hm = _norm(xm, row(mix_norm[l]), tm=N_META, name="mix_norm_meta")

        ps = _mm(hs, w_main, tm=1024, tn=768, out_dtype=BF16, scale=col_scale, name="proj")
        pm = _mm(hm, w_main, tm=N_META, tn=768, out_dtype=BF16, scale=col_scale, name="proj_meta")
        gates = _mm(hs, w_g, tm=1024, tn=512, out_dtype=BF16, bias=row(b_gate[l]), name="gates")
        kpe_s = _kpe(hs, w_kpe, cos_r, sin_r, tm=1024, name="kpe")
        kpe_m = _kpe(hm, w_kpe, cos_m, sin_m, tm=N_META, name="kpe_meta")

        q_mla = _mlaq(ps, cq_block, row(mla_q_norm[l]), w_q, cos_r, sin_r, tm=512, heads=q_heads,
                      scale=(NOPE_D + ROPE_D) ** -0.5 * LOG2E, name="mla_q")
        kv_s = _mlakv(ps, ckv_block, row(mla_kv_norm[l]), w_ukv, tm=1024, tn=1024, name="mla_kv")
        kv_m = _mlakv(pm, ckv_block, row(mla_kv_norm[l]), w_ukv, tm=N_META, tn=1024, name="mla_kv_meta")

        pad_rows = lambda a: jnp.pad(a, ((0, META_PAD - N_META), (0, 0)))
        o_diff = _diff_attn(ps, pad_rows(pm), slopes, row(diff_lambda_q1[l]), row(diff_lambda_k1[l]),
                            row(diff_lambda_q2[l]), row(diff_lambda_k2[l]), row(diff_subln[l]),
                            batch=batch, seq=seq, tq=256, lam_init=lam_init, name="diff_attn")
        o_mla = _mla_attn(q_mla, kv_s, kpe_s, pad_rows(kv_m), pad_rows(kpe_m),
                          batch=batch, seq=seq, tq=1024, tk=512, name="mla_attn")

        merged = _merge(o_diff, o_mla, w_pa, w_pb, gates, tm=1024, tn=512, name="merge")
        xs = _mm(merged, w_o, tm=1024, tn=512, out_dtype=F32, res=xs, name="out_proj")

        last = l == depth - 1
        xs = _ffn(xs, row(ffn2_norm[l]), wg2, wu2, wd2, row(final_norm),
                  tm=512, tf=256, final_norm=last, name="ffn2")
        if not last:
            raise NotImplementedError("DEPTH > 1 needs the meta-row query path")

    return xs.reshape(batch, seq, d)
```

```python
import functools
import math

import numpy as np
import jax
import jax.numpy as jnp
from jax import lax
from jax.experimental import pallas as pl
from jax.experimental.pallas import tpu as pltpu

N_META = 16
EPS = 1e-6
DIFF_HEADS = 8
DIFF_HEAD_DIM = 128
DIFF_V_DIM = 256
MLA_HEADS = 16
Q_LORA = 1024
KV_LORA = 512
NOPE_D = 128
ROPE_D = 64
MLA_V_D = 128
ROPE_THETA = 10000.0

LANES = 128
MLA_SLOT = 256
META_PAD = 128
VMEM_CAP = 60000 * 1024
VMEM_PHYSICAL = 64 * 2**20
VMEM_RESERVE = 2 * 2**20
VMEM_FLOOR = 16 * 2**20
NEG_BIG = -1e30
LOG2E = math.log2(math.e)

F32 = jnp.float32
BF16 = jnp.bfloat16


def _nbytes(shape, dtype):
    return int(np.prod(shape)) * jnp.dtype(dtype).itemsize


def _params(semantics, block_bytes, scratch_bytes, temp_bytes):
    need = 2 * block_bytes + scratch_bytes + temp_bytes
    return pltpu.CompilerParams(dimension_semantics=semantics,
                                vmem_limit_bytes=int(min(VMEM_CAP, max(need, VMEM_FLOOR))))


def _rms(x, gain):
    return x * lax.rsqrt(jnp.mean(x * x, axis=-1, keepdims=True) + EPS) * gain


def _ffn_kernel(x_ref, g_ref, wga_ref, wua_ref, wda_ref, wgb_ref, wub_ref, wdb_ref, gf_ref, out_ref,
                h_scr, *, final_norm, nchunks):
    f = pl.program_id(1)
    last = pl.num_programs(1) - 1

    @pl.when(f == 0)
    def _():
        x = x_ref[...]
        h_scr[...] = _rms(x, g_ref[...]).astype(BF16)
        out_ref[...] = x

    def act(wg_ref, wu_ref):
        h = h_scr[...]
        g = jnp.dot(h, wg_ref[...], preferred_element_type=F32)
        u = jnp.dot(h, wu_ref[...], preferred_element_type=F32)
        return (0.5 * (g * jax.nn.sigmoid(g)) * u).astype(BF16)

    def both():
        a = act(wga_ref, wua_ref)
        b = act(wgb_ref, wub_ref)
        out_ref[...] += (jnp.dot(a, wda_ref[...], preferred_element_type=F32)
                         + jnp.dot(b, wdb_ref[...], preferred_element_type=F32))

    if nchunks % 2:
        pl.when(f < last)(both)

        @pl.when(f == last)
        def _():
            out_ref[...] += jnp.dot(act(wga_ref, wua_ref), wda_ref[...], preferred_element_type=F32)
    else:
        both()

    if final_norm:
        @pl.when(f == last)
        def _():
            out_ref[...] = _rms(out_ref[...], gf_ref[...])


def _ffn(x, gain, wg, wu, wd, final_gain, *, tm, tf, final_norm, name):
    m, d = x.shape
    nchunks = wg.shape[1] // tf
    kern = functools.partial(_ffn_kernel, final_norm=final_norm, nchunks=nchunks)
    col_a = lambda i, f: (0, 2 * f)
    col_b = lambda i, f: (0, jnp.minimum(2 * f + 1, nchunks - 1))
    row_a = lambda i, f: (2 * f, 0)
    row_b = lambda i, f: (jnp.minimum(2 * f + 1, nchunks - 1), 0)
    weights = 2 * (2 * _nbytes((d, tf), BF16) + _nbytes((tf, d), BF16))
    need = (_nbytes((tm, d), F32) + 2 * _nbytes((tm, d), F32) + 2 * weights
            + _nbytes((tm, d), BF16) + 6 * _nbytes((tm, tf), F32) + _nbytes((tm, d), F32))
    return pl.pallas_call(
        kern,
        grid=(m // tm, (nchunks + 1) // 2),
        in_specs=[
            pl.BlockSpec((tm, d), lambda i, f: (i, 0), pipeline_mode=pl.Buffered(1)),
            pl.BlockSpec((1, d), lambda i, f: (0, 0)),
            pl.BlockSpec((d, tf), col_a), pl.BlockSpec((d, tf), col_a), pl.BlockSpec((tf, d), row_a),
            pl.BlockSpec((d, tf), col_b), pl.BlockSpec((d, tf), col_b), pl.BlockSpec((tf, d), row_b),
            pl.BlockSpec((1, d), lambda i, f: (0, 0)),
        ],
        out_specs=pl.BlockSpec((tm, d), lambda i, f: (i, 0)),
        out_shape=jax.ShapeDtypeStruct((m, d), F32),
        scratch_shapes=[pltpu.VMEM((tm, d), BF16)],
        compiler_params=pltpu.CompilerParams(
            dimension_semantics=("parallel", "arbitrary"),
            vmem_limit_bytes=int(min(VMEM_PHYSICAL - VMEM_RESERVE, max(need, VMEM_FLOOR)))),
        name=name,
    )(x, gain, wg, wu, wd, wg, wu, wd, final_gain)


def _norm_kernel(x_ref, g_ref, out_ref):
    out_ref[...] = _rms(x_ref[...], g_ref[...]).astype(out_ref.dtype)


def _norm(x, gain, *, tm, name):
    m, d = x.shape
    blocks = _nbytes((tm, d), F32) + _nbytes((tm, d), BF16)
    return pl.pallas_call(
        _norm_kernel,
        grid=(m // tm,),
        in_specs=[pl.BlockSpec((tm, d), lambda i: (i, 0)),
                  pl.BlockSpec((1, d), lambda i: (0, 0))],
        out_specs=pl.BlockSpec((tm, d), lambda i: (i, 0)),
        out_shape=jax.ShapeDtypeStruct((m, d), BF16),
        compiler_params=_params(("parallel",), blocks, 0, 2 * _nbytes((tm, d), F32)),
        name=name,
    )(x, gain)


def _mm_kernel(*refs, has_scale, has_bias, has_res):
    a_ref, w_ref = refs[0], refs[1]
    out_ref = refs[-1]
    extra = list(refs[2:-1])
    y = jnp.dot(a_ref[...], w_ref[...], preferred_element_type=F32)
    if has_scale:
        y = y * extra.pop(0)[...]
    if has_bias:
        y = jax.nn.sigmoid(y + extra.pop(0)[...])
    if has_res:
        y = y + extra.pop(0)[...]
    out_ref[...] = y.astype(out_ref.dtype)


def _mm(a, w, *, tm, tn, out_dtype, name, scale=None, bias=None, res=None):
    m, k = a.shape
    n = w.shape[1]
    ins = [a, w]
    in_specs = [pl.BlockSpec((tm, k), lambda i, j: (i, 0)),
                pl.BlockSpec((k, tn), lambda i, j: (0, j))]
    blocks = _nbytes((tm, k), a.dtype) + _nbytes((k, tn), w.dtype) + _nbytes((tm, tn), out_dtype)
    for vec in (scale, bias):
        if vec is not None:
            ins.append(vec)
            in_specs.append(pl.BlockSpec((1, tn), lambda i, j: (0, j)))
    if res is not None:
        ins.append(res)
        in_specs.append(pl.BlockSpec((tm, tn), lambda i, j: (i, j)))
        blocks += _nbytes((tm, tn), res.dtype)
    kern = functools.partial(_mm_kernel, has_scale=scale is not None,
                             has_bias=bias is not None, has_res=res is not None)
    return pl.pallas_call(
        kern,
        grid=(m // tm, n // tn),
        in_specs=in_specs,
        out_specs=pl.BlockSpec((tm, tn), lambda i, j: (i, j)),
        out_shape=jax.ShapeDtypeStruct((m, n), out_dtype),
        compiler_params=_params(("parallel", "arbitrary"), blocks, 0, 4 * _nbytes((tm, tn), F32)),
        name=name,
    )(*ins)


def _kpe_kernel(h_ref, w_ref, cos_ref, sin_ref, out_ref):
    y = jnp.dot(h_ref[...], w_ref[...], preferred_element_type=F32)
    out_ref[...] = (y[:, :LANES] * cos_ref[...] + y[:, LANES:] * sin_ref[...]).astype(out_ref.dtype)


def _kpe(h, w, cos, sin, *, tm, name):
    m, d = h.shape
    nt = cos.shape[0] // tm
    blocks = _nbytes((tm, d), BF16) + _nbytes((d, 2 * LANES), BF16) + 3 * _nbytes((tm, LANES), F32)
    return pl.pallas_call(
        _kpe_kernel,
        grid=(m // tm,),
        in_specs=[pl.BlockSpec((tm, d), lambda i: (i, 0)),
                  pl.BlockSpec((d, 2 * LANES), lambda i: (0, 0)),
                  pl.BlockSpec((tm, LANES), lambda i: (i % nt, 0)),
                  pl.BlockSpec((tm, LANES), lambda i: (i % nt, 0))],
        out_specs=pl.BlockSpec((tm, LANES), lambda i: (i, 0)),
        out_shape=jax.ShapeDtypeStruct((m, LANES), BF16),
        compiler_params=_params(("parallel",), blocks, 0, _nbytes((tm, 2 * LANES), F32)),
        name=name,
    )(h, w, cos, sin)


def _mlaq_kernel(c_ref, g_ref, w_ref, cos_ref, sin_ref, out_ref, cn_scr, *, heads, scale):
    @pl.when(pl.program_id(1) == 0)
    def _():
        cn_scr[...] = _rms(c_ref[...].astype(F32), g_ref[...]).astype(BF16)

    y = jnp.dot(cn_scr[...], w_ref[...], preferred_element_type=F32)
    cos = cos_ref[...]
    sin = sin_ref[...]
    rot0 = heads * MLA_SLOT
    for h in range(heads):
        s0 = h * MLA_SLOT
        out_ref[:, s0:s0 + NOPE_D] = (y[:, s0:s0 + NOPE_D] * scale).astype(out_ref.dtype)
        pe = y[:, s0 + NOPE_D:s0 + MLA_SLOT] * cos + y[:, rot0 + h * LANES:rot0 + (h + 1) * LANES] * sin
        out_ref[:, s0 + NOPE_D:s0 + MLA_SLOT] = (pe * scale).astype(out_ref.dtype)


def _mlaq(p, col_block, gain, w, cos, sin, *, tm, heads, scale, name):
    m = p.shape[0]
    k = gain.shape[1]
    wn = heads * (MLA_SLOT + LANES)
    groups = w.shape[1] // wn
    nt = cos.shape[0] // tm
    kern = functools.partial(_mlaq_kernel, heads=heads, scale=scale)
    blocks = (_nbytes((tm, k), BF16) + _nbytes((k, wn), BF16) + 2 * _nbytes((tm, LANES), F32)
              + _nbytes((tm, heads * MLA_SLOT), BF16))
    return pl.pallas_call(
        kern,
        grid=(m // tm, groups),
        in_specs=[pl.BlockSpec((tm, k), lambda i, j: (i, col_block)),
                  pl.BlockSpec((1, k), lambda i, j: (0, 0)),
                  pl.BlockSpec((k, wn), lambda i, j: (0, j)),
                  pl.BlockSpec((tm, LANES), lambda i, j: (i % nt, 0)),
                  pl.BlockSpec((tm, LANES), lambda i, j: (i % nt, 0))],
        out_specs=pl.BlockSpec((tm, heads * MLA_SLOT), lambda i, j: (i, j)),
        out_shape=jax.ShapeDtypeStruct((m, groups * heads * MLA_SLOT), BF16),
        scratch_shapes=[pltpu.VMEM((tm, k), BF16)],
        compiler_params=_params(("parallel", "arbitrary"), blocks, _nbytes((tm, k), BF16),
                                2 * _nbytes((tm, wn), F32)),
        name=name,
    )(p, gain, w, cos, sin)


def _mlakv_kernel(c_ref, g_ref, w_ref, out_ref, cn_scr):
    @pl.when(pl.program_id(1) == 0)
    def _():
        cn_scr[...] = _rms(c_ref[...].astype(F32), g_ref[...]).astype(BF16)

    out_ref[...] = jnp.dot(cn_scr[...], w_ref[...], preferred_element_type=F32).astype(out_ref.dtype)


def _mlakv(p, col_block, gain, w, *, tm, tn, name):
    m = p.shape[0]
    k, n = w.shape
    blocks = _nbytes((tm, k), BF16) + _nbytes((k, tn), BF16) + _nbytes((tm, tn), BF16)
    return pl.pallas_call(
        _mlakv_kernel,
        grid=(m // tm, n // tn),
        in_specs=[pl.BlockSpec((tm, k), lambda i, j: (i, col_block)),
                  pl.BlockSpec((1, k), lambda i, j: (0, 0)),
                  pl.BlockSpec((k, tn), lambda i, j: (0, j))],
        out_specs=pl.BlockSpec((tm, tn), lambda i, j: (i, j)),
        out_shape=jax.ShapeDtypeStruct((m, n), BF16),
        scratch_shapes=[pltpu.VMEM((tm, k), BF16)],
        compiler_params=_params(("parallel", "arbitrary"), blocks, _nbytes((tm, k), BF16),
                                2 * _nbytes((tm, tn), F32)),
        name=name,
    )(p, gain, w)


def _dot_nt(a, b):
    return lax.dot_general(a, b, (((1,), (1,)), ((), ())), preferred_element_type=F32)


def _transpose_bf16(x):
    return x.astype(F32).T.astype(BF16)


def _softmax_first_t(s_t, v_t):
    m = jnp.max(s_t, axis=0, keepdims=True)
    p = jnp.exp2(s_t - m)
    return m, jnp.sum(p, axis=0, keepdims=True), jnp.dot(v_t, p.astype(BF16), preferred_element_type=F32)


def _softmax_next_t(s_t, shift, v_t, m, l, acc):
    m_new = jnp.maximum(m, jnp.max(s_t, axis=0, keepdims=True) + shift)
    alpha = jnp.exp2(m - m_new)
    p = jnp.exp2(s_t - (m_new - shift))
    l = alpha * l + jnp.sum(p, axis=0, keepdims=True)
    acc = alpha * acc + jnp.dot(v_t, p.astype(BF16), preferred_element_type=F32)
    return m_new, l, acc


def _meta_mask_t(s_t):
    row = lax.broadcasted_iota(jnp.int32, s_t.shape, 0)
    return jnp.where(row < N_META, s_t, NEG_BIG)


def _diff_attn_kernel(q_ref, k_ref, v_ref, km_ref, vm_ref, slope_ref, lq1_ref, lk1_ref, lq2_ref,
                      lk2_ref, subln_ref, out_ref, vt_scr, bias_scr, *, lam_init):
    tq = q_ref.shape[0]
    tk = tq
    nk = k_ref.shape[0] // tk
    hd = DIFF_HEAD_DIM
    qi = pl.program_id(2)
    slope2 = slope_ref[...] * LOG2E

    @pl.when(qi == 0)
    def _():
        vt_scr[:, :META_PAD] = _transpose_bf16(vm_ref[...])
        for c in range(nk):
            vt_scr[:, META_PAD + c * tk:META_PAD + (c + 1) * tk] = _transpose_bf16(
                v_ref[c * tk:(c + 1) * tk, :])
        d = (lax.broadcasted_iota(jnp.int32, (tk, tq), 1)
             - lax.broadcasted_iota(jnp.int32, (tk, tq), 0)).astype(F32) * slope2
        bias_scr[0] = -d
        bias_scr[1] = d
        bias_scr[2] = -jnp.abs(d)

    q1 = q_ref[:, :hd]
    q2 = q_ref[:, hd:]

    def qk(j):
        kb = k_ref[j * tk:(j + 1) * tk, :]
        return _dot_nt(kb[:, :hd], q1), _dot_nt(kb[:, hd:], q2)

    vtm = vt_scr[:, :META_PAD]
    st1 = _softmax_first_t(_meta_mask_t(_dot_nt(km_ref[:, :hd], q1)), vtm)
    st2 = _softmax_first_t(_meta_mask_t(_dot_nt(km_ref[:, hd:], q2)), vtm)

    s1, s2 = qk(0)
    for j in range(nk):
        nxt = qk(j + 1) if j + 1 < nk else None
        bias = bias_scr[jnp.where(j < qi, 0, jnp.where(j == qi, 2, 1))]
        shift = -jnp.abs(qi * tq - j * tk).astype(F32) * slope2
        vt = vt_scr[:, META_PAD + j * tk:META_PAD + (j + 1) * tk]
        st1 = _softmax_next_t(s1 + bias, shift, vt, *st1)
        st2 = _softmax_next_t(s2 + bias, shift, vt, *st2)
        if nxt is not None:
            s1, s2 = nxt

    lam = (jnp.exp(jnp.sum(lq1_ref[...] * lk1_ref[...], axis=-1, keepdims=True))
           - jnp.exp(jnp.sum(lq2_ref[...] * lk2_ref[...], axis=-1, keepdims=True)) + lam_init)
    o_t = st1[2] / st1[1] - lam * (st2[2] / st2[1])
    out_ref[...] = (_rms(o_t.T, subln_ref[...]) * (1.0 - lam_init)).astype(out_ref.dtype)


def _diff_attn(p, pm, slopes, lq1, lk1, lq2, lk2, subln, *, batch, seq, tq, lam_init, name):
    m = p.shape[0]
    hw = 2 * DIFF_HEAD_DIM
    nq = seq // tq
    koff = DIFF_HEADS
    voff = 2 * DIFF_HEADS
    kern = functools.partial(_diff_attn_kernel, lam_init=lam_init)
    vec = pl.BlockSpec((1, DIFF_HEAD_DIM), lambda b, h, i: (0, 0))
    blocks = (2 * _nbytes((tq, hw), BF16) + 2 * _nbytes((seq, hw), BF16)
              + 2 * _nbytes((META_PAD, hw), BF16))
    scratch = _nbytes((hw, seq + META_PAD), BF16) + 3 * _nbytes((tq, tq), F32)
    return pl.pallas_call(
        kern,
        grid=(batch, DIFF_HEADS, nq),
        in_specs=[pl.BlockSpec((tq, hw), lambda b, h, i: (b * nq + i, h)),
                  pl.BlockSpec((seq, hw), lambda b, h, i: (b, koff + h)),
                  pl.BlockSpec((seq, hw), lambda b, h, i: (b, voff + h)),
                  pl.BlockSpec((META_PAD, hw), lambda b, h, i: (0, koff + h)),
                  pl.BlockSpec((META_PAD, hw), lambda b, h, i: (0, voff + h)),
                  pl.BlockSpec((None, 1, 1), lambda b, h, i: (h, 0, 0)),
                  vec, vec, vec, vec,
                  pl.BlockSpec((1, hw), lambda b, h, i: (0, 0))],
        out_specs=pl.BlockSpec((tq, hw), lambda b, h, i: (b * nq + i, h)),
        out_shape=jax.ShapeDtypeStruct((m, DIFF_HEADS * hw), BF16),
        scratch_shapes=[pltpu.VMEM((hw, seq + META_PAD), BF16),
                        pltpu.VMEM((3, tq, tq), F32)],
        compiler_params=_params(("parallel", "parallel", "arbitrary"), blocks, scratch,
                                12 * _nbytes((tq, tq), F32)),
        name=name,
    )(p, p, p, pm, pm, slopes, lq1, lk1, lq2, lk2, subln)


def _mla_attn_kernel(q_ref, kv_ref, kpe_ref, kvm_ref, kpem_ref, out_ref, k_scr, vt_scr, *, tk):
    nk = kv_ref.shape[0] // tk

    @pl.when(pl.program_id(2) == 0)
    def _():
        k_scr[:META_PAD, :NOPE_D] = kvm_ref[:, :NOPE_D]
        k_scr[:META_PAD, NOPE_D:] = kpem_ref[...]
        k_scr[META_PAD:, :NOPE_D] = kv_ref[:, :NOPE_D]
        k_scr[META_PAD:, NOPE_D:] = kpe_ref[...]
        vt_scr[:, :META_PAD] = _transpose_bf16(kvm_ref[:, NOPE_D:])
        for c in range(nk):
            vt_scr[:, META_PAD + c * tk:META_PAD + (c + 1) * tk] = _transpose_bf16(
                kv_ref[c * tk:(c + 1) * tk, NOPE_D:])

    q = q_ref[...]

    def qk(j):
        return _dot_nt(k_scr[META_PAD + j * tk:META_PAD + (j + 1) * tk, :], q)

    st = _softmax_first_t(_meta_mask_t(_dot_nt(k_scr[:META_PAD, :], q)), vt_scr[:, :META_PAD])

    s = qk(0)
    for j in range(nk):
        nxt = qk(j + 1) if j + 1 < nk else None
        st = _softmax_next_t(s, 0.0, vt_scr[:, META_PAD + j * tk:META_PAD + (j + 1) * tk], *st)
        s = nxt

    out_ref[...] = (st[2] / st[1]).T.astype(out_ref.dtype)


def _mla_attn(q, kv, kpe, kvm, kpem, *, batch, seq, tq, tk, name):
    mrows = q.shape[0]
    nq = seq // tq
    kern = functools.partial(_mla_attn_kernel, tk=tk)
    blocks = (_nbytes((tq, MLA_SLOT), BF16) + _nbytes((seq, MLA_SLOT), BF16)
              + _nbytes((seq, LANES), BF16) + _nbytes((META_PAD, MLA_SLOT + LANES), BF16)
              + _nbytes((tq, MLA_V_D), BF16))
    scratch = _nbytes((seq + META_PAD, MLA_SLOT), BF16) + _nbytes((MLA_V_D, seq + META_PAD), BF16)
    return pl.pallas_call(
        kern,
        grid=(batch, MLA_HEADS, nq),
        in_specs=[pl.BlockSpec((tq, MLA_SLOT), lambda b, h, i: (b * nq + i, h)),
                  pl.BlockSpec((seq, MLA_SLOT), lambda b, h, i: (b, h)),
                  pl.BlockSpec((seq, LANES), lambda b, h, i: (b, 0)),
                  pl.BlockSpec((META_PAD, MLA_SLOT), lambda b, h, i: (0, h)),
                  pl.BlockSpec((META_PAD, LANES), lambda b, h, i: (0, 0))],
        out_specs=pl.BlockSpec((tq, MLA_V_D), lambda b, h, i: (b * nq + i, h)),
        out_shape=jax.ShapeDtypeStruct((mrows, MLA_HEADS * MLA_V_D), BF16),
        scratch_shapes=[pltpu.VMEM((seq + META_PAD, MLA_SLOT), BF16),
                        pltpu.VMEM((MLA_V_D, seq + META_PAD), BF16)],
        compiler_params=_params(("parallel", "parallel", "arbitrary"), blocks, scratch,
                                6 * _nbytes((tk, tq), F32)),
        name=name,
    )(q, kv, kpe, kvm, kpem)


def _merge_kernel(od_ref, om_ref, wa_ref, wb_ref, gd_ref, gm_ref, out_ref):
    ya = jnp.dot(od_ref[...], wa_ref[...], preferred_element_type=F32)
    yb = jnp.dot(om_ref[...], wb_ref[...], preferred_element_type=F32)
    out_ref[...] = (gd_ref[...].astype(F32) * ya + gm_ref[...].astype(F32) * yb).astype(out_ref.dtype)


def _merge(od, om, wa, wb, gates, *, tm, tn, name):
    m, k = od.shape
    n = wa.shape[1]
    nj = n // tn
    blocks = 2 * _nbytes((tm, k), BF16) + 2 * _nbytes((k, tn), BF16) + 3 * _nbytes((tm, tn), BF16)
    return pl.pallas_call(
        _merge_kernel,
        grid=(m // tm, nj),
        in_specs=[pl.BlockSpec((tm, k), lambda i, j: (i, 0)),
                  pl.BlockSpec((tm, k), lambda i, j: (i, 0)),
                  pl.BlockSpec((k, tn), lambda i, j: (0, j)),
                  pl.BlockSpec((k, tn), lambda i, j: (0, j)),
                  pl.BlockSpec((tm, tn), lambda i, j: (i, j)),
                  pl.BlockSpec((tm, tn), lambda i, j: (i, nj + j))],
        out_specs=pl.BlockSpec((tm, tn), lambda i, j: (i, j)),
        out_shape=jax.ShapeDtypeStruct((m, n), BF16),
        compiler_params=_params(("parallel", "arbitrary"), blocks, 0, 4 * _nbytes((tm, tn), F32)),
        name=name,
    )(od, om, wa, wb, gates, gates)


def _rope_tables(t):
    inv_freq = 1.0 / (ROPE_THETA ** (jnp.arange(0, ROPE_D, 2, dtype=F32) / ROPE_D))
    ang = jnp.arange(t).astype(F32)[:, None] * inv_freq[None, :]
    pad = jnp.zeros((t, LANES - ROPE_D), F32)
    cos = jnp.concatenate([jnp.cos(ang), jnp.cos(ang), pad], axis=-1)
    sin = jnp.concatenate([jnp.sin(ang), jnp.sin(ang), pad], axis=-1)
    return cos, sin


def _rotate_cols(w):
    half = ROPE_D // 2
    return jnp.concatenate([-w[..., half:], w[..., :half]], axis=-1)


def kernel(x, meta_tokens, ffn1_norm, ffn1_w_gate, ffn1_w_up, ffn1_w_down, mix_norm, w_in, diff_lambda_q1, diff_lambda_k1, diff_lambda_q2, diff_lambda_k2, diff_subln, mla_q_norm, mla_w_uq, mla_kv_norm, mla_w_ukv, w_gate, b_gate, w_branch_diff, w_branch_mla, w_out, ffn2_norm, ffn2_w_gate, ffn2_w_up, ffn2_w_down, final_norm):
    batch, seq, d = x.shape
    depth = ffn1_norm.shape[0]
    m = batch * seq
    dq_w = DIFF_HEADS * 2 * DIFF_HEAD_DIM
    dv_w = DIFF_HEADS * DIFF_V_DIM
    main_w = 2 * dq_w + dv_w + Q_LORA + KV_LORA
    cq_block = (2 * dq_w + dv_w) // Q_LORA
    ckv_block = (2 * dq_w + dv_w + Q_LORA) // KV_LORA
    q_heads = 4

    cos, sin = _rope_tables(N_META + seq)
    cos_m, sin_m, cos_r, sin_r = cos[:N_META], sin[:N_META], cos[N_META:], sin[N_META:]
    slopes = jnp.asarray([2.0 ** (-8.0 * (h + 1) / DIFF_HEADS) for h in range(DIFF_HEADS)],
                         F32).reshape(DIFF_HEADS, 1, 1)
    col_scale = jnp.concatenate([jnp.full((1, dq_w), DIFF_HEAD_DIM ** -0.5 * LOG2E, F32),
                                 jnp.ones((1, main_w - dq_w), F32)], axis=-1)
    row = lambda v: v.reshape(1, -1).astype(F32)

    xs = x.reshape(m, d)
    xm = meta_tokens.astype(x.dtype)

    for l in range(depth):
        lam_init = 0.8 - 0.6 * math.exp(-0.3 * l)
        wg1, wu1, wd1 = (w[l].astype(BF16) for w in (ffn1_w_gate, ffn1_w_up, ffn1_w_down))
        wg2, wu2, wd2 = (w[l].astype(BF16) for w in (ffn2_w_gate, ffn2_w_up, ffn2_w_down))
        w_main = w_in[l][:, :main_w].astype(BF16)
        w_kr = w_in[l][:, main_w:]
        zpad = jnp.zeros((d, LANES - ROPE_D), F32)
        w_kpe = jnp.concatenate([w_kr, zpad, _rotate_cols(w_kr), zpad], axis=-1).astype(BF16)

        wuq = mla_w_uq[l].reshape(Q_LORA, MLA_HEADS, NOPE_D + ROPE_D)
        zq = jnp.zeros((Q_LORA, MLA_HEADS, MLA_SLOT - NOPE_D - ROPE_D), F32)
        w_a = jnp.concatenate([wuq, zq], axis=-1).reshape(Q_LORA, MLA_HEADS // q_heads, q_heads * MLA_SLOT)
        w_b = jnp.concatenate([_rotate_cols(wuq[..., NOPE_D:]),
                               jnp.zeros((Q_LORA, MLA_HEADS, LANES - ROPE_D), F32)], axis=-1)
        w_b = w_b.reshape(Q_LORA, MLA_HEADS // q_heads, q_heads * LANES)
        w_q = jnp.concatenate([w_a, w_b], axis=-1).reshape(Q_LORA, -1).astype(BF16)
        w_ukv = mla_w_ukv[l].astype(BF16)
        w_g = w_gate[l].astype(BF16)
        w_pa = w_branch_diff[l].astype(BF16)
        w_pb = w_branch_mla[l].astype(BF16)
        w_o = w_out[l].astype(BF16)

        xs = _ffn(xs, row(ffn1_norm[l]), wg1, wu1, wd1, row(final_norm),
                  tm=512, tf=256, final_norm=False, name="ffn1")
        xm = _ffn(xm, row(ffn1_norm[l]), wg1, wu1, wd1, row(final_norm),
                  tm=N_META, tf=256, final_norm=False, name="ffn1_meta")

        hs = _norm(xs, row(mix_norm[l]), tm=512, name="mix_norm")
        hm = _norm(xm, row(mix_norm[l]), tm=N_META, name="mix_norm_meta")

        ps = _mm(hs, w_main, tm=1024, tn=768, out_dtype=BF16, scale=col_scale, name="proj")
        pm = _mm(hm, w_main, tm=N_META, tn=768, out_dtype=BF16, scale=col_scale, name="proj_meta")
        gates = _mm(hs, w_g, tm=1024, tn=512, out_dtype=BF16, bias=row(b_gate[l]), name="gates")
        kpe_s = _kpe(hs, w_kpe, cos_r, sin_r, tm=1024, name="kpe")
        kpe_m = _kpe(hm, w_kpe, cos_m, sin_m, tm=N_META, name="kpe_meta")

        q_mla = _mlaq(ps, cq_block, row(mla_q_norm[l]), w_q, cos_r, sin_r, tm=512, heads=q_heads,
                      scale=(NOPE_D + ROPE_D) ** -0.5 * LOG2E, name="mla_q")
        kv_s = _mlakv(ps, ckv_block, row(mla_kv_norm[l]), w_ukv, tm=1024, tn=1024, name="mla_kv")
        kv_m = _mlakv(pm, ckv_block, row(mla_kv_norm[l]), w_ukv, tm=N_META, tn=1024, name="mla_kv_meta")

        pad_rows = lambda a: jnp.pad(a, ((0, META_PAD - N_META), (0, 0)))
        o_diff = _diff_attn(ps, pad_rows(pm), slopes, row(diff_lambda_q1[l]), row(diff_lambda_k1[l]),
                            row(diff_lambda_q2[l]), row(diff_lambda_k2[l]), row(diff_subln[l]),
                            batch=batch, seq=seq, tq=256, lam_init=lam_init, name="diff_attn")
        o_mla = _mla_attn(q_mla, kv_s, kpe_s, pad_rows(kv_m), pad_rows(kpe_m),
                          batch=batch, seq=seq, tq=1024, tk=512, name="mla_attn")

        merged = _merge(o_diff, o_mla, w_pa, w_pb, gates, tm=1024, tn=512, name="merge")
        xs = _mm(merged, w_o, tm=1024, tn=512, out_dtype=F32, res=xs, name="out_proj")

        last = l == depth - 1
        xs = _ffn(xs, row(ffn2_norm[l]), wg2, wu2, wd2, row(final_norm),
                  tm=512, tf=256, final_norm=last, name="ffn2")
        if not last:
            raise NotImplementedError("DEPTH > 1 needs the meta-row query path")

    return xs.reshape(batch, seq, d)
```

```python
import functools
import math

import numpy as np
import jax
import jax.numpy as jnp
from jax import lax
from jax.experimental import pallas as pl
from jax.experimental.pallas import tpu as pltpu

N_META = 16
EPS = 1e-6
DIFF_HEADS = 8
DIFF_HEAD_DIM = 128
DIFF_V_DIM = 256
MLA_HEADS = 16
Q_LORA = 1024
KV_LORA = 512
NOPE_D = 128
ROPE_D = 64
MLA_V_D = 128
ROPE_THETA = 10000.0

LANES = 128
MLA_SLOT = 256
META_PAD = 128
VMEM_CAP = 60000 * 1024
VMEM_PHYSICAL = 64 * 2**20
VMEM_RESERVE = 2 * 2**20
VMEM_FLOOR = 16 * 2**20
NEG_BIG = -1e30
LOG2E = math.log2(math.e)

F32 = jnp.float32
BF16 = jnp.bfloat16


def _nbytes(shape, dtype):
    return int(np.prod(shape)) * jnp.dtype(dtype).itemsize


def _params(semantics, block_bytes, scratch_bytes, temp_bytes):
    need = 2 * block_bytes + scratch_bytes + temp_bytes
    return pltpu.CompilerParams(dimension_semantics=semantics,
                                vmem_limit_bytes=int(min(VMEM_CAP, max(need, VMEM_FLOOR))))


def _rms(x, gain):
    return x * lax.rsqrt(jnp.mean(x * x, axis=-1, keepdims=True) + EPS) * gain


def _ffn_kernel(x_ref, g_ref, wga_ref, wua_ref, wda_ref, wgb_ref, wub_ref, wdb_ref, gf_ref, out_ref,
                h_scr, *, final_norm, nchunks):
    f = pl.program_id(1)
    last = pl.num_programs(1) - 1

    @pl.when(f == 0)
    def _():
        x = x_ref[...]
        h_scr[...] = _rms(x, g_ref[...]).astype(BF16)
        out_ref[...] = x

    def act(wg_ref, wu_ref):
        h = h_scr[...]
        g = jnp.dot(h, wg_ref[...], preferred_element_type=F32)
        u = jnp.dot(h, wu_ref[...], preferred_element_type=F32)
        return (0.5 * (g * jax.nn.sigmoid(g)) * u).astype(BF16)

    def both():
        a = act(wga_ref, wua_ref)
        b = act(wgb_ref, wub_ref)
        out_ref[...] += (jnp.dot(a, wda_ref[...], preferred_element_type=F32)
                         + jnp.dot(b, wdb_ref[...], preferred_element_type=F32))

    if nchunks % 2:
        pl.when(f < last)(both)

        @pl.when(f == last)
        def _():
            out_ref[...] += jnp.dot(act(wga_ref, wua_ref), wda_ref[...], preferred_element_type=F32)
    else:
        both()

    if final_norm:
        @pl.when(f == last)
        def _():
            out_ref[...] = _rms(out_ref[...], gf_ref[...])


def _ffn(x, gain, wg, wu, wd, final_gain, *, tm, tf, final_norm, name):
    m, d = x.shape
    nchunks = wg.shape[1] // tf
    kern = functools.partial(_ffn_kernel, final_norm=final_norm, nchunks=nchunks)
    col_a = lambda i, f: (0, 2 * f)
    col_b = lambda i, f: (0, jnp.minimum(2 * f + 1, nchunks - 1))
    row_a = lambda i, f: (2 * f, 0)
    row_b = lambda i, f: (jnp.minimum(2 * f + 1, nchunks - 1), 0)
    weights = 2 * (2 * _nbytes((d, tf), BF16) + _nbytes((tf, d), BF16))
    need = (_nbytes((tm, d), F32) + 2 * _nbytes((tm, d), F32) + 2 * weights
            + _nbytes((tm, d), BF16) + 6 * _nbytes((tm, tf), F32) + _nbytes((tm, d), F32))
    return pl.pallas_call(
        kern,
        grid=(m // tm, (nchunks + 1) // 2),
        in_specs=[
            pl.BlockSpec((tm, d), lambda i, f: (i, 0), pipeline_mode=pl.Buffered(1)),
            pl.BlockSpec((1, d), lambda i, f: (0, 0)),
            pl.BlockSpec((d, tf), col_a), pl.BlockSpec((d, tf), col_a), pl.BlockSpec((tf, d), row_a),
            pl.BlockSpec((d, tf), col_b), pl.BlockSpec((d, tf), col_b), pl.BlockSpec((tf, d), row_b),
            pl.BlockSpec((1, d), lambda i, f: (0, 0)),
        ],
        out_specs=pl.BlockSpec((tm, d), lambda i, f: (i, 0)),
        out_shape=jax.ShapeDtypeStruct((m, d), F32),
        scratch_shapes=[pltpu.VMEM((tm, d), BF16)],
        compiler_params=pltpu.CompilerParams(
            dimension_semantics=("parallel", "arbitrary"),
            vmem_limit_bytes=int(min(VMEM_PHYSICAL - VMEM_RESERVE, max(need, VMEM_FLOOR)))),
        name=name,
    )(x, gain, wg, wu, wd, wg, wu, wd, final_gain)


def _norm_kernel(x_ref, g_ref, out_ref):
    out_ref[...] = _rms(x_ref[...], g_ref[...]).astype(out_ref.dtype)


def _norm(x, gain, *, tm, name):
    m, d = x.shape
    blocks = _nbytes((tm, d), F32) + _nbytes((tm, d), BF16)
    return pl.pallas_call(
        _norm_kernel,
        grid=(m // tm,),
        in_specs=[pl.BlockSpec((tm, d), lambda i: (i, 0)),
                  pl.BlockSpec((1, d), lambda i: (0, 0))],
        out_specs=pl.BlockSpec((tm, d), lambda i: (i, 0)),
        out_shape=jax.ShapeDtypeStruct((m, d), BF16),
        compiler_params=_params(("parallel",), blocks, 0, 2 * _nbytes((tm, d), F32)),
        name=name,
    )(x, gain)


def _cast_kernel(w_ref, out_ref):
    out_ref[...] = w_ref[...].astype(out_ref.dtype)


def _cast_cols(w, l, cols, *, tr, tc, name):
    rows = w.shape[1]
    blocks = _nbytes((tr, tc), w.dtype) + _nbytes((tr, tc), BF16)
    return pl.pallas_call(
        _cast_kernel,
        grid=(rows // tr, cols // tc),
        in_specs=[pl.BlockSpec((None, tr, tc), lambda i, j: (l, i, j))],
        out_specs=pl.BlockSpec((tr, tc), lambda i, j: (i, j)),
        out_shape=jax.ShapeDtypeStruct((rows, cols), BF16),
        compiler_params=_params(("parallel", "parallel"), blocks, 0, 0),
        name=name,
    )(w)


def _mm_kernel(*refs, has_scale, has_bias, has_res):
    a_ref, w_ref = refs[0], refs[1]
    out_ref = refs[-1]
    extra = list(refs[2:-1])
    y = jnp.dot(a_ref[...], w_ref[...], preferred_element_type=F32)
    if has_scale:
        y = y * extra.pop(0)[...]
    if has_bias:
        y = jax.nn.sigmoid(y + extra.pop(0)[...])
    if has_res:
        y = y + extra.pop(0)[...]
    out_ref[...] = y.astype(out_ref.dtype)


def _mm(a, w, *, tm, tn, out_dtype, name, scale=None, bias=None, res=None):
    m, k = a.shape
    n = w.shape[1]
    ins = [a, w]
    in_specs = [pl.BlockSpec((tm, k), lambda i, j: (i, 0)),
                pl.BlockSpec((k, tn), lambda i, j: (0, j))]
    blocks = _nbytes((tm, k), a.dtype) + _nbytes((k, tn), w.dtype) + _nbytes((tm, tn), out_dtype)
    for vec in (scale, bias):
        if vec is not None:
            ins.append(vec)
            in_specs.append(pl.BlockSpec((1, tn), lambda i, j: (0, j)))
    if res is not None:
        ins.append(res)
        in_specs.append(pl.BlockSpec((tm, tn), lambda i, j: (i, j)))
        blocks += _nbytes((tm, tn), res.dtype)
    kern = functools.partial(_mm_kernel, has_scale=scale is not None,
                             has_bias=bias is not None, has_res=res is not None)
    return pl.pallas_call(
        kern,
        grid=(m // tm, n // tn),
        in_specs=in_specs,
        out_specs=pl.BlockSpec((tm, tn), lambda i, j: (i, j)),
        out_shape=jax.ShapeDtypeStruct((m, n), out_dtype),
        compiler_params=_params(("parallel", "arbitrary"), blocks, 0, 4 * _nbytes((tm, tn), F32)),
        name=name,
    )(*ins)


def _kpe_kernel(h_ref, w_ref, cos_ref, sin_ref, out_ref):
    y = jnp.dot(h_ref[...], w_ref[...], preferred_element_type=F32)
    out_ref[...] = (y[:, :LANES] * cos_ref[...] + y[:, LANES:] * sin_ref[...]).astype(out_ref.dtype)


def _kpe(h, w, cos, sin, *, tm, name):
    m, d = h.shape
    nt = cos.shape[0] // tm
    blocks = _nbytes((tm, d), BF16) + _nbytes((d, 2 * LANES), BF16) + 3 * _nbytes((tm, LANES), F32)
    return pl.pallas_call(
        _kpe_kernel,
        grid=(m // tm,),
        in_specs=[pl.BlockSpec((tm, d), lambda i: (i, 0)),
                  pl.BlockSpec((d, 2 * LANES), lambda i: (0, 0)),
                  pl.BlockSpec((tm, LANES), lambda i: (i % nt, 0)),
                  pl.BlockSpec((tm, LANES), lambda i: (i % nt, 0))],
        out_specs=pl.BlockSpec((tm, LANES), lambda i: (i, 0)),
        out_shape=jax.ShapeDtypeStruct((m, LANES), BF16),
        compiler_params=_params(("parallel",), blocks, 0, _nbytes((tm, 2 * LANES), F32)),
        name=name,
    )(h, w, cos, sin)


def _mlaq_kernel(c_ref, g_ref, w_ref, cos_ref, sin_ref, out_ref, cn_scr, *, heads, scale):
    @pl.when(pl.program_id(1) == 0)
    def _():
        cn_scr[...] = _rms(c_ref[...].astype(F32), g_ref[...]).astype(BF16)

    y = jnp.dot(cn_scr[...], w_ref[...], preferred_element_type=F32)
    cos = cos_ref[...]
    sin = sin_ref[...]
    for h in range(heads):
        s0 = h * MLA_SLOT
        out_ref[:, s0:s0 + NOPE_D] = (y[:, s0:s0 + NOPE_D] * scale).astype(out_ref.dtype)
        hi = y[:, s0 + NOPE_D:s0 + MLA_SLOT]
        pe = hi * cos + pltpu.roll(hi, LANES // 2, axis=1) * sin
        out_ref[:, s0 + NOPE_D:s0 + MLA_SLOT] = (pe * scale).astype(out_ref.dtype)


def _mlaq(p, col_block, gain, w, cos, sin, *, tm, heads, scale, name):
    m = p.shape[0]
    k = gain.shape[1]
    wn = heads * MLA_SLOT
    groups = w.shape[1] // wn
    nt = cos.shape[0] // tm
    kern = functools.partial(_mlaq_kernel, heads=heads, scale=scale)
    blocks = (_nbytes((tm, k), BF16) + _nbytes((k, wn), BF16) + 2 * _nbytes((tm, LANES), F32)
              + _nbytes((tm, heads * MLA_SLOT), BF16))
    return pl.pallas_call(
        kern,
        grid=(m // tm, groups),
        in_specs=[pl.BlockSpec((tm, k), lambda i, j: (i, col_block)),
                  pl.BlockSpec((1, k), lambda i, j: (0, 0)),
                  pl.BlockSpec((k, wn), lambda i, j: (0, j)),
                  pl.BlockSpec((tm, LANES), lambda i, j: (i % nt, 0)),
                  pl.BlockSpec((tm, LANES), lambda i, j: (i % nt, 0))],
        out_specs=pl.BlockSpec((tm, heads * MLA_SLOT), lambda i, j: (i, j)),
        out_shape=jax.ShapeDtypeStruct((m, groups * heads * MLA_SLOT), BF16),
        scratch_shapes=[pltpu.VMEM((tm, k), BF16)],
        compiler_params=_params(("parallel", "arbitrary"), blocks, _nbytes((tm, k), BF16),
                                2 * _nbytes((tm, wn), F32)),
        name=name,
    )(p, gain, w, cos, sin)


def _mlakv_kernel(c_ref, g_ref, w_ref, out_ref, cn_scr):
    @pl.when(pl.program_id(1) == 0)
    def _():
        cn_scr[...] = _rms(c_ref[...].astype(F32), g_ref[...]).astype(BF16)

    out_ref[...] = jnp.dot(cn_scr[...], w_ref[...], preferred_element_type=F32).astype(out_ref.dtype)


def _mlakv(p, col_block, gain, w, *, tm, tn, name):
    m = p.shape[0]
    k, n = w.shape
    blocks = _nbytes((tm, k), BF16) + _nbytes((k, tn), BF16) + _nbytes((tm, tn), BF16)
    return pl.pallas_call(
        _mlakv_kernel,
        grid=(m // tm, n // tn),
        in_specs=[pl.BlockSpec((tm, k), lambda i, j: (i, col_block)),
                  pl.BlockSpec((1, k), lambda i, j: (0, 0)),
                  pl.BlockSpec((k, tn), lambda i, j: (0, j))],
        out_specs=pl.BlockSpec((tm, tn), lambda i, j: (i, j)),
        out_shape=jax.ShapeDtypeStruct((m, n), BF16),
        scratch_shapes=[pltpu.VMEM((tm, k), BF16)],
        compiler_params=_params(("parallel", "arbitrary"), blocks, _nbytes((tm, k), BF16),
                                2 * _nbytes((tm, tn), F32)),
        name=name,
    )(p, gain, w)


def _dot_nt(a, b):
    return lax.dot_general(a, b, (((1,), (1,)), ((), ())), preferred_element_type=F32)


def _transpose_bf16(x):
    return x.astype(F32).T.astype(BF16)


def _softmax_first_t(s_t, v_t):
    m = jnp.max(s_t, axis=0, keepdims=True)
    p = jnp.exp2(s_t - m)
    return m, jnp.sum(p, axis=0, keepdims=True), jnp.dot(v_t, p.astype(BF16), preferred_element_type=F32)


def _softmax_next_t(s_t, shift, v_t, m, l, acc):
    m_new = jnp.maximum(m, jnp.max(s_t, axis=0, keepdims=True) + shift)
    alpha = jnp.exp2(m - m_new)
    p = jnp.exp2(s_t - (m_new - shift))
    l = alpha * l + jnp.sum(p, axis=0, keepdims=True)
    acc = alpha * acc + jnp.dot(v_t, p.astype(BF16), preferred_element_type=F32)
    return m_new, l, acc


def _meta_mask_t(s_t):
    row = lax.broadcasted_iota(jnp.int32, s_t.shape, 0)
    return jnp.where(row < N_META, s_t, NEG_BIG)


def _diff_attn_kernel(q_ref, k_ref, v_ref, km_ref, vm_ref, slope_ref, lq1_ref, lk1_ref, lq2_ref,
                      lk2_ref, subln_ref, out_ref, vt_scr, bias_scr, *, lam_init):
    tq = q_ref.shape[0]
    tk = tq
    nk = k_ref.shape[0] // tk
    hd = DIFF_HEAD_DIM
    qi = pl.program_id(2)
    slope2 = slope_ref[...] * LOG2E

    @pl.when(qi == 0)
    def _():
        vt_scr[:, :META_PAD] = _transpose_bf16(vm_ref[...])
        for c in range(nk):
            vt_scr[:, META_PAD + c * tk:META_PAD + (c + 1) * tk] = _transpose_bf16(
                v_ref[c * tk:(c + 1) * tk, :])
        d = (lax.broadcasted_iota(jnp.int32, (tk, tq), 1)
             - lax.broadcasted_iota(jnp.int32, (tk, tq), 0)).astype(F32) * slope2
        bias_scr[0] = -d
        bias_scr[1] = d
        bias_scr[2] = -jnp.abs(d)

    q1 = q_ref[:, :hd]
    q2 = q_ref[:, hd:]

    def qk(j):
        kb = k_ref[j * tk:(j + 1) * tk, :]
        return _dot_nt(kb[:, :hd], q1), _dot_nt(kb[:, hd:], q2)

    vtm = vt_scr[:, :META_PAD]
    st1 = _softmax_first_t(_meta_mask_t(_dot_nt(km_ref[:, :hd], q1)), vtm)
    st2 = _softmax_first_t(_meta_mask_t(_dot_nt(km_ref[:, hd:], q2)), vtm)

    s1, s2 = qk(0)
    for j in range(nk):
        nxt = qk(j + 1) if j + 1 < nk else None
        bias = bias_scr[jnp.where(j < qi, 0, jnp.where(j == qi, 2, 1))]
        shift = -jnp.abs(qi * tq - j * tk).astype(F32) * slope2
        vt = vt_scr[:, META_PAD + j * tk:META_PAD + (j + 1) * tk]
        st1 = _softmax_next_t(s1 + bias, shift, vt, *st1)
        st2 = _softmax_next_t(s2 + bias, shift, vt, *st2)
        if nxt is not None:
            s1, s2 = nxt

    lam = (jnp.exp(jnp.sum(lq1_ref[...] * lk1_ref[...], axis=-1, keepdims=True))
           - jnp.exp(jnp.sum(lq2_ref[...] * lk2_ref[...], axis=-1, keepdims=True)) + lam_init)
    o_t = st1[2] / st1[1] - lam * (st2[2] / st2[1])
    out_ref[...] = (_rms(o_t.T, subln_ref[...]) * (1.0 - lam_init)).astype(out_ref.dtype)


def _diff_attn(p, pm, slopes, lq1, lk1, lq2, lk2, subln, *, batch, seq, tq, lam_init, name):
    m = p.shape[0]
    hw = 2 * DIFF_HEAD_DIM
    nq = seq // tq
    koff = DIFF_HEADS
    voff = 2 * DIFF_HEADS
    kern = functools.partial(_diff_attn_kernel, lam_init=lam_init)
    vec = pl.BlockSpec((1, DIFF_HEAD_DIM), lambda b, h, i: (0, 0))
    blocks = (2 * _nbytes((tq, hw), BF16) + 2 * _nbytes((seq, hw), BF16)
              + 2 * _nbytes((META_PAD, hw), BF16))
    scratch = _nbytes((hw, seq + META_PAD), BF16) + 3 * _nbytes((tq, tq), F32)
    return pl.pallas_call(
        kern,
        grid=(batch, DIFF_HEADS, nq),
        in_specs=[pl.BlockSpec((tq, hw), lambda b, h, i: (b * nq + i, h)),
                  pl.BlockSpec((seq, hw), lambda b, h, i: (b, koff + h)),
                  pl.BlockSpec((seq, hw), lambda b, h, i: (b, voff + h)),
                  pl.BlockSpec((META_PAD, hw), lambda b, h, i: (0, koff + h)),
                  pl.BlockSpec((META_PAD, hw), lambda b, h, i: (0, voff + h)),
                  pl.BlockSpec((None, 1, 1), lambda b, h, i: (h, 0, 0)),
                  vec, vec, vec, vec,
                  pl.BlockSpec((1, hw), lambda b, h, i: (0, 0))],
        out_specs=pl.BlockSpec((tq, hw), lambda b, h, i: (b * nq + i, h)),
        out_shape=jax.ShapeDtypeStruct((m, DIFF_HEADS * hw), BF16),
        scratch_shapes=[pltpu.VMEM((hw, seq + META_PAD), BF16),
                        pltpu.VMEM((3, tq, tq), F32)],
        compiler_params=_params(("parallel", "parallel", "arbitrary"), blocks, scratch,
                                12 * _nbytes((tq, tq), F32)),
        name=name,
    )(p, p, p, pm, pm, slopes, lq1, lk1, lq2, lk2, subln)


def _mla_attn_kernel(q_ref, kv_ref, kpe_ref, kvm_ref, kpem_ref, out_ref, k_scr, vt_scr, *, tk):
    nk = kv_ref.shape[0] // tk

    @pl.when(pl.program_id(2) == 0)
    def _():
        k_scr[:META_PAD, :NOPE_D] = kvm_ref[:, :NOPE_D]
        k_scr[:META_PAD, NOPE_D:] = kpem_ref[...]
        k_scr[META_PAD:, :NOPE_D] = kv_ref[:, :NOPE_D]
        k_scr[META_PAD:, NOPE_D:] = kpe_ref[...]
        vt_scr[:, :META_PAD] = _transpose_bf16(kvm_ref[:, NOPE_D:])
        for c in range(nk):
            vt_scr[:, META_PAD + c * tk:META_PAD + (c + 1) * tk] = _transpose_bf16(
                kv_ref[c * tk:(c + 1) * tk, NOPE_D:])

    q = q_ref[...]

    def qk(j):
        return _dot_nt(k_scr[META_PAD + j * tk:META_PAD + (j + 1) * tk, :], q)

    st = _softmax_first_t(_meta_mask_t(_dot_nt(k_scr[:META_PAD, :], q)), vt_scr[:, :META_PAD])

    s = qk(0)
    for j in range(nk):
        nxt = qk(j + 1) if j + 1 < nk else None
        st = _softmax_next_t(s, 0.0, vt_scr[:, META_PAD + j * tk:META_PAD + (j + 1) * tk], *st)
        s = nxt

    out_ref[...] = (st[2] / st[1]).T.astype(out_ref.dtype)


def _mla_attn(q, kv, kpe, kvm, kpem, *, batch, seq, tq, tk, name):
    mrows = q.shape[0]
    nq = seq // tq
    kern = functools.partial(_mla_attn_kernel, tk=tk)
    blocks = (_nbytes((tq, MLA_SLOT), BF16) + _nbytes((seq, MLA_SLOT), BF16)
              + _nbytes((seq, LANES), BF16) + _nbytes((META_PAD, MLA_SLOT + LANES), BF16)
              + _nbytes((tq, MLA_V_D), BF16))
    scratch = _nbytes((seq + META_PAD, MLA_SLOT), BF16) + _nbytes((MLA_V_D, seq + META_PAD), BF16)
    return pl.pallas_call(
        kern,
        grid=(batch, MLA_HEADS, nq),
        in_specs=[pl.BlockSpec((tq, MLA_SLOT), lambda b, h, i: (b * nq + i, h)),
                  pl.BlockSpec((seq, MLA_SLOT), lambda b, h, i: (b, h)),
                  pl.BlockSpec((seq, LANES), lambda b, h, i: (b, 0)),
                  pl.BlockSpec((META_PAD, MLA_SLOT), lambda b, h, i: (0, h)),
                  pl.BlockSpec((META_PAD, LANES), lambda b, h, i: (0, 0))],
        out_specs=pl.BlockSpec((tq, MLA_V_D), lambda b, h, i: (b * nq + i, h)),
        out_shape=jax.ShapeDtypeStruct((mrows, MLA_HEADS * MLA_V_D), BF16),
        scratch_shapes=[pltpu.VMEM((seq + META_PAD, MLA_SLOT), BF16),
                        pltpu.VMEM((MLA_V_D, seq + META_PAD), BF16)],
        compiler_params=_params(("parallel", "parallel", "arbitrary"), blocks, scratch,
                                6 * _nbytes((tk, tq), F32)),
        name=name,
    )(q, kv, kpe, kvm, kpem)


def _merge_kernel(od_ref, om_ref, wa_ref, wb_ref, gd_ref, gm_ref, out_ref):
    ya = jnp.dot(od_ref[...], wa_ref[...], preferred_element_type=F32)
    yb = jnp.dot(om_ref[...], wb_ref[...], preferred_element_type=F32)
    out_ref[...] = (gd_ref[...].astype(F32) * ya + gm_ref[...].astype(F32) * yb).astype(out_ref.dtype)


def _merge(od, om, wa, wb, gates, *, tm, tn, name):
    m, k = od.shape
    n = wa.shape[1]
    nj = n // tn
    blocks = 2 * _nbytes((tm, k), BF16) + 2 * _nbytes((k, tn), BF16) + 3 * _nbytes((tm, tn), BF16)
    return pl.pallas_call(
        _merge_kernel,
        grid=(m // tm, nj),
        in_specs=[pl.BlockSpec((tm, k), lambda i, j: (i, 0)),
                  pl.BlockSpec((tm, k), lambda i, j: (i, 0)),
                  pl.BlockSpec((k, tn), lambda i, j: (0, j)),
                  pl.BlockSpec((k, tn), lambda i, j: (0, j)),
                  pl.BlockSpec((tm, tn), lambda i, j: (i, j)),
                  pl.BlockSpec((tm, tn), lambda i, j: (i, nj + j))],
        out_specs=pl.BlockSpec((tm, tn), lambda i, j: (i, j)),
        out_shape=jax.ShapeDtypeStruct((m, n), BF16),
        compiler_params=_params(("parallel", "arbitrary"), blocks, 0, 4 * _nbytes((tm, tn), F32)),
        name=name,
    )(od, om, wa, wb, gates, gates)


def _rope_tables(t):
    inv_freq = 1.0 / (ROPE_THETA ** (jnp.arange(0, ROPE_D, 2, dtype=F32) / ROPE_D))
    ang = jnp.arange(t).astype(F32)[:, None] * inv_freq[None, :]
    pad = jnp.zeros((t, LANES - ROPE_D), F32)
    cos = jnp.concatenate([jnp.cos(ang), jnp.cos(ang), pad], axis=-1)
    sin = jnp.concatenate([jnp.sin(ang), jnp.sin(ang), pad], axis=-1)
    return cos, sin


def _rotate_cols(w):
    half = ROPE_D // 2
    return jnp.concatenate([-w[..., half:], w[..., :half]], axis=-1)


def kernel(x, meta_tokens, ffn1_norm, ffn1_w_gate, ffn1_w_up, ffn1_w_down, mix_norm, w_in, diff_lambda_q1, diff_lambda_k1, diff_lambda_q2, diff_lambda_k2, diff_subln, mla_q_norm, mla_w_uq, mla_kv_norm, mla_w_ukv, w_gate, b_gate, w_branch_diff, w_branch_mla, w_out, ffn2_norm, ffn2_w_gate, ffn2_w_up, ffn2_w_down, final_norm):
    batch, seq, d = x.shape
    depth = ffn1_norm.shape[0]
    m = batch * seq
    dq_w = DIFF_HEADS * 2 * DIFF_HEAD_DIM
    dv_w = DIFF_HEADS * DIFF_V_DIM
    main_w = 2 * dq_w + dv_w + Q_LORA + KV_LORA
    cq_block = (2 * dq_w + dv_w) // Q_LORA
    ckv_block = (2 * dq_w + dv_w + Q_LORA) // KV_LORA
    q_heads = 4

    cos, sin = _rope_tables(N_META + seq)
    cos_m, sin_m, cos_r, sin_r = cos[:N_META], sin[:N_META], cos[N_META:], sin[N_META:]
    slopes = jnp.asarray([2.0 ** (-8.0 * (h + 1) / DIFF_HEADS) for h in range(DIFF_HEADS)],
                         F32).reshape(DIFF_HEADS, 1, 1)
    col_scale = jnp.concatenate([jnp.full((1, dq_w), DIFF_HEAD_DIM ** -0.5 * LOG2E, F32),
                                 jnp.ones((1, main_w - dq_w), F32)], axis=-1)
    row = lambda v: v.reshape(1, -1).astype(F32)

    xs = x.reshape(m, d)
    xm = meta_tokens.astype(x.dtype)

    for l in range(depth):
        lam_init = 0.8 - 0.6 * math.exp(-0.3 * l)
        wg1, wu1, wd1 = (w[l].astype(BF16) for w in (ffn1_w_gate, ffn1_w_up, ffn1_w_down))
        wg2, wu2, wd2 = (w[l].astype(BF16) for w in (ffn2_w_gate, ffn2_w_up, ffn2_w_down))
        w_main = _cast_cols(w_in, l, main_w, tr=512, tc=768, name="cast_w_in")
        w_kr = w_in[l][:, main_w:]
        zpad = jnp.zeros((d, LANES - ROPE_D), F32)
        w_kpe = jnp.concatenate([w_kr, zpad, _rotate_cols(w_kr), zpad], axis=-1).astype(BF16)

        wuq = mla_w_uq[l].reshape(Q_LORA, MLA_HEADS, NOPE_D + ROPE_D)
        w_q = jnp.concatenate([wuq, _rotate_cols(wuq[..., NOPE_D:])], axis=-1)
        w_q = w_q.reshape(Q_LORA, MLA_HEADS * MLA_SLOT).astype(BF16)
        w_ukv = mla_w_ukv[l].astype(BF16)
        w_g = w_gate[l].astype(BF16)
        w_pa = w_branch_diff[l].astype(BF16)
        w_pb = w_branch_mla[l].astype(BF16)
        w_o = w_out[l].astype(BF16)

        xs = _ffn(xs, row(ffn1_norm[l]), wg1, wu1, wd1, row(final_norm),
                  tm=512, tf=256, final_norm=False, name="ffn1")
        xm = _ffn(xm, row(ffn1_norm[l]), wg1, wu1, wd1, row(final_norm),
                  tm=N_META, tf=256, final_norm=False, name="ffn1_meta")

        hs = _norm(xs, row(mix_norm[l]), tm=512, name="mix_norm")
        hm = _norm(xm, row(mix_norm[l]), tm=N_META, name="mix_norm_meta")

        ps = _mm(hs, w_main, tm=1024, tn=768, out_dtype=BF16, scale=col_scale, name="proj")
        pm = _mm(hm, w_main, tm=N_META, tn=768, out_dtype=BF16, scale=col_scale, name="proj_meta")
        gates = _mm(hs, w_g, tm=1024, tn=512, out_dtype=BF16, bias=row(b_gate[l]), name="gates")
        kpe_s = _kpe(hs, w_kpe, cos_r, sin_r, tm=1024, name="kpe")
        kpe_m = _kpe(hm, w_kpe, cos_m, sin_m, tm=N_META, name="kpe_meta")

        q_mla = _mlaq(ps, cq_block, row(mla_q_norm[l]), w_q, cos_r, sin_r, tm=512, heads=q_heads,
                      scale=(NOPE_D + ROPE_D) ** -0.5 * LOG2E, name="mla_q")
        kv_s = _mlakv(ps, ckv_block, row(mla_kv_norm[l]), w_ukv, tm=1024, tn=1024, name="mla_kv")
        kv_m = _mlakv(pm, ckv_block, row(mla_kv_norm[l]), w_ukv, tm=N_META, tn=1024, name="mla_kv_meta")

        pad_rows = lambda a: jnp.pad(a, ((0, META_PAD - N_META), (0, 0)))
        o_diff = _diff_attn(ps, pad_rows(pm), slopes, row(diff_lambda_q1[l]), row(diff_lambda_k1[l]),
                            row(diff_lambda_q2[l]), row(diff_lambda_k2[l]), row(diff_subln[l]),
                            batch=batch, seq=seq, tq=256, lam_init=lam_init, name="diff_attn")
        o_mla = _mla_attn(q_mla, kv_s, kpe_s, pad_rows(kv_m), pad_rows(kpe_m),
                          batch=batch, seq=seq, tq=2048, tk=512, name="mla_attn")

        merged = _merge(o_diff, o_mla, w_pa, w_pb, gates, tm=1024, tn=512, name="merge")
        xs = _mm(merged, w_o, tm=1024, tn=512, out_dtype=F32, res=xs, name="out_proj")

        last = l == depth - 1
        xs = _ffn(xs, row(ffn2_norm[l]), wg2, wu2, wd2, row(final_norm),
                  tm=512, tf=256, final_norm=last, name="ffn2")
        if not last:
            raise NotImplementedError("DEPTH > 1 needs the meta-row query path")

    return xs.reshape(batch, seq, d)
```

```python
import functools
import math
from typing import NamedTuple

import numpy as np
import jax
import jax.numpy as jnp
from jax import lax
from jax.experimental import pallas as pl
from jax.experimental.pallas import tpu as pltpu

N_META = 16
EPS = 1e-6
DIFF_HEADS = 8
DIFF_HEAD_DIM = 128
DIFF_V_DIM = 256
MLA_HEADS = 16
Q_LORA = 1024
KV_LORA = 512
NOPE_D = 128
ROPE_D = 64
MLA_V_D = 128
ROPE_THETA = 10000.0

LANES = 128
MLA_SLOT = 256
META_PAD = 128
VMEM_CAP = 60000 * 1024
VMEM_PHYSICAL = 64 * 2**20
VMEM_RESERVE = 2 * 2**20
VMEM_FLOOR = 16 * 2**20
NEG_BIG = -1e30
LOG2E = math.log2(math.e)

F32 = jnp.float32
BF16 = jnp.bfloat16


def _nbytes(shape, dtype):
    return int(np.prod(shape)) * jnp.dtype(dtype).itemsize


def _params(semantics, block_bytes, scratch_bytes, temp_bytes):
    need = 2 * block_bytes + scratch_bytes + temp_bytes
    return pltpu.CompilerParams(dimension_semantics=semantics,
                                vmem_limit_bytes=int(min(VMEM_CAP, max(need, VMEM_FLOOR))))


def _rms(x, gain):
    return x * lax.rsqrt(jnp.mean(x * x, axis=-1, keepdims=True) + EPS) * gain


def _ffn_kernel(x_ref, g_ref, wga_ref, wua_ref, wda_ref, wgb_ref, wub_ref, wdb_ref, gf_ref, out_ref,
                h_scr, *, final_norm, nchunks):
    f = pl.program_id(1)
    last = pl.num_programs(1) - 1

    @pl.when(f == 0)
    def _():
        x = x_ref[...]
        h_scr[...] = _rms(x, g_ref[...]).astype(BF16)
        out_ref[...] = x

    def act(wg_ref, wu_ref):
        h = h_scr[...]
        g = jnp.dot(h, wg_ref[...], preferred_element_type=F32)
        u = jnp.dot(h, wu_ref[...], preferred_element_type=F32)
        return (0.5 * (g * jax.nn.sigmoid(g)) * u).astype(BF16)

    def both():
        a = act(wga_ref, wua_ref)
        b = act(wgb_ref, wub_ref)
        out_ref[...] += (jnp.dot(a, wda_ref[...], preferred_element_type=F32)
                         + jnp.dot(b, wdb_ref[...], preferred_element_type=F32))

    if nchunks % 2:
        pl.when(f < last)(both)

        @pl.when(f == last)
        def _():
            out_ref[...] += jnp.dot(act(wga_ref, wua_ref), wda_ref[...], preferred_element_type=F32)
    else:
        both()

    if final_norm:
        @pl.when(f == last)
        def _():
            out_ref[...] = _rms(out_ref[...], gf_ref[...])


def _ffn(x, gain, wg, wu, wd, final_gain, *, tm, tf, final_norm, name):
    m, d = x.shape
    nchunks = wg.shape[1] // tf
    kern = functools.partial(_ffn_kernel, final_norm=final_norm, nchunks=nchunks)
    col_a = lambda i, f: (0, 2 * f)
    col_b = lambda i, f: (0, jnp.minimum(2 * f + 1, nchunks - 1))
    row_a = lambda i, f: (2 * f, 0)
    row_b = lambda i, f: (jnp.minimum(2 * f + 1, nchunks - 1), 0)
    weights = 2 * (2 * _nbytes((d, tf), BF16) + _nbytes((tf, d), BF16))
    need = (_nbytes((tm, d), F32) + 2 * _nbytes((tm, d), F32) + 2 * weights
            + _nbytes((tm, d), BF16) + 6 * _nbytes((tm, tf), F32) + _nbytes((tm, d), F32))
    return pl.pallas_call(
        kern,
        grid=(m // tm, (nchunks + 1) // 2),
        in_specs=[
            pl.BlockSpec((tm, d), lambda i, f: (i, 0), pipeline_mode=pl.Buffered(1)),
            pl.BlockSpec((1, d), lambda i, f: (0, 0)),
            pl.BlockSpec((d, tf), col_a), pl.BlockSpec((d, tf), col_a), pl.BlockSpec((tf, d), row_a),
            pl.BlockSpec((d, tf), col_b), pl.BlockSpec((d, tf), col_b), pl.BlockSpec((tf, d), row_b),
            pl.BlockSpec((1, d), lambda i, f: (0, 0)),
        ],
        out_specs=pl.BlockSpec((tm, d), lambda i, f: (i, 0)),
        out_shape=jax.ShapeDtypeStruct((m, d), F32),
        scratch_shapes=[pltpu.VMEM((tm, d), BF16)],
        compiler_params=pltpu.CompilerParams(
            dimension_semantics=("parallel", "arbitrary"),
            vmem_limit_bytes=int(min(VMEM_PHYSICAL - VMEM_RESERVE, max(need, VMEM_FLOOR)))),
        name=name,
    )(x, gain, wg, wu, wd, wg, wu, wd, final_gain)


def _norm_kernel(x_ref, g_ref, out_ref):
    out_ref[...] = _rms(x_ref[...], g_ref[...]).astype(out_ref.dtype)


def _norm(x, gain, *, tm, name):
    m, d = x.shape
    blocks = _nbytes((tm, d), F32) + _nbytes((tm, d), BF16)
    return pl.pallas_call(
        _norm_kernel,
        grid=(m // tm,),
        in_specs=[pl.BlockSpec((tm, d), lambda i: (i, 0)),
                  pl.BlockSpec((1, d), lambda i: (0, 0))],
        out_specs=pl.BlockSpec((tm, d), lambda i: (i, 0)),
        out_shape=jax.ShapeDtypeStruct((m, d), BF16),
        compiler_params=_params(("parallel",), blocks, 0, 2 * _nbytes((tm, d), F32)),
        name=name,
    )(x, gain)


def _cast_kernel(w_ref, out_ref):
    out_ref[...] = w_ref[...].astype(out_ref.dtype)


def _cast_cols(w, l, cols, *, tr, tc, name):
    rows = w.shape[1]
    blocks = _nbytes((tr, tc), w.dtype) + _nbytes((tr, tc), BF16)
    return pl.pallas_call(
        _cast_kernel,
        grid=(rows // tr, cols // tc),
        in_specs=[pl.BlockSpec((None, tr, tc), lambda i, j: (l, i, j))],
        out_specs=pl.BlockSpec((tr, tc), lambda i, j: (i, j)),
        out_shape=jax.ShapeDtypeStruct((rows, cols), BF16),
        compiler_params=_params(("parallel", "parallel"), blocks, 0, 0),
        name=name,
    )(w)


def _tail_cols(w, l, start, *, tr, name):
    rows, cols = w.shape[1:]
    assert start % LANES == 0 and cols - start <= LANES
    blocks = 2 * _nbytes((tr, LANES), F32)
    out = pl.pallas_call(
        _cast_kernel,
        grid=(rows // tr,),
        in_specs=[pl.BlockSpec((None, tr, LANES), lambda i: (l, i, start // LANES))],
        out_specs=pl.BlockSpec((tr, LANES), lambda i: (i, 0)),
        out_shape=jax.ShapeDtypeStruct((rows, LANES), F32),
        compiler_params=_params(("parallel",), blocks, 0, 0),
        name=name,
    )(w)
    return out[:, :cols - start]


class SideCast(NamedTuple):
    w: jax.Array
    l: int
    br: int
    bc: int


def _side(w, l, steps, bc=None):
    rows, cols = w.shape[1:]
    bc = cols if bc is None else bc
    sublanes = 16
    br = next(r for r in range(sublanes, rows + 1, sublanes)
              if rows % r == 0 and (rows // r) * (cols // bc) <= steps)
    return SideCast(w, l, br, bc)


def _side_cast_specs(side, grid):
    rows, cols = side.w.shape[1:]
    ncols = cols // side.bc
    nblocks = (rows // side.br) * ncols
    assert rows % side.br == 0 and cols % side.bc == 0 and nblocks <= grid[0] * grid[1], (side.w.shape, grid)

    def block(i, j):
        t = jnp.minimum(i * grid[1] + j, nblocks - 1)
        return t // ncols, t % ncols

    in_spec = pl.BlockSpec((None, side.br, side.bc), lambda i, j: (side.l, *block(i, j)))
    out_spec = pl.BlockSpec((side.br, side.bc), block)
    nbytes = _nbytes((side.br, side.bc), F32) + _nbytes((side.br, side.bc), BF16)
    return in_spec, out_spec, jax.ShapeDtypeStruct((rows, cols), BF16), nbytes


def _side_cast_run(side_in_refs, side_out_refs):
    for src, dst in zip(side_in_refs, side_out_refs):
        dst[...] = src[...].astype(dst.dtype)


def _mm_kernel(*refs, has_scale, has_bias, has_res, n_side):
    n_in = 2 + has_scale + has_bias + has_res + n_side
    a_ref, w_ref = refs[0], refs[1]
    extra = list(refs[2:n_in - n_side])
    out_ref = refs[n_in]
    y = jnp.dot(a_ref[...], w_ref[...], preferred_element_type=F32)
    if has_scale:
        y = y * extra.pop(0)[...]
    if has_bias:
        y = jax.nn.sigmoid(y + extra.pop(0)[...])
    if has_res:
        y = y + extra.pop(0)[...]
    out_ref[...] = y.astype(out_ref.dtype)
    _side_cast_run(refs[n_in - n_side:n_in], refs[n_in + 1:])


def _mm(a, w, *, tm, tn, out_dtype, name, scale=None, bias=None, res=None, sides=()):
    m, k = a.shape
    n = w.shape[1]
    grid = (m // tm, n // tn)
    ins = [a, w]
    in_specs = [pl.BlockSpec((tm, k), lambda i, j: (i, 0)),
                pl.BlockSpec((k, tn), lambda i, j: (0, j))]
    blocks = _nbytes((tm, k), a.dtype) + _nbytes((k, tn), w.dtype) + _nbytes((tm, tn), out_dtype)
    for vec in (scale, bias):
        if vec is not None:
            ins.append(vec)
            in_specs.append(pl.BlockSpec((1, tn), lambda i, j: (0, j)))
    if res is not None:
        ins.append(res)
        in_specs.append(pl.BlockSpec((tm, tn), lambda i, j: (i, j)))
        blocks += _nbytes((tm, tn), res.dtype)
    out_specs = [pl.BlockSpec((tm, tn), lambda i, j: (i, j))]
    out_shape = [jax.ShapeDtypeStruct((m, n), out_dtype)]
    for side in sides:
        in_spec, out_spec, shape, nbytes = _side_cast_specs(side, grid)
        ins.append(side.w)
        in_specs.append(in_spec)
        out_specs.append(out_spec)
        out_shape.append(shape)
        blocks += nbytes
    kern = functools.partial(_mm_kernel, has_scale=scale is not None, has_bias=bias is not None,
                             has_res=res is not None, n_side=len(sides))
    outs = pl.pallas_call(
        kern,
        grid=grid,
        in_specs=in_specs,
        out_specs=out_specs,
        out_shape=out_shape,
        compiler_params=_params(("parallel", "arbitrary"), blocks, 0, 4 * _nbytes((tm, tn), F32)),
        name=name,
    )(*ins)
    return outs if sides else outs[0]


def _kpe_kernel(h_ref, w_ref, cos_ref, sin_ref, out_ref):
    y = jnp.dot(h_ref[...], w_ref[...], preferred_element_type=F32)
    out_ref[...] = (y[:, :LANES] * cos_ref[...] + y[:, LANES:] * sin_ref[...]).astype(out_ref.dtype)


def _kpe(h, w, cos, sin, *, tm, name):
    m, d = h.shape
    nt = cos.shape[0] // tm
    blocks = _nbytes((tm, d), BF16) + _nbytes((d, 2 * LANES), BF16) + 3 * _nbytes((tm, LANES), F32)
    return pl.pallas_call(
        _kpe_kernel,
        grid=(m // tm,),
        in_specs=[pl.BlockSpec((tm, d), lambda i: (i, 0)),
                  pl.BlockSpec((d, 2 * LANES), lambda i: (0, 0)),
                  pl.BlockSpec((tm, LANES), lambda i: (i % nt, 0)),
                  pl.BlockSpec((tm, LANES), lambda i: (i % nt, 0))],
        out_specs=pl.BlockSpec((tm, LANES), lambda i: (i, 0)),
        out_shape=jax.ShapeDtypeStruct((m, LANES), BF16),
        compiler_params=_params(("parallel",), blocks, 0, _nbytes((tm, 2 * LANES), F32)),
        name=name,
    )(h, w, cos, sin)


def _mlaq_kernel(c_ref, g_ref, w_ref, cos_ref, sin_ref, out_ref, cn_scr, *, heads, scale):
    @pl.when(pl.program_id(1) == 0)
    def _():
        cn_scr[...] = _rms(c_ref[...].astype(F32), g_ref[...]).astype(BF16)

    y = jnp.dot(cn_scr[...], w_ref[...], preferred_element_type=F32)
    cos = cos_ref[...]
    sin = sin_ref[...]
    for h in range(heads):
        s0 = h * MLA_SLOT
        out_ref[:, s0:s0 + NOPE_D] = (y[:, s0:s0 + NOPE_D] * scale).astype(out_ref.dtype)
        hi = y[:, s0 + NOPE_D:s0 + MLA_SLOT]
        pe = hi * cos + pltpu.roll(hi, LANES // 2, axis=1) * sin
        out_ref[:, s0 + NOPE_D:s0 + MLA_SLOT] = (pe * scale).astype(out_ref.dtype)


def _mlaq(p, col_block, gain, w, cos, sin, *, tm, heads, scale, name):
    m = p.shape[0]
    k = gain.shape[1]
    wn = heads * MLA_SLOT
    groups = w.shape[1] // wn
    nt = cos.shape[0] // tm
    kern = functools.partial(_mlaq_kernel, heads=heads, scale=scale)
    blocks = (_nbytes((tm, k), BF16) + _nbytes((k, wn), BF16) + 2 * _nbytes((tm, LANES), F32)
              + _nbytes((tm, heads * MLA_SLOT), BF16))
    return pl.pallas_call(
        kern,
        grid=(m // tm, groups),
        in_specs=[pl.BlockSpec((tm, k), lambda i, j: (i, col_block)),
                  pl.BlockSpec((1, k), lambda i, j: (0, 0)),
                  pl.BlockSpec((k, wn), lambda i, j: (0, j)),
                  pl.BlockSpec((tm, LANES), lambda i, j: (i % nt, 0)),
                  pl.BlockSpec((tm, LANES), lambda i, j: (i % nt, 0))],
        out_specs=pl.BlockSpec((tm, heads * MLA_SLOT), lambda i, j: (i, j)),
        out_shape=jax.ShapeDtypeStruct((m, groups * heads * MLA_SLOT), BF16),
        scratch_shapes=[pltpu.VMEM((tm, k), BF16)],
        compiler_params=_params(("parallel", "arbitrary"), blocks, _nbytes((tm, k), BF16),
                                2 * _nbytes((tm, wn), F32)),
        name=name,
    )(p, gain, w, cos, sin)


def _mlakv_kernel(c_ref, g_ref, w_ref, out_ref, cn_scr):
    @pl.when(pl.program_id(1) == 0)
    def _():
        cn_scr[...] = _rms(c_ref[...].astype(F32), g_ref[...]).astype(BF16)

    out_ref[...] = jnp.dot(cn_scr[...], w_ref[...], preferred_element_type=F32).astype(out_ref.dtype)


def _mlakv(p, col_block, gain, w, *, tm, tn, name):
    m = p.shape[0]
    k, n = w.shape
    blocks = _nbytes((tm, k), BF16) + _nbytes((k, tn), BF16) + _nbytes((tm, tn), BF16)
    return pl.pallas_call(
        _mlakv_kernel,
        grid=(m // tm, n // tn),
        in_specs=[pl.BlockSpec((tm, k), lambda i, j: (i, col_block)),
                  pl.BlockSpec((1, k), lambda i, j: (0, 0)),
                  pl.BlockSpec((k, tn), lambda i, j: (0, j))],
        out_specs=pl.BlockSpec((tm, tn), lambda i, j: (i, j)),
        out_shape=jax.ShapeDtypeStruct((m, n), BF16),
        scratch_shapes=[pltpu.VMEM((tm, k), BF16)],
        compiler_params=_params(("parallel", "arbitrary"), blocks, _nbytes((tm, k), BF16),
                                2 * _nbytes((tm, tn), F32)),
        name=name,
    )(p, gain, w)


def _dot_nt(a, b):
    return lax.dot_general(a, b, (((1,), (1,)), ((), ())), preferred_element_type=F32)


def _transpose_bf16(x):
    return x.astype(F32).T.astype(BF16)


def _softmax_first_t(s_t, v_t):
    m = jnp.max(s_t, axis=0, keepdims=True)
    p = jnp.exp2(s_t - m)
    return m, jnp.sum(p, axis=0, keepdims=True), jnp.dot(v_t, p.astype(BF16), preferred_element_type=F32)


def _softmax_next_t(s_t, shift, v_t, m, l, acc):
    m_new = jnp.maximum(m, jnp.max(s_t, axis=0, keepdims=True) + shift)
    alpha = jnp.exp2(m - m_new)
    p = jnp.exp2(s_t - (m_new - shift))
    l = alpha * l + jnp.sum(p, axis=0, keepdims=True)
    acc = alpha * acc + jnp.dot(v_t, p.astype(BF16), preferred_element_type=F32)
    return m_new, l, acc


def _meta_mask_t(s_t):
    row = lax.broadcasted_iota(jnp.int32, s_t.shape, 0)
    return jnp.where(row < N_META, s_t, NEG_BIG)


def _diff_attn_kernel(q_ref, k_ref, v_ref, km_ref, vm_ref, slope_ref, lq1_ref, lk1_ref, lq2_ref,
                      lk2_ref, subln_ref, out_ref, vt_scr, bias_scr, *, lam_init):
    tq = q_ref.shape[0]
    tk = tq
    nk = k_ref.shape[0] // tk
    hd = DIFF_HEAD_DIM
    qi = pl.program_id(2)
    slope2 = slope_ref[...] * LOG2E

    @pl.when(qi == 0)
    def _():
        vt_scr[:, :META_PAD] = _transpose_bf16(vm_ref[...])
        for c in range(nk):
            vt_scr[:, META_PAD + c * tk:META_PAD + (c + 1) * tk] = _transpose_bf16(
                v_ref[c * tk:(c + 1) * tk, :])
        d = (lax.broadcasted_iota(jnp.int32, (tk, tq), 1)
             - lax.broadcasted_iota(jnp.int32, (tk, tq), 0)).astype(F32) * slope2
        bias_scr[0] = -d
        bias_scr[1] = d
        bias_scr[2] = -jnp.abs(d)

    q1 = q_ref[:, :hd]
    q2 = q_ref[:, hd:]

    def qk(j):
        kb = k_ref[j * tk:(j + 1) * tk, :]
        return _dot_nt(kb[:, :hd], q1), _dot_nt(kb[:, hd:], q2)

    vtm = vt_scr[:, :META_PAD]
    st1 = _softmax_first_t(_meta_mask_t(_dot_nt(km_ref[:, :hd], q1)), vtm)
    st2 = _softmax_first_t(_meta_mask_t(_dot_nt(km_ref[:, hd:], q2)), vtm)

    s1, s2 = qk(0)
    for j in range(nk):
        nxt = qk(j + 1) if j + 1 < nk else None
        bias = bias_scr[jnp.where(j < qi, 0, jnp.where(j == qi, 2, 1))]
        shift = -jnp.abs(qi * tq - j * tk).astype(F32) * slope2
        vt = vt_scr[:, META_PAD + j * tk:META_PAD + (j + 1) * tk]
        st1 = _softmax_next_t(s1 + bias, shift, vt, *st1)
        st2 = _softmax_next_t(s2 + bias, shift, vt, *st2)
        if nxt is not None:
            s1, s2 = nxt

    lam = (jnp.exp(jnp.sum(lq1_ref[...] * lk1_ref[...], axis=-1, keepdims=True))
           - jnp.exp(jnp.sum(lq2_ref[...] * lk2_ref[...], axis=-1, keepdims=True)) + lam_init)
    o_t = st1[2] / st1[1] - lam * (st2[2] / st2[1])
    out_ref[...] = (_rms(o_t.T, subln_ref[...]) * (1.0 - lam_init)).astype(out_ref.dtype)


def _diff_attn(p, pm, slopes, lq1, lk1, lq2, lk2, subln, *, batch, seq, tq, lam_init, name):
    m = p.shape[0]
    hw = 2 * DIFF_HEAD_DIM
    nq = seq // tq
    koff = DIFF_HEADS
    voff = 2 * DIFF_HEADS
    kern = functools.partial(_diff_attn_kernel, lam_init=lam_init)
    vec = pl.BlockSpec((1, DIFF_HEAD_DIM), lambda b, h, i: (0, 0))
    blocks = (2 * _nbytes((tq, hw), BF16) + 2 * _nbytes((seq, hw), BF16)
              + 2 * _nbytes((META_PAD, hw), BF16))
    scratch = _nbytes((hw, seq + META_PAD), BF16) + 3 * _nbytes((tq, tq), F32)
    return pl.pallas_call(
        kern,
        grid=(batch, DIFF_HEADS, nq),
        in_specs=[pl.BlockSpec((tq, hw), lambda b, h, i: (b * nq + i, h)),
                  pl.BlockSpec((seq, hw), lambda b, h, i: (b, koff + h)),
                  pl.BlockSpec((seq, hw), lambda b, h, i: (b, voff + h)),
                  pl.BlockSpec((META_PAD, hw), lambda b, h, i: (0, koff + h)),
                  pl.BlockSpec((META_PAD, hw), lambda b, h, i: (0, voff + h)),
                  pl.BlockSpec((None, 1, 1), lambda b, h, i: (h, 0, 0)),
                  vec, vec, vec, vec,
                  pl.BlockSpec((1, hw), lambda b, h, i: (0, 0))],
        out_specs=pl.BlockSpec((tq, hw), lambda b, h, i: (b * nq + i, h)),
        out_shape=jax.ShapeDtypeStruct((m, DIFF_HEADS * hw), BF16),
        scratch_shapes=[pltpu.VMEM((hw, seq + META_PAD), BF16),
                        pltpu.VMEM((3, tq, tq), F32)],
        compiler_params=_params(("parallel", "parallel", "arbitrary"), blocks, scratch,
                                12 * _nbytes((tq, tq), F32)),
        name=name,
    )(p, p, p, pm, pm, slopes, lq1, lk1, lq2, lk2, subln)


def _mla_attn_kernel(q_ref, kv_ref, kpe_ref, kvm_ref, kpem_ref, out_ref, k_scr, vt_scr, *, tk):
    nk = kv_ref.shape[0] // tk

    @pl.when(pl.program_id(2) == 0)
    def _():
        k_scr[:META_PAD, :NOPE_D] = kvm_ref[:, :NOPE_D]
        k_scr[:META_PAD, NOPE_D:] = kpem_ref[...]
        k_scr[META_PAD:, :NOPE_D] = kv_ref[:, :NOPE_D]
        k_scr[META_PAD:, NOPE_D:] = kpe_ref[...]
        vt_scr[:, :META_PAD] = _transpose_bf16(kvm_ref[:, NOPE_D:])
        for c in range(nk):
            vt_scr[:, META_PAD + c * tk:META_PAD + (c + 1) * tk] = _transpose_bf16(
                kv_ref[c * tk:(c + 1) * tk, NOPE_D:])

    q = q_ref[...]

    def qk(j):
        return _dot_nt(k_scr[META_PAD + j * tk:META_PAD + (j + 1) * tk, :], q)

    st = _softmax_first_t(_meta_mask_t(_dot_nt(k_scr[:META_PAD, :], q)), vt_scr[:, :META_PAD])

    s = qk(0)
    for j in range(nk):
        nxt = qk(j + 1) if j + 1 < nk else None
        st = _softmax_next_t(s, 0.0, vt_scr[:, META_PAD + j * tk:META_PAD + (j + 1) * tk], *st)
        s = nxt

    out_ref[...] = (st[2] / st[1]).T.astype(out_ref.dtype)


def _mla_attn(q, kv, kpe, kvm, kpem, *, batch, seq, tq, tk, name):
    mrows = q.shape[0]
    nq = seq // tq
    kern = functools.partial(_mla_attn_kernel, tk=tk)
    blocks = (_nbytes((tq, MLA_SLOT), BF16) + _nbytes((seq, MLA_SLOT), BF16)
              + _nbytes((seq, LANES), BF16) + _nbytes((META_PAD, MLA_SLOT + LANES), BF16)
              + _nbytes((tq, MLA_V_D), BF16))
    scratch = _nbytes((seq + META_PAD, MLA_SLOT), BF16) + _nbytes((MLA_V_D, seq + META_PAD), BF16)
    return pl.pallas_call(
        kern,
        grid=(batch, MLA_HEADS, nq),
        in_specs=[pl.BlockSpec((tq, MLA_SLOT), lambda b, h, i: (b * nq + i, h)),
                  pl.BlockSpec((seq, MLA_SLOT), lambda b, h, i: (b, h)),
                  pl.BlockSpec((seq, LANES), lambda b, h, i: (b, 0)),
                  pl.BlockSpec((META_PAD, MLA_SLOT), lambda b, h, i: (0, h)),
                  pl.BlockSpec((META_PAD, LANES), lambda b, h, i: (0, 0))],
        out_specs=pl.BlockSpec((tq, MLA_V_D), lambda b, h, i: (b * nq + i, h)),
        out_shape=jax.ShapeDtypeStruct((mrows, MLA_HEADS * MLA_V_D), BF16),
        scratch_shapes=[pltpu.VMEM((seq + META_PAD, MLA_SLOT), BF16),
                        pltpu.VMEM((MLA_V_D, seq + META_PAD), BF16)],
        compiler_params=_params(("parallel", "parallel", "arbitrary"), blocks, scratch,
                                6 * _nbytes((tk, tq), F32)),
        name=name,
    )(q, kv, kpe, kvm, kpem)


def _merge_kernel(od_ref, om_ref, wa_ref, wb_ref, gd_ref, gm_ref, side_ref, out_ref, side_out_ref):
    ya = jnp.dot(od_ref[...], wa_ref[...], preferred_element_type=F32)
    yb = jnp.dot(om_ref[...], wb_ref[...], preferred_element_type=F32)
    out_ref[...] = (gd_ref[...].astype(F32) * ya + gm_ref[...].astype(F32) * yb).astype(out_ref.dtype)
    _side_cast_run([side_ref], [side_out_ref])


def _merge(od, om, wa, wb, gates, side, *, tm, tn, name):
    m, k = od.shape
    n = wa.shape[1]
    nj = n // tn
    grid = (m // tm, nj)
    side_in, side_out, side_shape, side_bytes = _side_cast_specs(side, grid)
    blocks = (2 * _nbytes((tm, k), BF16) + 2 * _nbytes((k, tn), BF16) + 3 * _nbytes((tm, tn), BF16)
              + side_bytes)
    return pl.pallas_call(
        _merge_kernel,
        grid=grid,
        in_specs=[pl.BlockSpec((tm, k), lambda i, j: (i, 0)),
                  pl.BlockSpec((tm, k), lambda i, j: (i, 0)),
                  pl.BlockSpec((k, tn), lambda i, j: (0, j)),
                  pl.BlockSpec((k, tn), lambda i, j: (0, j)),
                  pl.BlockSpec((tm, tn), lambda i, j: (i, j)),
                  pl.BlockSpec((tm, tn), lambda i, j: (i, nj + j)),
                  side_in],
        out_specs=[pl.BlockSpec((tm, tn), lambda i, j: (i, j)), side_out],
        out_shape=[jax.ShapeDtypeStruct((m, n), BF16), side_shape],
        compiler_params=_params(("parallel", "arbitrary"), blocks, 0, 4 * _nbytes((tm, tn), F32)),
        name=name,
    )(od, om, wa, wb, gates, gates, side.w)


def _rope_tables(t):
    inv_freq = 1.0 / (ROPE_THETA ** (jnp.arange(0, ROPE_D, 2, dtype=F32) / ROPE_D))
    ang = jnp.arange(t).astype(F32)[:, None] * inv_freq[None, :]
    pad = jnp.zeros((t, LANES - ROPE_D), F32)
    cos = jnp.concatenate([jnp.cos(ang), jnp.cos(ang), pad], axis=-1)
    sin = jnp.concatenate([jnp.sin(ang), jnp.sin(ang), pad], axis=-1)
    return cos, sin


def _rotate_cols(w):
    half = ROPE_D // 2
    return jnp.concatenate([-w[..., half:], w[..., :half]], axis=-1)


def kernel(x, meta_tokens, ffn1_norm, ffn1_w_gate, ffn1_w_up, ffn1_w_down, mix_norm, w_in, diff_lambda_q1, diff_lambda_k1, diff_lambda_q2, diff_lambda_k2, diff_subln, mla_q_norm, mla_w_uq, mla_kv_norm, mla_w_ukv, w_gate, b_gate, w_branch_diff, w_branch_mla, w_out, ffn2_norm, ffn2_w_gate, ffn2_w_up, ffn2_w_down, final_norm):
    batch, seq, d = x.shape
    depth = ffn1_norm.shape[0]
    m = batch * seq
    dq_w = DIFF_HEADS * 2 * DIFF_HEAD_DIM
    dv_w = DIFF_HEADS * DIFF_V_DIM
    main_w = 2 * dq_w + dv_w + Q_LORA + KV_LORA
    cq_block = (2 * dq_w + dv_w) // Q_LORA
    ckv_block = (2 * dq_w + dv_w + Q_LORA) // KV_LORA
    q_heads = 4

    cos, sin = _rope_tables(N_META + seq)
    cos_m, sin_m, cos_r, sin_r = cos[:N_META], sin[:N_META], cos[N_META:], sin[N_META:]
    slopes = jnp.asarray([2.0 ** (-8.0 * (h + 1) / DIFF_HEADS) for h in range(DIFF_HEADS)],
                         F32).reshape(DIFF_HEADS, 1, 1)
    col_scale = jnp.concatenate([jnp.full((1, dq_w), DIFF_HEAD_DIM ** -0.5 * LOG2E, F32),
                                 jnp.ones((1, main_w - dq_w), F32)], axis=-1)
    row = lambda v: v.reshape(1, -1).astype(F32)

    xs = x.reshape(m, d)
    xm = meta_tokens.astype(x.dtype)

    for l in range(depth):
        lam_init = 0.8 - 0.6 * math.exp(-0.3 * l)
        wg1, wu1, wd1 = (w[l].astype(BF16) for w in (ffn1_w_gate, ffn1_w_up, ffn1_w_down))
        w_main = _cast_cols(w_in, l, main_w, tr=512, tc=768, name="cast_w_in")
        w_kr = _tail_cols(w_in, l, main_w, tr=512, name="w_in_rope_cols")
        zpad = jnp.zeros((d, LANES - ROPE_D), F32)
        w_kpe = jnp.concatenate([w_kr, zpad, _rotate_cols(w_kr), zpad], axis=-1).astype(BF16)

        wuq = mla_w_uq[l].reshape(Q_LORA, MLA_HEADS, NOPE_D + ROPE_D)
        w_q = jnp.concatenate([wuq, _rotate_cols(wuq[..., NOPE_D:])], axis=-1)
        w_q = w_q.reshape(Q_LORA, MLA_HEADS * MLA_SLOT).astype(BF16)
        w_ukv = mla_w_ukv[l].astype(BF16)
        w_pa = w_branch_diff[l].astype(BF16)
        w_pb = w_branch_mla[l].astype(BF16)
        w_o = w_out[l].astype(BF16)

        xs = _ffn(xs, row(ffn1_norm[l]), wg1, wu1, wd1, row(final_norm),
                  tm=512, tf=256, final_norm=False, name="ffn1")
        xm = _ffn(xm, row(ffn1_norm[l]), wg1, wu1, wd1, row(final_norm),
                  tm=N_META, tf=256, final_norm=False, name="ffn1_meta")

        hs = _norm(xs, row(mix_norm[l]), tm=512, name="mix_norm")
        hm = _norm(xm, row(mix_norm[l]), tm=N_META, name="mix_norm_meta")

        ps, wg2, w_g = _mm(hs, w_main, tm=1024, tn=768, out_dtype=BF16, scale=col_scale, name="proj",
                           sides=(_side(ffn2_w_gate, l, (m // 1024) * (main_w // 768)),
                                  _side(w_gate, l, (m // 1024) * (main_w // 768))))
        pm = _mm(hm, w_main, tm=N_META, tn=768, out_dtype=BF16, scale=col_scale, name="proj_meta")
        gates, wu2 = _mm(hs, w_g, tm=1024, tn=1024, out_dtype=BF16, bias=row(b_gate[l]), name="gates",
                         sides=(_side(ffn2_w_up, l, (m // 1024) * (2 * d // 1024)),))
        kpe_s = _kpe(hs, w_kpe, cos_r, sin_r, tm=1024, name="kpe")
        kpe_m = _kpe(hm, w_kpe, cos_m, sin_m, tm=N_META, name="kpe_meta")

        q_mla = _mlaq(ps, cq_block, row(mla_q_norm[l]), w_q, cos_r, sin_r, tm=512, heads=q_heads,
                      scale=(NOPE_D + ROPE_D) ** -0.5 * LOG2E, name="mla_q")
        kv_s = _mlakv(ps, ckv_block, row(mla_kv_norm[l]), w_ukv, tm=1024, tn=1024, name="mla_kv")
        kv_m = _mlakv(pm, ckv_block, row(mla_kv_norm[l]), w_ukv, tm=N_META, tn=1024, name="mla_kv_meta")

        pad_rows = lambda a: jnp.pad(a, ((0, META_PAD - N_META), (0, 0)))
        o_diff = _diff_attn(ps, pad_rows(pm), slopes, row(diff_lambda_q1[l]), row(diff_lambda_k1[l]),
                            row(diff_lambda_q2[l]), row(diff_lambda_k2[l]), row(diff_subln[l]),
                            batch=batch, seq=seq, tq=256, lam_init=lam_init, name="diff_attn")
        o_mla = _mla_attn(q_mla, kv_s, kpe_s, pad_rows(kv_m), pad_rows(kpe_m),
                          batch=batch, seq=seq, tq=2048, tk=512, name="mla_attn")

        merged, wd2 = _merge(o_diff, o_mla, w_pa, w_pb, gates,
                             _side(ffn2_w_down, l, (m // 1024) * (d // 512), bc=512),
                             tm=1024, tn=512, name="merge")
        xs = _mm(merged, w_o, tm=1024, tn=512, out_dtype=F32, res=xs, name="out_proj")

        last = l == depth - 1
        xs = _ffn(xs, row(ffn2_norm[l]), wg2, wu2, wd2, row(final_norm),
                  tm=512, tf=256, final_norm=last, name="ffn2")
        if not last:
            raise NotImplementedError("DEPTH > 1 needs the meta-row query path")

    return xs.reshape(batch, seq, d)
```

```python
import functools
import math
from typing import NamedTuple

import numpy as np
import jax
import jax.numpy as jnp
from jax import lax
from jax.experimental import pallas as pl
from jax.experimental.pallas import tpu as pltpu

N_META = 16
EPS = 1e-6
DIFF_HEADS = 8
DIFF_HEAD_DIM = 128
DIFF_V_DIM = 256
MLA_HEADS = 16
Q_LORA = 1024
KV_LORA = 512
NOPE_D = 128
ROPE_D = 64
MLA_V_D = 128
ROPE_THETA = 10000.0

LANES = 128
MLA_SLOT = 256
META_PAD = 128
VMEM_CAP = 60000 * 1024
VMEM_PHYSICAL = 64 * 2**20
VMEM_RESERVE = 2 * 2**20
VMEM_FLOOR = 16 * 2**20
NEG_BIG = -1e30
LOG2E = math.log2(math.e)

F32 = jnp.float32
BF16 = jnp.bfloat16


def _nbytes(shape, dtype):
    return int(np.prod(shape)) * jnp.dtype(dtype).itemsize


def _params(semantics, block_bytes, scratch_bytes, temp_bytes):
    need = 2 * block_bytes + scratch_bytes + temp_bytes
    return pltpu.CompilerParams(dimension_semantics=semantics,
                                vmem_limit_bytes=int(min(VMEM_CAP, max(need, VMEM_FLOOR))))


def _rms(x, gain):
    return x * lax.rsqrt(jnp.mean(x * x, axis=-1, keepdims=True) + EPS) * gain


def _ffn_kernel(x_ref, g_ref, wga_ref, wua_ref, wda_ref, wgb_ref, wub_ref, wdb_ref, gf_ref, out_ref,
                h_scr, *, final_norm, nchunks):
    f = pl.program_id(1)
    last = pl.num_programs(1) - 1

    @pl.when(f == 0)
    def _():
        x = x_ref[...]
        h_scr[...] = _rms(x, g_ref[...]).astype(BF16)
        out_ref[...] = x

    def act(wg_ref, wu_ref):
        h = h_scr[...]
        g = jnp.dot(h, wg_ref[...], preferred_element_type=F32)
        u = jnp.dot(h, wu_ref[...], preferred_element_type=F32)
        return (0.5 * (g * jax.nn.sigmoid(g)) * u).astype(BF16)

    def both():
        a = act(wga_ref, wua_ref)
        b = act(wgb_ref, wub_ref)
        out_ref[...] += (jnp.dot(a, wda_ref[...], preferred_element_type=F32)
                         + jnp.dot(b, wdb_ref[...], preferred_element_type=F32))

    if nchunks % 2:
        pl.when(f < last)(both)

        @pl.when(f == last)
        def _():
            out_ref[...] += jnp.dot(act(wga_ref, wua_ref), wda_ref[...], preferred_element_type=F32)
    else:
        both()

    if final_norm:
        @pl.when(f == last)
        def _():
            out_ref[...] = _rms(out_ref[...], gf_ref[...])


def _ffn(x, gain, wg, wu, wd, final_gain, *, tm, tf, final_norm, name):
    m, d = x.shape
    nchunks = wg.shape[1] // tf
    kern = functools.partial(_ffn_kernel, final_norm=final_norm, nchunks=nchunks)
    col_a = lambda i, f: (0, 2 * f)
    col_b = lambda i, f: (0, jnp.minimum(2 * f + 1, nchunks - 1))
    row_a = lambda i, f: (2 * f, 0)
    row_b = lambda i, f: (jnp.minimum(2 * f + 1, nchunks - 1), 0)
    weights = 2 * (2 * _nbytes((d, tf), BF16) + _nbytes((tf, d), BF16))
    need = (_nbytes((tm, d), F32) + 2 * _nbytes((tm, d), F32) + 2 * weights
            + _nbytes((tm, d), BF16) + 6 * _nbytes((tm, tf), F32) + _nbytes((tm, d), F32))
    return pl.pallas_call(
        kern,
        grid=(m // tm, (nchunks + 1) // 2),
        in_specs=[
            pl.BlockSpec((tm, d), lambda i, f: (i, 0), pipeline_mode=pl.Buffered(1)),
            pl.BlockSpec((1, d), lambda i, f: (0, 0)),
            pl.BlockSpec((d, tf), col_a), pl.BlockSpec((d, tf), col_a), pl.BlockSpec((tf, d), row_a),
            pl.BlockSpec((d, tf), col_b), pl.BlockSpec((d, tf), col_b), pl.BlockSpec((tf, d), row_b),
            pl.BlockSpec((1, d), lambda i, f: (0, 0)),
        ],
        out_specs=pl.BlockSpec((tm, d), lambda i, f: (i, 0)),
        out_shape=jax.ShapeDtypeStruct((m, d), F32),
        scratch_shapes=[pltpu.VMEM((tm, d), BF16)],
        compiler_params=pltpu.CompilerParams(
            dimension_semantics=("parallel", "arbitrary"),
            vmem_limit_bytes=int(min(VMEM_PHYSICAL - VMEM_RESERVE, max(need, VMEM_FLOOR)))),
        name=name,
    )(x, gain, wg, wu, wd, wg, wu, wd, final_gain)


def _norm_kernel(x_ref, g_ref, out_ref):
    out_ref[...] = _rms(x_ref[...], g_ref[...]).astype(out_ref.dtype)


def _norm(x, gain, *, tm, name):
    m, d = x.shape
    blocks = _nbytes((tm, d), F32) + _nbytes((tm, d), BF16)
    return pl.pallas_call(
        _norm_kernel,
        grid=(m // tm,),
        in_specs=[pl.BlockSpec((tm, d), lambda i: (i, 0)),
                  pl.BlockSpec((1, d), lambda i: (0, 0))],
        out_specs=pl.BlockSpec((tm, d), lambda i: (i, 0)),
        out_shape=jax.ShapeDtypeStruct((m, d), BF16),
        compiler_params=_params(("parallel",), blocks, 0, 2 * _nbytes((tm, d), F32)),
        name=name,
    )(x, gain)


def _cast_kernel(w_ref, out_ref):
    out_ref[...] = w_ref[...].astype(out_ref.dtype)


def _cast_cols(w, l, cols, *, tr, tc, name):
    rows = w.shape[1]
    blocks = _nbytes((tr, tc), w.dtype) + _nbytes((tr, tc), BF16)
    return pl.pallas_call(
        _cast_kernel,
        grid=(rows // tr, cols // tc),
        in_specs=[pl.BlockSpec((None, tr, tc), lambda i, j: (l, i, j))],
        out_specs=pl.BlockSpec((tr, tc), lambda i, j: (i, j)),
        out_shape=jax.ShapeDtypeStruct((rows, cols), BF16),
        compiler_params=_params(("parallel", "parallel"), blocks, 0, 0),
        name=name,
    )(w)


def _tail_cols(w, l, start, *, tr, name):
    rows, cols = w.shape[1:]
    assert start % LANES == 0 and cols - start <= LANES
    blocks = 2 * _nbytes((tr, LANES), F32)
    out = pl.pallas_call(
        _cast_kernel,
        grid=(rows // tr,),
        in_specs=[pl.BlockSpec((None, tr, LANES), lambda i: (l, i, start // LANES))],
        out_specs=pl.BlockSpec((tr, LANES), lambda i: (i, 0)),
        out_shape=jax.ShapeDtypeStruct((rows, LANES), F32),
        compiler_params=_params(("parallel",), blocks, 0, 0),
        name=name,
    )(w)
    return out[:, :cols - start]


class SideCast(NamedTuple):
    w: jax.Array
    l: int
    br: int
    bc: int


def _side(w, l, steps, bc=None):
    rows, cols = w.shape[1:]
    bc = cols if bc is None else bc
    sublanes = 16
    br = next(r for r in range(sublanes, rows + 1, sublanes)
              if rows % r == 0 and (rows // r) * (cols // bc) <= steps)
    return SideCast(w, l, br, bc)


def _side_cast_specs(side, grid):
    rows, cols = side.w.shape[1:]
    ncols = cols // side.bc
    nblocks = (rows // side.br) * ncols
    assert rows % side.br == 0 and cols % side.bc == 0 and nblocks <= math.prod(grid), (side.w.shape, grid)

    def block(*idx):
        step = 0
        for i, n in zip(idx, grid):
            step = step * n + i
        t = jnp.minimum(step, nblocks - 1)
        return t // ncols, t % ncols

    in_spec = pl.BlockSpec((None, side.br, side.bc), lambda *idx: (side.l, *block(*idx)))
    out_spec = pl.BlockSpec((side.br, side.bc), block)
    nbytes = _nbytes((side.br, side.bc), F32) + _nbytes((side.br, side.bc), BF16)
    return in_spec, out_spec, jax.ShapeDtypeStruct((rows, cols), BF16), nbytes


def _side_cast_run(side_in_refs, side_out_refs):
    for src, dst in zip(side_in_refs, side_out_refs):
        dst[...] = src[...].astype(dst.dtype)


def _mm_kernel(*refs, has_scale, has_bias, has_res, n_side):
    n_in = 2 + has_scale + has_bias + has_res + n_side
    a_ref, w_ref = refs[0], refs[1]
    extra = list(refs[2:n_in - n_side])
    out_ref = refs[n_in]
    y = jnp.dot(a_ref[...], w_ref[...], preferred_element_type=F32)
    if has_scale:
        y = y * extra.pop(0)[...]
    if has_bias:
        y = jax.nn.sigmoid(y + extra.pop(0)[...])
    if has_res:
        y = y + extra.pop(0)[...]
    out_ref[...] = y.astype(out_ref.dtype)
    _side_cast_run(refs[n_in - n_side:n_in], refs[n_in + 1:])


def _mm(a, w, *, tm, tn, out_dtype, name, scale=None, bias=None, res=None, sides=()):
    m, k = a.shape
    n = w.shape[1]
    grid = (m // tm, n // tn)
    ins = [a, w]
    in_specs = [pl.BlockSpec((tm, k), lambda i, j: (i, 0)),
                pl.BlockSpec((k, tn), lambda i, j: (0, j))]
    blocks = _nbytes((tm, k), a.dtype) + _nbytes((k, tn), w.dtype) + _nbytes((tm, tn), out_dtype)
    for vec in (scale, bias):
        if vec is not None:
            ins.append(vec)
            in_specs.append(pl.BlockSpec((1, tn), lambda i, j: (0, j)))
    if res is not None:
        ins.append(res)
        in_specs.append(pl.BlockSpec((tm, tn), lambda i, j: (i, j)))
        blocks += _nbytes((tm, tn), res.dtype)
    out_specs = [pl.BlockSpec((tm, tn), lambda i, j: (i, j))]
    out_shape = [jax.ShapeDtypeStruct((m, n), out_dtype)]
    for side in sides:
        in_spec, out_spec, shape, nbytes = _side_cast_specs(side, grid)
        ins.append(side.w)
        in_specs.append(in_spec)
        out_specs.append(out_spec)
        out_shape.append(shape)
        blocks += nbytes
    kern = functools.partial(_mm_kernel, has_scale=scale is not None, has_bias=bias is not None,
                             has_res=res is not None, n_side=len(sides))
    outs = pl.pallas_call(
        kern,
        grid=grid,
        in_specs=in_specs,
        out_specs=out_specs,
        out_shape=out_shape,
        compiler_params=_params(("parallel", "arbitrary"), blocks, 0, 4 * _nbytes((tm, tn), F32)),
        name=name,
    )(*ins)
    return outs if sides else outs[0]


def _kpe_kernel(h_ref, w_ref, cos_ref, sin_ref, out_ref):
    y = jnp.dot(h_ref[...], w_ref[...], preferred_element_type=F32)
    out_ref[...] = (y[:, :LANES] * cos_ref[...] + y[:, LANES:] * sin_ref[...]).astype(out_ref.dtype)


def _kpe(h, w, cos, sin, *, tm, name):
    m, d = h.shape
    nt = cos.shape[0] // tm
    blocks = _nbytes((tm, d), BF16) + _nbytes((d, 2 * LANES), BF16) + 3 * _nbytes((tm, LANES), F32)
    return pl.pallas_call(
        _kpe_kernel,
        grid=(m // tm,),
        in_specs=[pl.BlockSpec((tm, d), lambda i: (i, 0)),
                  pl.BlockSpec((d, 2 * LANES), lambda i: (0, 0)),
                  pl.BlockSpec((tm, LANES), lambda i: (i % nt, 0)),
                  pl.BlockSpec((tm, LANES), lambda i: (i % nt, 0))],
        out_specs=pl.BlockSpec((tm, LANES), lambda i: (i, 0)),
        out_shape=jax.ShapeDtypeStruct((m, LANES), BF16),
        compiler_params=_params(("parallel",), blocks, 0, _nbytes((tm, 2 * LANES), F32)),
        name=name,
    )(h, w, cos, sin)


def _mlaq_kernel(c_ref, g_ref, w_ref, cos_ref, sin_ref, out_ref, cn_scr, *, heads, scale):
    @pl.when(pl.program_id(1) == 0)
    def _():
        cn_scr[...] = _rms(c_ref[...].astype(F32), g_ref[...]).astype(BF16)

    y = jnp.dot(cn_scr[...], w_ref[...], preferred_element_type=F32)
    cos = cos_ref[...]
    sin = sin_ref[...]
    for h in range(heads):
        s0 = h * MLA_SLOT
        out_ref[:, s0:s0 + NOPE_D] = (y[:, s0:s0 + NOPE_D] * scale).astype(out_ref.dtype)
        hi = y[:, s0 + NOPE_D:s0 + MLA_SLOT]
        pe = hi * cos + pltpu.roll(hi, LANES // 2, axis=1) * sin
        out_ref[:, s0 + NOPE_D:s0 + MLA_SLOT] = (pe * scale).astype(out_ref.dtype)


def _mlaq(p, col_block, gain, w, cos, sin, *, tm, heads, scale, name):
    m = p.shape[0]
    k = gain.shape[1]
    wn = heads * MLA_SLOT
    groups = w.shape[1] // wn
    nt = cos.shape[0] // tm
    kern = functools.partial(_mlaq_kernel, heads=heads, scale=scale)
    blocks = (_nbytes((tm, k), BF16) + _nbytes((k, wn), BF16) + 2 * _nbytes((tm, LANES), F32)
              + _nbytes((tm, heads * MLA_SLOT), BF16))
    return pl.pallas_call(
        kern,
        grid=(m // tm, groups),
        in_specs=[pl.BlockSpec((tm, k), lambda i, j: (i, col_block)),
                  pl.BlockSpec((1, k), lambda i, j: (0, 0)),
                  pl.BlockSpec((k, wn), lambda i, j: (0, j)),
                  pl.BlockSpec((tm, LANES), lambda i, j: (i % nt, 0)),
                  pl.BlockSpec((tm, LANES), lambda i, j: (i % nt, 0))],
        out_specs=pl.BlockSpec((tm, heads * MLA_SLOT), lambda i, j: (i, j)),
        out_shape=jax.ShapeDtypeStruct((m, groups * heads * MLA_SLOT), BF16),
        scratch_shapes=[pltpu.VMEM((tm, k), BF16)],
        compiler_params=_params(("parallel", "arbitrary"), blocks, _nbytes((tm, k), BF16),
                                2 * _nbytes((tm, wn), F32)),
        name=name,
    )(p, gain, w, cos, sin)


def _mlakv_kernel(c_ref, g_ref, w_ref, out_ref, cn_scr):
    @pl.when(pl.program_id(1) == 0)
    def _():
        cn_scr[...] = _rms(c_ref[...].astype(F32), g_ref[...]).astype(BF16)

    out_ref[...] = jnp.dot(cn_scr[...], w_ref[...], preferred_element_type=F32).astype(out_ref.dtype)


def _mlakv(p, col_block, gain, w, *, tm, tn, name):
    m = p.shape[0]
    k, n = w.shape
    blocks = _nbytes((tm, k), BF16) + _nbytes((k, tn), BF16) + _nbytes((tm, tn), BF16)
    return pl.pallas_call(
        _mlakv_kernel,
        grid=(m // tm, n // tn),
        in_specs=[pl.BlockSpec((tm, k), lambda i, j: (i, col_block)),
                  pl.BlockSpec((1, k), lambda i, j: (0, 0)),
                  pl.BlockSpec((k, tn), lambda i, j: (0, j))],
        out_specs=pl.BlockSpec((tm, tn), lambda i, j: (i, j)),
        out_shape=jax.ShapeDtypeStruct((m, n), BF16),
        scratch_shapes=[pltpu.VMEM((tm, k), BF16)],
        compiler_params=_params(("parallel", "arbitrary"), blocks, _nbytes((tm, k), BF16),
                                2 * _nbytes((tm, tn), F32)),
        name=name,
    )(p, gain, w)


def _dot_nt(a, b):
    return lax.dot_general(a, b, (((1,), (1,)), ((), ())), preferred_element_type=F32)


def _transpose_bf16(x):
    return x.astype(F32).T.astype(BF16)


def _softmax_first_t(s_t, v_t):
    m = jnp.max(s_t, axis=0, keepdims=True)
    p = jnp.exp2(s_t - m)
    return m, jnp.sum(p, axis=0, keepdims=True), jnp.dot(v_t, p.astype(BF16), preferred_element_type=F32)


def _softmax_next_t(s_t, shift, v_t, m, l, acc):
    m_new = jnp.maximum(m, jnp.max(s_t, axis=0, keepdims=True) + shift)
    alpha = jnp.exp2(m - m_new)
    p = jnp.exp2(s_t - (m_new - shift))
    l = alpha * l + jnp.sum(p, axis=0, keepdims=True)
    acc = alpha * acc + jnp.dot(v_t, p.astype(BF16), preferred_element_type=F32)
    return m_new, l, acc


def _meta_mask_t(s_t):
    row = lax.broadcasted_iota(jnp.int32, s_t.shape, 0)
    return jnp.where(row < N_META, s_t, NEG_BIG)


def _diff_attn_kernel(*refs, lam_init, n_side):
    (q_ref, k_ref, v_ref, km_ref, vm_ref, slope_ref, lq1_ref, lk1_ref, lq2_ref, lk2_ref,
     subln_ref) = refs[:11]
    out_ref = refs[11 + n_side]
    vt_scr, bias_scr = refs[12 + 2 * n_side:]
    _side_cast_run(refs[11:11 + n_side], refs[12 + n_side:12 + 2 * n_side])
    tq = q_ref.shape[0]
    tk = tq
    nk = k_ref.shape[0] // tk
    hd = DIFF_HEAD_DIM
    qi = pl.program_id(2)
    slope2 = slope_ref[...] * LOG2E

    @pl.when(qi == 0)
    def _():
        vt_scr[:, :META_PAD] = _transpose_bf16(vm_ref[...])
        for c in range(nk):
            vt_scr[:, META_PAD + c * tk:META_PAD + (c + 1) * tk] = _transpose_bf16(
                v_ref[c * tk:(c + 1) * tk, :])
        d = (lax.broadcasted_iota(jnp.int32, (tk, tq), 1)
             - lax.broadcasted_iota(jnp.int32, (tk, tq), 0)).astype(F32) * slope2
        bias_scr[0] = -d
        bias_scr[1] = d
        bias_scr[2] = -jnp.abs(d)

    q1 = q_ref[:, :hd]
    q2 = q_ref[:, hd:]

    def qk(j):
        kb = k_ref[j * tk:(j + 1) * tk, :]
        return _dot_nt(kb[:, :hd], q1), _dot_nt(kb[:, hd:], q2)

    vtm = vt_scr[:, :META_PAD]
    st1 = _softmax_first_t(_meta_mask_t(_dot_nt(km_ref[:, :hd], q1)), vtm)
    st2 = _softmax_first_t(_meta_mask_t(_dot_nt(km_ref[:, hd:], q2)), vtm)

    s1, s2 = qk(0)
    for j in range(nk):
        nxt = qk(j + 1) if j + 1 < nk else None
        bias = bias_scr[jnp.where(j < qi, 0, jnp.where(j == qi, 2, 1))]
        shift = -jnp.abs(qi * tq - j * tk).astype(F32) * slope2
        vt = vt_scr[:, META_PAD + j * tk:META_PAD + (j + 1) * tk]
        st1 = _softmax_next_t(s1 + bias, shift, vt, *st1)
        st2 = _softmax_next_t(s2 + bias, shift, vt, *st2)
        if nxt is not None:
            s1, s2 = nxt

    lam = (jnp.exp(jnp.sum(lq1_ref[...] * lk1_ref[...], axis=-1, keepdims=True))
           - jnp.exp(jnp.sum(lq2_ref[...] * lk2_ref[...], axis=-1, keepdims=True)) + lam_init)
    o_t = st1[2] / st1[1] - lam * (st2[2] / st2[1])
    out_ref[...] = (_rms(o_t.T, subln_ref[...]) * (1.0 - lam_init)).astype(out_ref.dtype)


def _diff_attn(p, pm, slopes, lq1, lk1, lq2, lk2, subln, *, batch, seq, tq, lam_init, name, sides=()):
    m = p.shape[0]
    hw = 2 * DIFF_HEAD_DIM
    nq = seq // tq
    koff = DIFF_HEADS
    voff = 2 * DIFF_HEADS
    grid = (batch, DIFF_HEADS, nq)
    side_specs = [_side_cast_specs(side, grid) for side in sides]
    kern = functools.partial(_diff_attn_kernel, lam_init=lam_init, n_side=len(sides))
    vec = pl.BlockSpec((1, DIFF_HEAD_DIM), lambda b, h, i: (0, 0))
    blocks = (2 * _nbytes((tq, hw), BF16) + 2 * _nbytes((seq, hw), BF16)
              + 2 * _nbytes((META_PAD, hw), BF16) + sum(spec[3] for spec in side_specs))
    scratch = _nbytes((hw, seq + META_PAD), BF16) + 3 * _nbytes((tq, tq), F32)
    outs = pl.pallas_call(
        kern,
        grid=grid,
        in_specs=[pl.BlockSpec((tq, hw), lambda b, h, i: (b * nq + i, h)),
                  pl.BlockSpec((seq, hw), lambda b, h, i: (b, koff + h)),
                  pl.BlockSpec((seq, hw), lambda b, h, i: (b, voff + h)),
                  pl.BlockSpec((META_PAD, hw), lambda b, h, i: (0, koff + h)),
                  pl.BlockSpec((META_PAD, hw), lambda b, h, i: (0, voff + h)),
                  pl.BlockSpec((None, 1, 1), lambda b, h, i: (h, 0, 0)),
                  vec, vec, vec, vec,
                  pl.BlockSpec((1, hw), lambda b, h, i: (0, 0))] + [spec[0] for spec in side_specs],
        out_specs=[pl.BlockSpec((tq, hw), lambda b, h, i: (b * nq + i, h))] + [spec[1] for spec in side_specs],
        out_shape=[jax.ShapeDtypeStruct((m, DIFF_HEADS * hw), BF16)] + [spec[2] for spec in side_specs],
        scratch_shapes=[pltpu.VMEM((hw, seq + META_PAD), BF16),
                        pltpu.VMEM((3, tq, tq), F32)],
        compiler_params=_params(("parallel", "parallel", "arbitrary"), blocks, scratch,
                                12 * _nbytes((tq, tq), F32)),
        name=name,
    )(p, p, p, pm, pm, slopes, lq1, lk1, lq2, lk2, subln, *[side.w for side in sides])
    return outs if sides else outs[0]


def _mla_attn_kernel(*refs, tk, n_side):
    q_ref, kv_ref, kpe_ref, kvm_ref, kpem_ref = refs[:5]
    out_ref = refs[5 + n_side]
    k_scr, vt_scr = refs[6 + 2 * n_side:]
    _side_cast_run(refs[5:5 + n_side], refs[6 + n_side:6 + 2 * n_side])
    nk = kv_ref.shape[0] // tk

    @pl.when(pl.program_id(2) == 0)
    def _():
        k_scr[:META_PAD, :NOPE_D] = kvm_ref[:, :NOPE_D]
        k_scr[:META_PAD, NOPE_D:] = kpem_ref[...]
        k_scr[META_PAD:, :NOPE_D] = kv_ref[:, :NOPE_D]
        k_scr[META_PAD:, NOPE_D:] = kpe_ref[...]
        vt_scr[:, :META_PAD] = _transpose_bf16(kvm_ref[:, NOPE_D:])
        for c in range(nk):
            vt_scr[:, META_PAD + c * tk:META_PAD + (c + 1) * tk] = _transpose_bf16(
                kv_ref[c * tk:(c + 1) * tk, NOPE_D:])

    q = q_ref[...]

    def qk(j):
        return _dot_nt(k_scr[META_PAD + j * tk:META_PAD + (j + 1) * tk, :], q)

    st = _softmax_first_t(_meta_mask_t(_dot_nt(k_scr[:META_PAD, :], q)), vt_scr[:, :META_PAD])

    s = qk(0)
    for j in range(nk):
        nxt = qk(j + 1) if j + 1 < nk else None
        st = _softmax_next_t(s, 0.0, vt_scr[:, META_PAD + j * tk:META_PAD + (j + 1) * tk], *st)
        s = nxt

    out_ref[...] = (st[2] / st[1]).T.astype(out_ref.dtype)


def _mla_attn(q, kv, kpe, kvm, kpem, *, batch, seq, tq, tk, name, sides=()):
    mrows = q.shape[0]
    nq = seq // tq
    grid = (batch, MLA_HEADS, nq)
    side_specs = [_side_cast_specs(side, grid) for side in sides]
    kern = functools.partial(_mla_attn_kernel, tk=tk, n_side=len(sides))
    blocks = (_nbytes((tq, MLA_SLOT), BF16) + _nbytes((seq, MLA_SLOT), BF16)
              + _nbytes((seq, LANES), BF16) + _nbytes((META_PAD, MLA_SLOT + LANES), BF16)
              + _nbytes((tq, MLA_V_D), BF16) + sum(spec[3] for spec in side_specs))
    scratch = _nbytes((seq + META_PAD, MLA_SLOT), BF16) + _nbytes((MLA_V_D, seq + META_PAD), BF16)
    outs = pl.pallas_call(
        kern,
        grid=grid,
        in_specs=[pl.BlockSpec((tq, MLA_SLOT), lambda b, h, i: (b * nq + i, h)),
                  pl.BlockSpec((seq, MLA_SLOT), lambda b, h, i: (b, h)),
                  pl.BlockSpec((seq, LANES), lambda b, h, i: (b, 0)),
                  pl.BlockSpec((META_PAD, MLA_SLOT), lambda b, h, i: (0, h)),
                  pl.BlockSpec((META_PAD, LANES), lambda b, h, i: (0, 0))] + [spec[0] for spec in side_specs],
        out_specs=[pl.BlockSpec((tq, MLA_V_D), lambda b, h, i: (b * nq + i, h))] + [spec[1] for spec in side_specs],
        out_shape=[jax.ShapeDtypeStruct((mrows, MLA_HEADS * MLA_V_D), BF16)] + [spec[2] for spec in side_specs],
        scratch_shapes=[pltpu.VMEM((seq + META_PAD, MLA_SLOT), BF16),
                        pltpu.VMEM((MLA_V_D, seq + META_PAD), BF16)],
        compiler_params=_params(("parallel", "parallel", "arbitrary"), blocks, scratch,
                                6 * _nbytes((tk, tq), F32)),
        name=name,
    )(q, kv, kpe, kvm, kpem, *[side.w for side in sides])
    return outs if sides else outs[0]


def _merge_kernel(od_ref, om_ref, wa_ref, wb_ref, gd_ref, gm_ref, side_ref, out_ref, side_out_ref):
    ya = jnp.dot(od_ref[...], wa_ref[...], preferred_element_type=F32)
    yb = jnp.dot(om_ref[...], wb_ref[...], preferred_element_type=F32)
    out_ref[...] = (gd_ref[...].astype(F32) * ya + gm_ref[...].astype(F32) * yb).astype(out_ref.dtype)
    _side_cast_run([side_ref], [side_out_ref])


def _merge(od, om, wa, wb, gates, side, *, tm, tn, name):
    m, k = od.shape
    n = wa.shape[1]
    nj = n // tn
    grid = (m // tm, nj)
    side_in, side_out, side_shape, side_bytes = _side_cast_specs(side, grid)
    blocks = (2 * _nbytes((tm, k), BF16) + 2 * _nbytes((k, tn), BF16) + 3 * _nbytes((tm, tn), BF16)
              + side_bytes)
    return pl.pallas_call(
        _merge_kernel,
        grid=grid,
        in_specs=[pl.BlockSpec((tm, k), lambda i, j: (i, 0)),
                  pl.BlockSpec((tm, k), lambda i, j: (i, 0)),
                  pl.BlockSpec((k, tn), lambda i, j: (0, j)),
                  pl.BlockSpec((k, tn), lambda i, j: (0, j)),
                  pl.BlockSpec((tm, tn), lambda i, j: (i, j)),
                  pl.BlockSpec((tm, tn), lambda i, j: (i, nj + j)),
                  side_in],
        out_specs=[pl.BlockSpec((tm, tn), lambda i, j: (i, j)), side_out],
        out_shape=[jax.ShapeDtypeStruct((m, n), BF16), side_shape],
        compiler_params=_params(("parallel", "arbitrary"), blocks, 0, 4 * _nbytes((tm, tn), F32)),
        name=name,
    )(od, om, wa, wb, gates, gates, side.w)


def _rope_tables(t):
    inv_freq = 1.0 / (ROPE_THETA ** (jnp.arange(0, ROPE_D, 2, dtype=F32) / ROPE_D))
    ang = jnp.arange(t).astype(F32)[:, None] * inv_freq[None, :]
    pad = jnp.zeros((t, LANES - ROPE_D), F32)
    cos = jnp.concatenate([jnp.cos(ang), jnp.cos(ang), pad], axis=-1)
    sin = jnp.concatenate([jnp.sin(ang), jnp.sin(ang), pad], axis=-1)
    return cos, sin


def _rotate_cols(w):
    half = ROPE_D // 2
    return jnp.concatenate([-w[..., half:], w[..., :half]], axis=-1)


def kernel(x, meta_tokens, ffn1_norm, ffn1_w_gate, ffn1_w_up, ffn1_w_down, mix_norm, w_in, diff_lambda_q1, diff_lambda_k1, diff_lambda_q2, diff_lambda_k2, diff_subln, mla_q_norm, mla_w_uq, mla_kv_norm, mla_w_ukv, w_gate, b_gate, w_branch_diff, w_branch_mla, w_out, ffn2_norm, ffn2_w_gate, ffn2_w_up, ffn2_w_down, final_norm):
    batch, seq, d = x.shape
    depth = ffn1_norm.shape[0]
    m = batch * seq
    dq_w = DIFF_HEADS * 2 * DIFF_HEAD_DIM
    dv_w = DIFF_HEADS * DIFF_V_DIM
    main_w = 2 * dq_w + dv_w + Q_LORA + KV_LORA
    cq_block = (2 * dq_w + dv_w) // Q_LORA
    ckv_block = (2 * dq_w + dv_w + Q_LORA) // KV_LORA
    q_heads = 4

    cos, sin = _rope_tables(N_META + seq)
    cos_m, sin_m, cos_r, sin_r = cos[:N_META], sin[:N_META], cos[N_META:], sin[N_META:]
    slopes = jnp.asarray([2.0 ** (-8.0 * (h + 1) / DIFF_HEADS) for h in range(DIFF_HEADS)],
                         F32).reshape(DIFF_HEADS, 1, 1)
    col_scale = jnp.concatenate([jnp.full((1, dq_w), DIFF_HEAD_DIM ** -0.5 * LOG2E, F32),
                                 jnp.ones((1, main_w - dq_w), F32)], axis=-1)
    row = lambda v: v.reshape(1, -1).astype(F32)

    xs = x.reshape(m, d)
    xm = meta_tokens.astype(x.dtype)

    for l in range(depth):
        lam_init = 0.8 - 0.6 * math.exp(-0.3 * l)
        wg1, wu1, wd1 = (w[l].astype(BF16) for w in (ffn1_w_gate, ffn1_w_up, ffn1_w_down))
        w_main = _cast_cols(w_in, l, main_w, tr=512, tc=768, name="cast_w_in")
        w_kr = _tail_cols(w_in, l, main_w, tr=512, name="w_in_rope_cols")
        zpad = jnp.zeros((d, LANES - ROPE_D), F32)
        w_kpe = jnp.concatenate([w_kr, zpad, _rotate_cols(w_kr), zpad], axis=-1).astype(BF16)

        wuq = mla_w_uq[l].reshape(Q_LORA, MLA_HEADS, NOPE_D + ROPE_D)
        w_q = jnp.concatenate([wuq, _rotate_cols(wuq[..., NOPE_D:])], axis=-1)
        w_q = w_q.reshape(Q_LORA, MLA_HEADS * MLA_SLOT).astype(BF16)
        w_ukv = mla_w_ukv[l].astype(BF16)

        xs = _ffn(xs, row(ffn1_norm[l]), wg1, wu1, wd1, row(final_norm),
                  tm=512, tf=256, final_norm=False, name="ffn1")
        xm = _ffn(xm, row(ffn1_norm[l]), wg1, wu1, wd1, row(final_norm),
                  tm=N_META, tf=256, final_norm=False, name="ffn1_meta")

        hs = _norm(xs, row(mix_norm[l]), tm=512, name="mix_norm")
        hm = _norm(xm, row(mix_norm[l]), tm=N_META, name="mix_norm_meta")

        ps, wg2, w_g = _mm(hs, w_main, tm=1024, tn=768, out_dtype=BF16, scale=col_scale, name="proj",
                           sides=(_side(ffn2_w_gate, l, (m // 1024) * (main_w // 768)),
                                  _side(w_gate, l, (m // 1024) * (main_w // 768))))
        pm = _mm(hm, w_main, tm=N_META, tn=768, out_dtype=BF16, scale=col_scale, name="proj_meta")
        gates, wu2 = _mm(hs, w_g, tm=1024, tn=1024, out_dtype=BF16, bias=row(b_gate[l]), name="gates",
                         sides=(_side(ffn2_w_up, l, (m // 1024) * (2 * d // 1024)),))
        kpe_s = _kpe(hs, w_kpe, cos_r, sin_r, tm=1024, name="kpe")
        kpe_m = _kpe(hm, w_kpe, cos_m, sin_m, tm=N_META, name="kpe_meta")

        q_mla = _mlaq(ps, cq_block, row(mla_q_norm[l]), w_q, cos_r, sin_r, tm=512, heads=q_heads,
                      scale=(NOPE_D + ROPE_D) ** -0.5 * LOG2E, name="mla_q")
        kv_s = _mlakv(ps, ckv_block, row(mla_kv_norm[l]), w_ukv, tm=1024, tn=1024, name="mla_kv")
        kv_m = _mlakv(pm, ckv_block, row(mla_kv_norm[l]), w_ukv, tm=N_META, tn=1024, name="mla_kv_meta")

        pad_rows = lambda a: jnp.pad(a, ((0, META_PAD - N_META), (0, 0)))
        diff_steps = batch * DIFF_HEADS * (seq // 256)
        o_diff, w_pa, w_pb = _diff_attn(
            ps, pad_rows(pm), slopes, row(diff_lambda_q1[l]), row(diff_lambda_k1[l]),
            row(diff_lambda_q2[l]), row(diff_lambda_k2[l]), row(diff_subln[l]),
            batch=batch, seq=seq, tq=256, lam_init=lam_init, name="diff_attn",
            sides=(_side(w_branch_diff, l, diff_steps), _side(w_branch_mla, l, diff_steps)))
        o_mla, w_o = _mla_attn(q_mla, kv_s, kpe_s, pad_rows(kv_m), pad_rows(kpe_m),
                               batch=batch, seq=seq, tq=2048, tk=512, name="mla_attn",
                               sides=(_side(w_out, l, batch * MLA_HEADS * (seq // 2048)),))

        merged, wd2 = _merge(o_diff, o_mla, w_pa, w_pb, gates,
                             _side(ffn2_w_down, l, (m // 1024) * (d // 512), bc=512),
                             tm=1024, tn=512, name="merge")
        xs = _mm(merged, w_o, tm=1024, tn=512, out_dtype=F32, res=xs, name="out_proj")

        last = l == depth - 1
        xs = _ffn(xs, row(ffn2_norm[l]), wg2, wu2, wd2, row(final_norm),
                  tm=512, tf=256, final_norm=last, name="ffn2")
        if not last:
            raise NotImplementedError("DEPTH > 1 needs the meta-row query path")

    return xs.reshape(batch, seq, d)
```

```python
import functools
import math
from typing import NamedTuple

import numpy as np
import jax
import jax.numpy as jnp
from jax import lax
from jax.experimental import pallas as pl
from jax.experimental.pallas import tpu as pltpu

N_META = 16
EPS = 1e-6
DIFF_HEADS = 8
DIFF_HEAD_DIM = 128
DIFF_V_DIM = 256
MLA_HEADS = 16
Q_LORA = 1024
KV_LORA = 512
NOPE_D = 128
ROPE_D = 64
MLA_V_D = 128
ROPE_THETA = 10000.0

LANES = 128
MLA_SLOT = 256
META_PAD = 128
VMEM_CAP = 60000 * 1024
VMEM_PHYSICAL = 64 * 2**20
VMEM_RESERVE = 2 * 2**20
VMEM_FLOOR = 16 * 2**20
NEG_BIG = -1e30
LOG2E = math.log2(math.e)

F32 = jnp.float32
BF16 = jnp.bfloat16


def _nbytes(shape, dtype):
    return int(np.prod(shape)) * jnp.dtype(dtype).itemsize


def _params(semantics, block_bytes, scratch_bytes, temp_bytes):
    need = 2 * block_bytes + scratch_bytes + temp_bytes
    return pltpu.CompilerParams(dimension_semantics=semantics,
                                vmem_limit_bytes=int(min(VMEM_CAP, max(need, VMEM_FLOOR))))


def _rms(x, gain):
    return x * lax.rsqrt(jnp.mean(x * x, axis=-1, keepdims=True) + EPS) * gain


def _ffn_kernel(x_ref, g_ref, wga_ref, wua_ref, wda_ref, wgb_ref, wub_ref, wdb_ref, gf_ref, out_ref,
                h_scr, *, final_norm, nchunks):
    f = pl.program_id(1)
    last = pl.num_programs(1) - 1

    @pl.when(f == 0)
    def _():
        x = x_ref[...]
        h_scr[...] = _rms(x, g_ref[...]).astype(BF16)
        out_ref[...] = x

    def act(wg_ref, wu_ref):
        h = h_scr[...]
        g = jnp.dot(h, wg_ref[...], preferred_element_type=F32)
        u = jnp.dot(h, wu_ref[...], preferred_element_type=F32)
        return (0.5 * (g * jax.nn.sigmoid(g)) * u).astype(BF16)

    def both():
        a = act(wga_ref, wua_ref)
        b = act(wgb_ref, wub_ref)
        out_ref[...] += (jnp.dot(a, wda_ref[...], preferred_element_type=F32)
                         + jnp.dot(b, wdb_ref[...], preferred_element_type=F32))

    if nchunks % 2:
        pl.when(f < last)(both)

        @pl.when(f == last)
        def _():
            out_ref[...] += jnp.dot(act(wga_ref, wua_ref), wda_ref[...], preferred_element_type=F32)
    else:
        both()

    if final_norm:
        @pl.when(f == last)
        def _():
            out_ref[...] = _rms(out_ref[...], gf_ref[...])


def _ffn(x, gain, wg, wu, wd, final_gain, *, tm, tf, final_norm, name):
    m, d = x.shape
    nchunks = wg.shape[1] // tf
    kern = functools.partial(_ffn_kernel, final_norm=final_norm, nchunks=nchunks)
    col_a = lambda i, f: (0, 2 * f)
    col_b = lambda i, f: (0, jnp.minimum(2 * f + 1, nchunks - 1))
    row_a = lambda i, f: (2 * f, 0)
    row_b = lambda i, f: (jnp.minimum(2 * f + 1, nchunks - 1), 0)
    weights = 2 * (2 * _nbytes((d, tf), BF16) + _nbytes((tf, d), BF16))
    need = (_nbytes((tm, d), F32) + 2 * _nbytes((tm, d), F32) + 2 * weights
            + _nbytes((tm, d), BF16) + 6 * _nbytes((tm, tf), F32) + _nbytes((tm, d), F32))
    return pl.pallas_call(
        kern,
        grid=(m // tm, (nchunks + 1) // 2),
        in_specs=[
            pl.BlockSpec((tm, d), lambda i, f: (i, 0), pipeline_mode=pl.Buffered(1)),
            pl.BlockSpec((1, d), lambda i, f: (0, 0)),
            pl.BlockSpec((d, tf), col_a), pl.BlockSpec((d, tf), col_a), pl.BlockSpec((tf, d), row_a),
            pl.BlockSpec((d, tf), col_b), pl.BlockSpec((d, tf), col_b), pl.BlockSpec((tf, d), row_b),
            pl.BlockSpec((1, d), lambda i, f: (0, 0)),
        ],
        out_specs=pl.BlockSpec((tm, d), lambda i, f: (i, 0)),
        out_shape=jax.ShapeDtypeStruct((m, d), F32),
        scratch_shapes=[pltpu.VMEM((tm, d), BF16)],
        compiler_params=pltpu.CompilerParams(
            dimension_semantics=("parallel", "arbitrary"),
            vmem_limit_bytes=int(min(VMEM_PHYSICAL - VMEM_RESERVE, max(need, VMEM_FLOOR)))),
        name=name,
    )(x, gain, wg, wu, wd, wg, wu, wd, final_gain)


def _ffn_meta_kernel(x_ref, g_ref, wg_ref, wu_ref, wd_ref, out_ref, wg_out, wu_out, wd_out, h_scr):
    @pl.when(pl.program_id(0) == 0)
    def _():
        x = x_ref[...]
        h_scr[...] = _rms(x, g_ref[...]).astype(BF16)
        out_ref[...] = x

    wg = wg_ref[...].astype(BF16)
    wu = wu_ref[...].astype(BF16)
    wd = wd_ref[...].astype(BF16)
    wg_out[...] = wg
    wu_out[...] = wu
    wd_out[...] = wd
    h = h_scr[...]
    g = jnp.dot(h, wg, preferred_element_type=F32)
    u = jnp.dot(h, wu, preferred_element_type=F32)
    a = (0.5 * (g * jax.nn.sigmoid(g)) * u).astype(BF16)
    out_ref[...] += jnp.dot(a, wd, preferred_element_type=F32)


def _ffn_meta(x, gain, wg, wu, wd, l, *, tf, name):
    m, d = x.shape
    ff = wg.shape[2]
    blocks = (2 * _nbytes((m, d), F32) + 3 * _nbytes((d, tf), F32) + 3 * _nbytes((d, tf), BF16))
    return pl.pallas_call(
        _ffn_meta_kernel,
        grid=(ff // tf,),
        in_specs=[pl.BlockSpec((m, d), lambda f: (0, 0)),
                  pl.BlockSpec((1, d), lambda f: (0, 0)),
                  pl.BlockSpec((None, d, tf), lambda f: (l, 0, f)),
                  pl.BlockSpec((None, d, tf), lambda f: (l, 0, f)),
                  pl.BlockSpec((None, tf, d), lambda f: (l, f, 0))],
        out_specs=[pl.BlockSpec((m, d), lambda f: (0, 0)),
                   pl.BlockSpec((d, tf), lambda f: (0, f)),
                   pl.BlockSpec((d, tf), lambda f: (0, f)),
                   pl.BlockSpec((tf, d), lambda f: (f, 0))],
        out_shape=[jax.ShapeDtypeStruct((m, d), F32),
                   jax.ShapeDtypeStruct((d, ff), BF16),
                   jax.ShapeDtypeStruct((d, ff), BF16),
                   jax.ShapeDtypeStruct((ff, d), BF16)],
        scratch_shapes=[pltpu.VMEM((m, d), BF16)],
        compiler_params=_params(("arbitrary",), blocks, _nbytes((m, d), BF16),
                                3 * _nbytes((d, tf), BF16)),
        name=name,
    )(x, gain, wg, wu, wd)


def _norm_kernel(x_ref, g_ref, out_ref):
    out_ref[...] = _rms(x_ref[...], g_ref[...]).astype(out_ref.dtype)


def _norm(x, gain, *, tm, name):
    m, d = x.shape
    blocks = _nbytes((tm, d), F32) + _nbytes((tm, d), BF16)
    return pl.pallas_call(
        _norm_kernel,
        grid=(m // tm,),
        in_specs=[pl.BlockSpec((tm, d), lambda i: (i, 0)),
                  pl.BlockSpec((1, d), lambda i: (0, 0))],
        out_specs=pl.BlockSpec((tm, d), lambda i: (i, 0)),
        out_shape=jax.ShapeDtypeStruct((m, d), BF16),
        compiler_params=_params(("parallel",), blocks, 0, 2 * _nbytes((tm, d), F32)),
        name=name,
    )(x, gain)


def _cast_kernel(w_ref, out_ref):
    out_ref[...] = w_ref[...].astype(out_ref.dtype)


def _cast_cols(w, l, cols, *, tr, tc, name):
    rows = w.shape[1]
    blocks = _nbytes((tr, tc), w.dtype) + _nbytes((tr, tc), BF16)
    return pl.pallas_call(
        _cast_kernel,
        grid=(rows // tr, cols // tc),
        in_specs=[pl.BlockSpec((None, tr, tc), lambda i, j: (l, i, j))],
        out_specs=pl.BlockSpec((tr, tc), lambda i, j: (i, j)),
        out_shape=jax.ShapeDtypeStruct((rows, cols), BF16),
        compiler_params=_params(("parallel", "parallel"), blocks, 0, 0),
        name=name,
    )(w)


def _tail_cols(w, l, start, *, tr, name):
    rows, cols = w.shape[1:]
    assert start % LANES == 0 and cols - start <= LANES
    blocks = 2 * _nbytes((tr, LANES), F32)
    out = pl.pallas_call(
        _cast_kernel,
        grid=(rows // tr,),
        in_specs=[pl.BlockSpec((None, tr, LANES), lambda i: (l, i, start // LANES))],
        out_specs=pl.BlockSpec((tr, LANES), lambda i: (i, 0)),
        out_shape=jax.ShapeDtypeStruct((rows, LANES), F32),
        compiler_params=_params(("parallel",), blocks, 0, 0),
        name=name,
    )(w)
    return out[:, :cols - start]


class SideCast(NamedTuple):
    w: jax.Array
    l: int
    br: int
    bc: int


def _side(w, l, steps, bc=None):
    rows, cols = w.shape[1:]
    bc = cols if bc is None else bc
    sublanes = 16
    br = next(r for r in range(sublanes, rows + 1, sublanes)
              if rows % r == 0 and (rows // r) * (cols // bc) <= steps)
    return SideCast(w, l, br, bc)


def _side_cast_specs(side, grid):
    rows, cols = side.w.shape[1:]
    ncols = cols // side.bc
    nblocks = (rows // side.br) * ncols
    assert rows % side.br == 0 and cols % side.bc == 0 and nblocks <= math.prod(grid), (side.w.shape, grid)

    def block(*idx):
        step = 0
        for i, n in zip(idx, grid):
            step = step * n + i
        t = jnp.minimum(step, nblocks - 1)
        return t // ncols, t % ncols

    in_spec = pl.BlockSpec((None, side.br, side.bc), lambda *idx: (side.l, *block(*idx)))
    out_spec = pl.BlockSpec((side.br, side.bc), block)
    nbytes = _nbytes((side.br, side.bc), F32) + _nbytes((side.br, side.bc), BF16)
    return in_spec, out_spec, jax.ShapeDtypeStruct((rows, cols), BF16), nbytes


def _side_cast_run(side_in_refs, side_out_refs):
    for src, dst in zip(side_in_refs, side_out_refs):
        dst[...] = src[...].astype(dst.dtype)


def _mm_kernel(*refs, has_scale, has_bias, has_res, n_side):
    n_in = 2 + has_scale + has_bias + has_res + n_side
    a_ref, w_ref = refs[0], refs[1]
    extra = list(refs[2:n_in - n_side])
    out_ref = refs[n_in]
    y = jnp.dot(a_ref[...], w_ref[...], preferred_element_type=F32)
    if has_scale:
        y = y * extra.pop(0)[...]
    if has_bias:
        y = jax.nn.sigmoid(y + extra.pop(0)[...])
    if has_res:
        y = y + extra.pop(0)[...]
    out_ref[...] = y.astype(out_ref.dtype)
    _side_cast_run(refs[n_in - n_side:n_in], refs[n_in + 1:])


def _mm(a, w, *, tm, tn, out_dtype, name, scale=None, bias=None, res=None, sides=()):
    m, k = a.shape
    n = w.shape[1]
    grid = (m // tm, n // tn)
    ins = [a, w]
    in_specs = [pl.BlockSpec((tm, k), lambda i, j: (i, 0)),
                pl.BlockSpec((k, tn), lambda i, j: (0, j))]
    blocks = _nbytes((tm, k), a.dtype) + _nbytes((k, tn), w.dtype) + _nbytes((tm, tn), out_dtype)
    for vec in (scale, bias):
        if vec is not None:
            ins.append(vec)
            in_specs.append(pl.BlockSpec((1, tn), lambda i, j: (0, j)))
    if res is not None:
        ins.append(res)
        in_specs.append(pl.BlockSpec((tm, tn), lambda i, j: (i, j)))
        blocks += _nbytes((tm, tn), res.dtype)
    out_specs = [pl.BlockSpec((tm, tn), lambda i, j: (i, j))]
    out_shape = [jax.ShapeDtypeStruct((m, n), out_dtype)]
    for side in sides:
        in_spec, out_spec, shape, nbytes = _side_cast_specs(side, grid)
        ins.append(side.w)
        in_specs.append(in_spec)
        out_specs.append(out_spec)
        out_shape.append(shape)
        blocks += nbytes
    kern = functools.partial(_mm_kernel, has_scale=scale is not None, has_bias=bias is not None,
                             has_res=res is not None, n_side=len(sides))
    outs = pl.pallas_call(
        kern,
        grid=grid,
        in_specs=in_specs,
        out_specs=out_specs,
        out_shape=out_shape,
        compiler_params=_params(("parallel", "arbitrary"), blocks, 0, 4 * _nbytes((tm, tn), F32)),
        name=name,
    )(*ins)
    return outs if sides else outs[0]


def _kpe_kernel(h_ref, w_ref, cos_ref, sin_ref, out_ref):
    y = jnp.dot(h_ref[...], w_ref[...], preferred_element_type=F32)
    out_ref[...] = (y[:, :LANES] * cos_ref[...] + y[:, LANES:] * sin_ref[...]).astype(out_ref.dtype)


def _kpe(h, w, cos, sin, *, tm, name):
    m, d = h.shape
    nt = cos.shape[0] // tm
    blocks = _nbytes((tm, d), BF16) + _nbytes((d, 2 * LANES), BF16) + 3 * _nbytes((tm, LANES), F32)
    return pl.pallas_call(
        _kpe_kernel,
        grid=(m // tm,),
        in_specs=[pl.BlockSpec((tm, d), lambda i: (i, 0)),
                  pl.BlockSpec((d, 2 * LANES), lambda i: (0, 0)),
                  pl.BlockSpec((tm, LANES), lambda i: (i % nt, 0)),
                  pl.BlockSpec((tm, LANES), lambda i: (i % nt, 0))],
        out_specs=pl.BlockSpec((tm, LANES), lambda i: (i, 0)),
        out_shape=jax.ShapeDtypeStruct((m, LANES), BF16),
        compiler_params=_params(("parallel",), blocks, 0, _nbytes((tm, 2 * LANES), F32)),
        name=name,
    )(h, w, cos, sin)


def _mlaq_kernel(c_ref, g_ref, w_ref, cos_ref, sin_ref, out_ref, cn_scr, *, heads, scale):
    @pl.when(pl.program_id(1) == 0)
    def _():
        cn_scr[...] = _rms(c_ref[...].astype(F32), g_ref[...]).astype(BF16)

    y = jnp.dot(cn_scr[...], w_ref[...], preferred_element_type=F32)
    cos = cos_ref[...]
    sin = sin_ref[...]
    for h in range(heads):
        s0 = h * MLA_SLOT
        out_ref[:, s0:s0 + NOPE_D] = (y[:, s0:s0 + NOPE_D] * scale).astype(out_ref.dtype)
        hi = y[:, s0 + NOPE_D:s0 + MLA_SLOT]
        pe = hi * cos + pltpu.roll(hi, LANES // 2, axis=1) * sin
        out_ref[:, s0 + NOPE_D:s0 + MLA_SLOT] = (pe * scale).astype(out_ref.dtype)


def _mlaq(p, col_block, gain, w, cos, sin, *, tm, heads, scale, name):
    m = p.shape[0]
    k = gain.shape[1]
    wn = heads * MLA_SLOT
    groups = w.shape[1] // wn
    nt = cos.shape[0] // tm
    kern = functools.partial(_mlaq_kernel, heads=heads, scale=scale)
    blocks = (_nbytes((tm, k), BF16) + _nbytes((k, wn), BF16) + 2 * _nbytes((tm, LANES), F32)
              + _nbytes((tm, heads * MLA_SLOT), BF16))
    return pl.pallas_call(
        kern,
        grid=(m // tm, groups),
        in_specs=[pl.BlockSpec((tm, k), lambda i, j: (i, col_block)),
                  pl.BlockSpec((1, k), lambda i, j: (0, 0)),
                  pl.BlockSpec((k, wn), lambda i, j: (0, j)),
                  pl.BlockSpec((tm, LANES), lambda i, j: (i % nt, 0)),
                  pl.BlockSpec((tm, LANES), lambda i, j: (i % nt, 0))],
        out_specs=pl.BlockSpec((tm, heads * MLA_SLOT), lambda i, j: (i, j)),
        out_shape=jax.ShapeDtypeStruct((m, groups * heads * MLA_SLOT), BF16),
        scratch_shapes=[pltpu.VMEM((tm, k), BF16)],
        compiler_params=_params(("parallel", "arbitrary"), blocks, _nbytes((tm, k), BF16),
                                2 * _nbytes((tm, wn), F32)),
        name=name,
    )(p, gain, w, cos, sin)


def _mlakv_kernel(c_ref, g_ref, w_ref, out_ref, cn_scr):
    @pl.when(pl.program_id(1) == 0)
    def _():
        cn_scr[...] = _rms(c_ref[...].astype(F32), g_ref[...]).astype(BF16)

    out_ref[...] = jnp.dot(cn_scr[...], w_ref[...], preferred_element_type=F32).astype(out_ref.dtype)


def _mlakv(p, col_block, gain, w, *, tm, tn, name):
    m = p.shape[0]
    k, n = w.shape
    blocks = _nbytes((tm, k), BF16) + _nbytes((k, tn), BF16) + _nbytes((tm, tn), BF16)
    return pl.pallas_call(
        _mlakv_kernel,
        grid=(m // tm, n // tn),
        in_specs=[pl.BlockSpec((tm, k), lambda i, j: (i, col_block)),
                  pl.BlockSpec((1, k), lambda i, j: (0, 0)),
                  pl.BlockSpec((k, tn), lambda i, j: (0, j))],
        out_specs=pl.BlockSpec((tm, tn), lambda i, j: (i, j)),
        out_shape=jax.ShapeDtypeStruct((m, n), BF16),
        scratch_shapes=[pltpu.VMEM((tm, k), BF16)],
        compiler_params=_params(("parallel", "arbitrary"), blocks, _nbytes((tm, k), BF16),
                                2 * _nbytes((tm, tn), F32)),
        name=name,
    )(p, gain, w)


def _dot_nt(a, b):
    return lax.dot_general(a, b, (((1,), (1,)), ((), ())), preferred_element_type=F32)


def _transpose_bf16(x):
    return x.astype(F32).T.astype(BF16)


def _softmax_first_t(s_t, v_t):
    m = jnp.max(s_t, axis=0, keepdims=True)
    p = jnp.exp2(s_t - m)
    return m, jnp.sum(p, axis=0, keepdims=True), jnp.dot(v_t, p.astype(BF16), preferred_element_type=F32)


def _softmax_next_t(s_t, shift, v_t, m, l, acc):
    m_new = jnp.maximum(m, jnp.max(s_t, axis=0, keepdims=True) + shift)
    alpha = jnp.exp2(m - m_new)
    p = jnp.exp2(s_t - (m_new - shift))
    l = alpha * l + jnp.sum(p, axis=0, keepdims=True)
    acc = alpha * acc + jnp.dot(v_t, p.astype(BF16), preferred_element_type=F32)
    return m_new, l, acc


def _meta_mask_t(s_t):
    row = lax.broadcasted_iota(jnp.int32, s_t.shape, 0)
    return jnp.where(row < N_META, s_t, NEG_BIG)


def _diff_attn_kernel(q_ref, k_ref, v_ref, km_ref, vm_ref, slope_ref, lq1_ref, lk1_ref, lq2_ref,
                      lk2_ref, subln_ref, out_ref, vt_scr, bias_scr, *, lam_init):
    tq = q_ref.shape[0]
    tk = tq
    nk = k_ref.shape[0] // tk
    hd = DIFF_HEAD_DIM
    qi = pl.program_id(2)
    slope2 = slope_ref[...] * LOG2E

    @pl.when(qi == 0)
    def _():
        vt_scr[:, :META_PAD] = _transpose_bf16(vm_ref[...])
        for c in range(nk):
            vt_scr[:, META_PAD + c * tk:META_PAD + (c + 1) * tk] = _transpose_bf16(
                v_ref[c * tk:(c + 1) * tk, :])
        d = (lax.broadcasted_iota(jnp.int32, (tk, tq), 1)
             - lax.broadcasted_iota(jnp.int32, (tk, tq), 0)).astype(F32) * slope2
        bias_scr[0] = -d
        bias_scr[1] = d
        bias_scr[2] = -jnp.abs(d)

    q1 = q_ref[:, :hd]
    q2 = q_ref[:, hd:]

    def qk(j):
        kb = k_ref[j * tk:(j + 1) * tk, :]
        return _dot_nt(kb[:, :hd], q1), _dot_nt(kb[:, hd:], q2)

    vtm = vt_scr[:, :META_PAD]
    st1 = _softmax_first_t(_meta_mask_t(_dot_nt(km_ref[:, :hd], q1)), vtm)
    st2 = _softmax_first_t(_meta_mask_t(_dot_nt(km_ref[:, hd:], q2)), vtm)

    s1, s2 = qk(0)
    for j in range(nk):
        nxt = qk(j + 1) if j + 1 < nk else None
        bias = bias_scr[jnp.where(j < qi, 0, jnp.where(j == qi, 2, 1))]
        shift = -jnp.abs(qi * tq - j * tk).astype(F32) * slope2
        vt = vt_scr[:, META_PAD + j * tk:META_PAD + (j + 1) * tk]
        st1 = _softmax_next_t(s1 + bias, shift, vt, *st1)
        st2 = _softmax_next_t(s2 + bias, shift, vt, *st2)
        if nxt is not None:
            s1, s2 = nxt

    lam = (jnp.exp(jnp.sum(lq1_ref[...] * lk1_ref[...], axis=-1, keepdims=True))
           - jnp.exp(jnp.sum(lq2_ref[...] * lk2_ref[...], axis=-1, keepdims=True)) + lam_init)
    o_t = st1[2] / st1[1] - lam * (st2[2] / st2[1])
    out_ref[...] = (_rms(o_t.T, subln_ref[...]) * (1.0 - lam_init)).astype(out_ref.dtype)


def _diff_attn(p, pm, slopes, lq1, lk1, lq2, lk2, subln, *, batch, seq, tq, lam_init, name):
    m = p.shape[0]
    hw = 2 * DIFF_HEAD_DIM
    nq = seq // tq
    koff = DIFF_HEADS
    voff = 2 * DIFF_HEADS
    kern = functools.partial(_diff_attn_kernel, lam_init=lam_init)
    vec = pl.BlockSpec((1, DIFF_HEAD_DIM), lambda b, h, i: (0, 0))
    blocks = (2 * _nbytes((tq, hw), BF16) + 2 * _nbytes((seq, hw), BF16)
              + 2 * _nbytes((META_PAD, hw), BF16))
    scratch = _nbytes((hw, seq + META_PAD), BF16) + 3 * _nbytes((tq, tq), F32)
    return pl.pallas_call(
        kern,
        grid=(batch, DIFF_HEADS, nq),
        in_specs=[pl.BlockSpec((tq, hw), lambda b, h, i: (b * nq + i, h)),
                  pl.BlockSpec((seq, hw), lambda b, h, i: (b, koff + h)),
                  pl.BlockSpec((seq, hw), lambda b, h, i: (b, voff + h)),
                  pl.BlockSpec((META_PAD, hw), lambda b, h, i: (0, koff + h)),
                  pl.BlockSpec((META_PAD, hw), lambda b, h, i: (0, voff + h)),
                  pl.BlockSpec((None, 1, 1), lambda b, h, i: (h, 0, 0)),
                  vec, vec, vec, vec,
                  pl.BlockSpec((1, hw), lambda b, h, i: (0, 0))],
        out_specs=pl.BlockSpec((tq, hw), lambda b, h, i: (b * nq + i, h)),
        out_shape=jax.ShapeDtypeStruct((m, DIFF_HEADS * hw), BF16),
        scratch_shapes=[pltpu.VMEM((hw, seq + META_PAD), BF16),
                        pltpu.VMEM((3, tq, tq), F32)],
        compiler_params=_params(("parallel", "parallel", "arbitrary"), blocks, scratch,
                                12 * _nbytes((tq, tq), F32)),
        name=name,
    )(p, p, p, pm, pm, slopes, lq1, lk1, lq2, lk2, subln)


def _mla_attn_kernel(q_ref, kv_ref, kpe_ref, kvm_ref, kpem_ref, out_ref, k_scr, vt_scr, *, tk):
    nk = kv_ref.shape[0] // tk

    @pl.when(pl.program_id(2) == 0)
    def _():
        k_scr[:META_PAD, :NOPE_D] = kvm_ref[:, :NOPE_D]
        k_scr[:META_PAD, NOPE_D:] = kpem_ref[...]
        k_scr[META_PAD:, :NOPE_D] = kv_ref[:, :NOPE_D]
        k_scr[META_PAD:, NOPE_D:] = kpe_ref[...]
        vt_scr[:, :META_PAD] = _transpose_bf16(kvm_ref[:, NOPE_D:])
        for c in range(nk):
            vt_scr[:, META_PAD + c * tk:META_PAD + (c + 1) * tk] = _transpose_bf16(
                kv_ref[c * tk:(c + 1) * tk, NOPE_D:])

    q = q_ref[...]

    def qk(j):
        return _dot_nt(k_scr[META_PAD + j * tk:META_PAD + (j + 1) * tk, :], q)

    st = _softmax_first_t(_meta_mask_t(_dot_nt(k_scr[:META_PAD, :], q)), vt_scr[:, :META_PAD])

    s = qk(0)
    for j in range(nk):
        nxt = qk(j + 1) if j + 1 < nk else None
        st = _softmax_next_t(s, 0.0, vt_scr[:, META_PAD + j * tk:META_PAD + (j + 1) * tk], *st)
        s = nxt

    out_ref[...] = (st[2] / st[1]).T.astype(out_ref.dtype)


def _mla_attn(q, kv, kpe, kvm, kpem, *, batch, seq, tq, tk, name):
    mrows = q.shape[0]
    nq = seq // tq
    kern = functools.partial(_mla_attn_kernel, tk=tk)
    blocks = (_nbytes((tq, MLA_SLOT), BF16) + _nbytes((seq, MLA_SLOT), BF16)
              + _nbytes((seq, LANES), BF16) + _nbytes((META_PAD, MLA_SLOT + LANES), BF16)
              + _nbytes((tq, MLA_V_D), BF16))
    scratch = _nbytes((seq + META_PAD, MLA_SLOT), BF16) + _nbytes((MLA_V_D, seq + META_PAD), BF16)
    return pl.pallas_call(
        kern,
        grid=(batch, MLA_HEADS, nq),
        in_specs=[pl.BlockSpec((tq, MLA_SLOT), lambda b, h, i: (b * nq + i, h)),
                  pl.BlockSpec((seq, MLA_SLOT), lambda b, h, i: (b, h)),
                  pl.BlockSpec((seq, LANES), lambda b, h, i: (b, 0)),
                  pl.BlockSpec((META_PAD, MLA_SLOT), lambda b, h, i: (0, h)),
                  pl.BlockSpec((META_PAD, LANES), lambda b, h, i: (0, 0))],
        out_specs=pl.BlockSpec((tq, MLA_V_D), lambda b, h, i: (b * nq + i, h)),
        out_shape=jax.ShapeDtypeStruct((mrows, MLA_HEADS * MLA_V_D), BF16),
        scratch_shapes=[pltpu.VMEM((seq + META_PAD, MLA_SLOT), BF16),
                        pltpu.VMEM((MLA_V_D, seq + META_PAD), BF16)],
        compiler_params=_params(("parallel", "parallel", "arbitrary"), blocks, scratch,
                                6 * _nbytes((tk, tq), F32)),
        name=name,
    )(q, kv, kpe, kvm, kpem)


def _merge_kernel(od_ref, om_ref, wa_ref, wb_ref, gd_ref, gm_ref, side_ref, out_ref, side_out_ref):
    ya = jnp.dot(od_ref[...], wa_ref[...], preferred_element_type=F32)
    yb = jnp.dot(om_ref[...], wb_ref[...], preferred_element_type=F32)
    out_ref[...] = (gd_ref[...].astype(F32) * ya + gm_ref[...].astype(F32) * yb).astype(out_ref.dtype)
    _side_cast_run([side_ref], [side_out_ref])


def _merge(od, om, wa, wb, gates, side, *, tm, tn, name):
    m, k = od.shape
    n = wa.shape[1]
    nj = n // tn
    grid = (m // tm, nj)
    side_in, side_out, side_shape, side_bytes = _side_cast_specs(side, grid)
    blocks = (2 * _nbytes((tm, k), BF16) + 2 * _nbytes((k, tn), BF16) + 3 * _nbytes((tm, tn), BF16)
              + side_bytes)
    return pl.pallas_call(
        _merge_kernel,
        grid=grid,
        in_specs=[pl.BlockSpec((tm, k), lambda i, j: (i, 0)),
                  pl.BlockSpec((tm, k), lambda i, j: (i, 0)),
                  pl.BlockSpec((k, tn), lambda i, j: (0, j)),
                  pl.BlockSpec((k, tn), lambda i, j: (0, j)),
                  pl.BlockSpec((tm, tn), lambda i, j: (i, j)),
                  pl.BlockSpec((tm, tn), lambda i, j: (i, nj + j)),
                  side_in],
        out_specs=[pl.BlockSpec((tm, tn), lambda i, j: (i, j)), side_out],
        out_shape=[jax.ShapeDtypeStruct((m, n), BF16), side_shape],
        compiler_params=_params(("parallel", "arbitrary"), blocks, 0, 4 * _nbytes((tm, tn), F32)),
        name=name,
    )(od, om, wa, wb, gates, gates, side.w)


def _rope_tables(t):
    inv_freq = 1.0 / (ROPE_THETA ** (jnp.arange(0, ROPE_D, 2, dtype=F32) / ROPE_D))
    ang = jnp.arange(t).astype(F32)[:, None] * inv_freq[None, :]
    pad = jnp.zeros((t, LANES - ROPE_D), F32)
    cos = jnp.concatenate([jnp.cos(ang), jnp.cos(ang), pad], axis=-1)
    sin = jnp.concatenate([jnp.sin(ang), jnp.sin(ang), pad], axis=-1)
    return cos, sin


def _rotate_cols(w):
    half = ROPE_D // 2
    return jnp.concatenate([-w[..., half:], w[..., :half]], axis=-1)


def kernel(x, meta_tokens, ffn1_norm, ffn1_w_gate, ffn1_w_up, ffn1_w_down, mix_norm, w_in, diff_lambda_q1, diff_lambda_k1, diff_lambda_q2, diff_lambda_k2, diff_subln, mla_q_norm, mla_w_uq, mla_kv_norm, mla_w_ukv, w_gate, b_gate, w_branch_diff, w_branch_mla, w_out, ffn2_norm, ffn2_w_gate, ffn2_w_up, ffn2_w_down, final_norm):
    batch, seq, d = x.shape
    depth = ffn1_norm.shape[0]
    m = batch * seq
    dq_w = DIFF_HEADS * 2 * DIFF_HEAD_DIM
    dv_w = DIFF_HEADS * DIFF_V_DIM
    main_w = 2 * dq_w + dv_w + Q_LORA + KV_LORA
    cq_block = (2 * dq_w + dv_w) // Q_LORA
    ckv_block = (2 * dq_w + dv_w + Q_LORA) // KV_LORA
    q_heads = 4

    cos, sin = _rope_tables(N_META + seq)
    cos_m, sin_m, cos_r, sin_r = cos[:N_META], sin[:N_META], cos[N_META:], sin[N_META:]
    slopes = jnp.asarray([2.0 ** (-8.0 * (h + 1) / DIFF_HEADS) for h in range(DIFF_HEADS)],
                         F32).reshape(DIFF_HEADS, 1, 1)
    col_scale = jnp.concatenate([jnp.full((1, dq_w), DIFF_HEAD_DIM ** -0.5 * LOG2E, F32),
                                 jnp.ones((1, main_w - dq_w), F32)], axis=-1)
    row = lambda v: v.reshape(1, -1).astype(F32)

    xs = x.reshape(m, d)
    xm = meta_tokens.astype(x.dtype)

    for l in range(depth):
        lam_init = 0.8 - 0.6 * math.exp(-0.3 * l)
        w_main = _cast_cols(w_in, l, main_w, tr=512, tc=768, name="cast_w_in")
        w_kr = _tail_cols(w_in, l, main_w, tr=512, name="w_in_rope_cols")
        zpad = jnp.zeros((d, LANES - ROPE_D), F32)
        w_kpe = jnp.concatenate([w_kr, zpad, _rotate_cols(w_kr), zpad], axis=-1).astype(BF16)

        wuq = mla_w_uq[l].reshape(Q_LORA, MLA_HEADS, NOPE_D + ROPE_D)
        w_q = jnp.concatenate([wuq, _rotate_cols(wuq[..., NOPE_D:])], axis=-1)
        w_q = w_q.reshape(Q_LORA, MLA_HEADS * MLA_SLOT).astype(BF16)
        w_ukv = mla_w_ukv[l].astype(BF16)
        w_pa = w_branch_diff[l].astype(BF16)
        w_pb = w_branch_mla[l].astype(BF16)
        w_o = w_out[l].astype(BF16)

        xm, wg1, wu1, wd1 = _ffn_meta(xm, row(ffn1_norm[l]), ffn1_w_gate, ffn1_w_up, ffn1_w_down, l,
                                      tf=256, name="ffn1_meta")
        xs = _ffn(xs, row(ffn1_norm[l]), wg1, wu1, wd1, row(final_norm),
                  tm=512, tf=256, final_norm=False, name="ffn1")

        hs = _norm(xs, row(mix_norm[l]), tm=512, name="mix_norm")
        hm = _norm(xm, row(mix_norm[l]), tm=N_META, name="mix_norm_meta")

        ps, wg2, w_g = _mm(hs, w_main, tm=1024, tn=768, out_dtype=BF16, scale=col_scale, name="proj",
                           sides=(_side(ffn2_w_gate, l, (m // 1024) * (main_w // 768)),
                                  _side(w_gate, l, (m // 1024) * (main_w // 768))))
        pm = _mm(hm, w_main, tm=N_META, tn=768, out_dtype=BF16, scale=col_scale, name="proj_meta")
        gates, wu2 = _mm(hs, w_g, tm=1024, tn=1024, out_dtype=BF16, bias=row(b_gate[l]), name="gates",
                         sides=(_side(ffn2_w_up, l, (m // 1024) * (2 * d // 1024)),))
        kpe_s = _kpe(hs, w_kpe, cos_r, sin_r, tm=1024, name="kpe")
        kpe_m = _kpe(hm, w_kpe, cos_m, sin_m, tm=N_META, name="kpe_meta")

        q_mla = _mlaq(ps, cq_block, row(mla_q_norm[l]), w_q, cos_r, sin_r, tm=512, heads=q_heads,
                      scale=(NOPE_D + ROPE_D) ** -0.5 * LOG2E, name="mla_q")
        kv_s = _mlakv(ps, ckv_block, row(mla_kv_norm[l]), w_ukv, tm=1024, tn=1024, name="mla_kv")
        kv_m = _mlakv(pm, ckv_block, row(mla_kv_norm[l]), w_ukv, tm=N_META, tn=1024, name="mla_kv_meta")

        pad_rows = lambda a: jnp.pad(a, ((0, META_PAD - N_META), (0, 0)))
        o_diff = _diff_attn(ps, pad_rows(pm), slopes, row(diff_lambda_q1[l]), row(diff_lambda_k1[l]),
                            row(diff_lambda_q2[l]), row(diff_lambda_k2[l]), row(diff_subln[l]),
                            batch=batch, seq=seq, tq=256, lam_init=lam_init, name="diff_attn")
        o_mla = _mla_attn(q_mla, kv_s, kpe_s, pad_rows(kv_m), pad_rows(kpe_m),
                          batch=batch, seq=seq, tq=2048, tk=512, name="mla_attn")

        merged, wd2 = _merge(o_diff, o_mla, w_pa, w_pb, gates,
                             _side(ffn2_w_down, l, (m // 1024) * (d // 512), bc=512),
                             tm=1024, tn=512, name="merge")
        xs = _mm(merged, w_o, tm=1024, tn=512, out_dtype=F32, res=xs, name="out_proj")

        last = l == depth - 1
        xs = _ffn(xs, row(ffn2_norm[l]), wg2, wu2, wd2, row(final_norm),
                  tm=512, tf=256, final_norm=last, name="ffn2")
        if not last:
            raise NotImplementedError("DEPTH > 1 needs the meta-row query path")

    return xs.reshape(batch, seq, d)
```

```python
import functools
import math
from typing import NamedTuple

import numpy as np
import jax
import jax.numpy as jnp
from jax import lax
from jax.experimental import pallas as pl
from jax.experimental.pallas import tpu as pltpu

N_META = 16
EPS = 1e-6
DIFF_HEADS = 8
DIFF_HEAD_DIM = 128
DIFF_V_DIM = 256
MLA_HEADS = 16
Q_LORA = 1024
KV_LORA = 512
NOPE_D = 128
ROPE_D = 64
MLA_V_D = 128
ROPE_THETA = 10000.0

LANES = 128
MLA_SLOT = 256
META_PAD = 128
VMEM_CAP = 60000 * 1024
VMEM_PHYSICAL = 64 * 2**20
VMEM_RESERVE = 2 * 2**20
VMEM_FLOOR = 16 * 2**20
NEG_BIG = -1e30
LOG2E = math.log2(math.e)

F32 = jnp.float32
BF16 = jnp.bfloat16


def _nbytes(shape, dtype):
    return int(np.prod(shape)) * jnp.dtype(dtype).itemsize


def _params(semantics, block_bytes, scratch_bytes, temp_bytes):
    need = 2 * block_bytes + scratch_bytes + temp_bytes
    return pltpu.CompilerParams(dimension_semantics=semantics,
                                vmem_limit_bytes=int(min(VMEM_CAP, max(need, VMEM_FLOOR))))


def _rms(x, gain):
    return x * lax.rsqrt(jnp.mean(x * x, axis=-1, keepdims=True) + EPS) * gain


def _ffn_kernel(x_ref, g_ref, wga_ref, wua_ref, wda_ref, wgb_ref, wub_ref, wdb_ref, gf_ref, out_ref,
                h_scr, *, final_norm, nchunks):
    f = pl.program_id(1)
    last = pl.num_programs(1) - 1

    def act(wg_ref, wu_ref):
        h = h_scr[...]
        g = jnp.dot(h, wg_ref[...], preferred_element_type=F32)
        u = jnp.dot(h, wu_ref[...], preferred_element_type=F32)
        return (0.5 * (g * jax.nn.sigmoid(g)) * u).astype(BF16)

    def both():
        a = act(wga_ref, wua_ref)
        b = act(wgb_ref, wub_ref)
        return (jnp.dot(a, wda_ref[...], preferred_element_type=F32)
                + jnp.dot(b, wdb_ref[...], preferred_element_type=F32))

    @pl.when(f == 0)
    def _():
        x = x_ref[...]
        h_scr[...] = _rms(x, g_ref[...]).astype(BF16)
        out_ref[...] = x + both()

    @pl.when((f > 0) & (f < last) if nchunks % 2 else f > 0)
    def _():
        out_ref[...] += both()

    if nchunks % 2:
        @pl.when((f == last) & (f > 0))
        def _():
            out_ref[...] += jnp.dot(act(wga_ref, wua_ref), wda_ref[...], preferred_element_type=F32)

    if final_norm:
        @pl.when(f == last)
        def _():
            out_ref[...] = _rms(out_ref[...], gf_ref[...])


def _ffn(x, gain, wg, wu, wd, final_gain, *, tm, tf, final_norm, name):
    m, d = x.shape
    nchunks = wg.shape[1] // tf
    assert nchunks != 1, "the first grid step always handles two chunks"
    kern = functools.partial(_ffn_kernel, final_norm=final_norm, nchunks=nchunks)
    col_a = lambda i, f: (0, 2 * f)
    col_b = lambda i, f: (0, jnp.minimum(2 * f + 1, nchunks - 1))
    row_a = lambda i, f: (2 * f, 0)
    row_b = lambda i, f: (jnp.minimum(2 * f + 1, nchunks - 1), 0)
    weights = 2 * (2 * _nbytes((d, tf), BF16) + _nbytes((tf, d), BF16))
    need = (_nbytes((tm, d), F32) + 2 * _nbytes((tm, d), F32) + 2 * weights
            + _nbytes((tm, d), BF16) + 6 * _nbytes((tm, tf), F32) + _nbytes((tm, d), F32))
    return pl.pallas_call(
        kern,
        grid=(m // tm, (nchunks + 1) // 2),
        in_specs=[
            pl.BlockSpec((tm, d), lambda i, f: (i, 0), pipeline_mode=pl.Buffered(1)),
            pl.BlockSpec((1, d), lambda i, f: (0, 0)),
            pl.BlockSpec((d, tf), col_a), pl.BlockSpec((d, tf), col_a), pl.BlockSpec((tf, d), row_a),
            pl.BlockSpec((d, tf), col_b), pl.BlockSpec((d, tf), col_b), pl.BlockSpec((tf, d), row_b),
            pl.BlockSpec((1, d), lambda i, f: (0, 0)),
        ],
        out_specs=pl.BlockSpec((tm, d), lambda i, f: (i, 0)),
        out_shape=jax.ShapeDtypeStruct((m, d), F32),
        scratch_shapes=[pltpu.VMEM((tm, d), BF16)],
        compiler_params=pltpu.CompilerParams(
            dimension_semantics=("parallel", "arbitrary"),
            vmem_limit_bytes=int(min(VMEM_PHYSICAL - VMEM_RESERVE, max(need, VMEM_FLOOR)))),
        name=name,
    )(x, gain, wg, wu, wd, wg, wu, wd, final_gain)


def _ffn_meta_kernel(x_ref, g_ref, wg_ref, wu_ref, wd_ref, out_ref, wg_out, wu_out, wd_out, h_scr):
    @pl.when(pl.program_id(0) == 0)
    def _():
        x = x_ref[...]
        h_scr[...] = _rms(x, g_ref[...]).astype(BF16)
        out_ref[...] = x

    wg = wg_ref[...].astype(BF16)
    wu = wu_ref[...].astype(BF16)
    wd = wd_ref[...].astype(BF16)
    wg_out[...] = wg
    wu_out[...] = wu
    wd_out[...] = wd
    h = h_scr[...]
    g = jnp.dot(h, wg, preferred_element_type=F32)
    u = jnp.dot(h, wu, preferred_element_type=F32)
    a = (0.5 * (g * jax.nn.sigmoid(g)) * u).astype(BF16)
    out_ref[...] += jnp.dot(a, wd, preferred_element_type=F32)


def _ffn_meta(x, gain, wg, wu, wd, l, *, tf, name):
    m, d = x.shape
    ff = wg.shape[2]
    blocks = (2 * _nbytes((m, d), F32) + 3 * _nbytes((d, tf), F32) + 3 * _nbytes((d, tf), BF16))
    return pl.pallas_call(
        _ffn_meta_kernel,
        grid=(ff // tf,),
        in_specs=[pl.BlockSpec((m, d), lambda f: (0, 0)),
                  pl.BlockSpec((1, d), lambda f: (0, 0)),
                  pl.BlockSpec((None, d, tf), lambda f: (l, 0, f)),
                  pl.BlockSpec((None, d, tf), lambda f: (l, 0, f)),
                  pl.BlockSpec((None, tf, d), lambda f: (l, f, 0))],
        out_specs=[pl.BlockSpec((m, d), lambda f: (0, 0)),
                   pl.BlockSpec((d, tf), lambda f: (0, f)),
                   pl.BlockSpec((d, tf), lambda f: (0, f)),
                   pl.BlockSpec((tf, d), lambda f: (f, 0))],
        out_shape=[jax.ShapeDtypeStruct((m, d), F32),
                   jax.ShapeDtypeStruct((d, ff), BF16),
                   jax.ShapeDtypeStruct((d, ff), BF16),
                   jax.ShapeDtypeStruct((ff, d), BF16)],
        scratch_shapes=[pltpu.VMEM((m, d), BF16)],
        compiler_params=_params(("arbitrary",), blocks, _nbytes((m, d), BF16),
                                3 * _nbytes((d, tf), BF16)),
        name=name,
    )(x, gain, wg, wu, wd)


def _norm_kernel(x_ref, g_ref, out_ref):
    out_ref[...] = _rms(x_ref[...], g_ref[...]).astype(out_ref.dtype)


def _norm(x, gain, *, tm, name):
    m, d = x.shape
    blocks = _nbytes((tm, d), F32) + _nbytes((tm, d), BF16)
    return pl.pallas_call(
        _norm_kernel,
        grid=(m // tm,),
        in_specs=[pl.BlockSpec((tm, d), lambda i: (i, 0)),
                  pl.BlockSpec((1, d), lambda i: (0, 0))],
        out_specs=pl.BlockSpec((tm, d), lambda i: (i, 0)),
        out_shape=jax.ShapeDtypeStruct((m, d), BF16),
        compiler_params=_params(("parallel",), blocks, 0, 2 * _nbytes((tm, d), F32)),
        name=name,
    )(x, gain)


def _cast_kernel(w_ref, out_ref):
    out_ref[...] = w_ref[...].astype(out_ref.dtype)


def _cast_cols(w, l, cols, *, tr, tc, name):
    rows = w.shape[1]
    blocks = _nbytes((tr, tc), w.dtype) + _nbytes((tr, tc), BF16)
    return pl.pallas_call(
        _cast_kernel,
        grid=(rows // tr, cols // tc),
        in_specs=[pl.BlockSpec((None, tr, tc), lambda i, j: (l, i, j))],
        out_specs=pl.BlockSpec((tr, tc), lambda i, j: (i, j)),
        out_shape=jax.ShapeDtypeStruct((rows, cols), BF16),
        compiler_params=_params(("parallel", "parallel"), blocks, 0, 0),
        name=name,
    )(w)


def _tail_cols(w, l, start, *, tr, name):
    rows, cols = w.shape[1:]
    assert start % LANES == 0 and cols - start <= LANES
    blocks = 2 * _nbytes((tr, LANES), F32)
    out = pl.pallas_call(
        _cast_kernel,
        grid=(rows // tr,),
        in_specs=[pl.BlockSpec((None, tr, LANES), lambda i: (l, i, start // LANES))],
        out_specs=pl.BlockSpec((tr, LANES), lambda i: (i, 0)),
        out_shape=jax.ShapeDtypeStruct((rows, LANES), F32),
        compiler_params=_params(("parallel",), blocks, 0, 0),
        name=name,
    )(w)
    return out[:, :cols - start]


class SideCast(NamedTuple):
    w: jax.Array
    l: int
    br: int
    bc: int


def _side(w, l, steps, bc=None):
    rows, cols = w.shape[1:]
    bc = cols if bc is None else bc
    sublanes = 16
    br = next(r for r in range(sublanes, rows + 1, sublanes)
              if rows % r == 0 and (rows // r) * (cols // bc) <= steps)
    return SideCast(w, l, br, bc)


def _side_cast_specs(side, grid):
    rows, cols = side.w.shape[1:]
    ncols = cols // side.bc
    nblocks = (rows // side.br) * ncols
    assert rows % side.br == 0 and cols % side.bc == 0 and nblocks <= math.prod(grid), (side.w.shape, grid)

    def block(*idx):
        step = 0
        for i, n in zip(idx, grid):
            step = step * n + i
        t = jnp.minimum(step, nblocks - 1)
        return t // ncols, t % ncols

    in_spec = pl.BlockSpec((None, side.br, side.bc), lambda *idx: (side.l, *block(*idx)))
    out_spec = pl.BlockSpec((side.br, side.bc), block)
    nbytes = _nbytes((side.br, side.bc), F32) + _nbytes((side.br, side.bc), BF16)
    return in_spec, out_spec, jax.ShapeDtypeStruct((rows, cols), BF16), nbytes


def _side_cast_run(side_in_refs, side_out_refs):
    for src, dst in zip(side_in_refs, side_out_refs):
        dst[...] = src[...].astype(dst.dtype)


def _mm_kernel(*refs, has_scale, has_bias, has_res, n_side):
    n_in = 2 + has_scale + has_bias + has_res + n_side
    a_ref, w_ref = refs[0], refs[1]
    extra = list(refs[2:n_in - n_side])
    out_ref = refs[n_in]
    y = jnp.dot(a_ref[...], w_ref[...], preferred_element_type=F32)
    if has_scale:
        y = y * extra.pop(0)[...]
    if has_bias:
        y = jax.nn.sigmoid(y + extra.pop(0)[...])
    if has_res:
        y = y + extra.pop(0)[...]
    out_ref[...] = y.astype(out_ref.dtype)
    _side_cast_run(refs[n_in - n_side:n_in], refs[n_in + 1:])


def _mm(a, w, *, tm, tn, out_dtype, name, scale=None, bias=None, res=None, sides=()):
    m, k = a.shape
    n = w.shape[1]
    grid = (m // tm, n // tn)
    ins = [a, w]
    in_specs = [pl.BlockSpec((tm, k), lambda i, j: (i, 0)),
                pl.BlockSpec((k, tn), lambda i, j: (0, j))]
    blocks = _nbytes((tm, k), a.dtype) + _nbytes((k, tn), w.dtype) + _nbytes((tm, tn), out_dtype)
    for vec in (scale, bias):
        if vec is not None:
            ins.append(vec)
            in_specs.append(pl.BlockSpec((1, tn), lambda i, j: (0, j)))
    if res is not None:
        ins.append(res)
        in_specs.append(pl.BlockSpec((tm, tn), lambda i, j: (i, j)))
        blocks += _nbytes((tm, tn), res.dtype)
    out_specs = [pl.BlockSpec((tm, tn), lambda i, j: (i, j))]
    out_shape = [jax.ShapeDtypeStruct((m, n), out_dtype)]
    for side in sides:
        in_spec, out_spec, shape, nbytes = _side_cast_specs(side, grid)
        ins.append(side.w)
        in_specs.append(in_spec)
        out_specs.append(out_spec)
        out_shape.append(shape)
        blocks += nbytes
    kern = functools.partial(_mm_kernel, has_scale=scale is not None, has_bias=bias is not None,
                             has_res=res is not None, n_side=len(sides))
    outs = pl.pallas_call(
        kern,
        grid=grid,
        in_specs=in_specs,
        out_specs=out_specs,
        out_shape=out_shape,
        compiler_params=_params(("parallel", "arbitrary"), blocks, 0, 4 * _nbytes((tm, tn), F32)),
        name=name,
    )(*ins)
    return outs if sides else outs[0]


def _kpe_kernel(h_ref, w_ref, cos_ref, sin_ref, out_ref):
    y = jnp.dot(h_ref[...], w_ref[...], preferred_element_type=F32)
    out_ref[...] = (y[:, :LANES] * cos_ref[...] + y[:, LANES:] * sin_ref[...]).astype(out_ref.dtype)


def _kpe(h, w, cos, sin, *, tm, name):
    m, d = h.shape
    nt = cos.shape[0] // tm
    blocks = _nbytes((tm, d), BF16) + _nbytes((d, 2 * LANES), BF16) + 3 * _nbytes((tm, LANES), F32)
    return pl.pallas_call(
        _kpe_kernel,
        grid=(m // tm,),
        in_specs=[pl.BlockSpec((tm, d), lambda i: (i, 0)),
                  pl.BlockSpec((d, 2 * LANES), lambda i: (0, 0)),
                  pl.BlockSpec((tm, LANES), lambda i: (i % nt, 0)),
                  pl.BlockSpec((tm, LANES), lambda i: (i % nt, 0))],
        out_specs=pl.BlockSpec((tm, LANES), lambda i: (i, 0)),
        out_shape=jax.ShapeDtypeStruct((m, LANES), BF16),
        compiler_params=_params(("parallel",), blocks, 0, _nbytes((tm, 2 * LANES), F32)),
        name=name,
    )(h, w, cos, sin)


def _mlaq_kernel(c_ref, g_ref, w_ref, cos_ref, sin_ref, out_ref, cn_scr, *, heads, scale):
    @pl.when(pl.program_id(1) == 0)
    def _():
        cn_scr[...] = _rms(c_ref[...].astype(F32), g_ref[...]).astype(BF16)

    y = jnp.dot(cn_scr[...], w_ref[...], preferred_element_type=F32)
    cos = cos_ref[...]
    sin = sin_ref[...]
    for h in range(heads):
        s0 = h * MLA_SLOT
        out_ref[:, s0:s0 + NOPE_D] = (y[:, s0:s0 + NOPE_D] * scale).astype(out_ref.dtype)
        hi = y[:, s0 + NOPE_D:s0 + MLA_SLOT]
        pe = hi * cos + pltpu.roll(hi, LANES // 2, axis=1) * sin
        out_ref[:, s0 + NOPE_D:s0 + MLA_SLOT] = (pe * scale).astype(out_ref.dtype)


def _mlaq(p, col_block, gain, w, cos, sin, *, tm, heads, scale, name):
    m = p.shape[0]
    k = gain.shape[1]
    wn = heads * MLA_SLOT
    groups = w.shape[1] // wn
    nt = cos.shape[0] // tm
    kern = functools.partial(_mlaq_kernel, heads=heads, scale=scale)
    blocks = (_nbytes((tm, k), BF16) + _nbytes((k, wn), BF16) + 2 * _nbytes((tm, LANES), F32)
              + _nbytes((tm, heads * MLA_SLOT), BF16))
    return pl.pallas_call(
        kern,
        grid=(m // tm, groups),
        in_specs=[pl.BlockSpec((tm, k), lambda i, j: (i, col_block)),
                  pl.BlockSpec((1, k), lambda i, j: (0, 0)),
                  pl.BlockSpec((k, wn), lambda i, j: (0, j)),
                  pl.BlockSpec((tm, LANES), lambda i, j: (i % nt, 0)),
                  pl.BlockSpec((tm, LANES), lambda i, j: (i % nt, 0))],
        out_specs=pl.BlockSpec((tm, heads * MLA_SLOT), lambda i, j: (i, j)),
        out_shape=jax.ShapeDtypeStruct((m, groups * heads * MLA_SLOT), BF16),
        scratch_shapes=[pltpu.VMEM((tm, k), BF16)],
        compiler_params=_params(("parallel", "arbitrary"), blocks, _nbytes((tm, k), BF16),
                                2 * _nbytes((tm, wn), F32)),
        name=name,
    )(p, gain, w, cos, sin)


def _mlakv_kernel(c_ref, g_ref, w_ref, out_ref, cn_scr):
    @pl.when(pl.program_id(1) == 0)
    def _():
        cn_scr[...] = _rms(c_ref[...].astype(F32), g_ref[...]).astype(BF16)

    out_ref[...] = jnp.dot(cn_scr[...], w_ref[...], preferred_element_type=F32).astype(out_ref.dtype)


def _mlakv(p, col_block, gain, w, *, tm, tn, name):
    m = p.shape[0]
    k, n = w.shape
    blocks = _nbytes((tm, k), BF16) + _nbytes((k, tn), BF16) + _nbytes((tm, tn), BF16)
    return pl.pallas_call(
        _mlakv_kernel,
        grid=(m // tm, n // tn),
        in_specs=[pl.BlockSpec((tm, k), lambda i, j: (i, col_block)),
                  pl.BlockSpec((1, k), lambda i, j: (0, 0)),
                  pl.BlockSpec((k, tn), lambda i, j: (0, j))],
        out_specs=pl.BlockSpec((tm, tn), lambda i, j: (i, j)),
        out_shape=jax.ShapeDtypeStruct((m, n), BF16),
        scratch_shapes=[pltpu.VMEM((tm, k), BF16)],
        compiler_params=_params(("parallel", "arbitrary"), blocks, _nbytes((tm, k), BF16),
                                2 * _nbytes((tm, tn), F32)),
        name=name,
    )(p, gain, w)


def _dot_nt(a, b):
    return lax.dot_general(a, b, (((1,), (1,)), ((), ())), preferred_element_type=F32)


def _transpose_bf16(x):
    return x.astype(F32).T.astype(BF16)


def _softmax_first_t(s_t, v_t):
    m = jnp.max(s_t, axis=0, keepdims=True)
    p = jnp.exp2(s_t - m)
    return m, jnp.sum(p, axis=0, keepdims=True), jnp.dot(v_t, p.astype(BF16), preferred_element_type=F32)


def _softmax_next_t(s_t, shift, v_t, m, l, acc):
    m_new = jnp.maximum(m, jnp.max(s_t, axis=0, keepdims=True) + shift)
    alpha = jnp.exp2(m - m_new)
    p = jnp.exp2(s_t - (m_new - shift))
    l = alpha * l + jnp.sum(p, axis=0, keepdims=True)
    acc = alpha * acc + jnp.dot(v_t, p.astype(BF16), preferred_element_type=F32)
    return m_new, l, acc


def _meta_mask_t(s_t):
    row = lax.broadcasted_iota(jnp.int32, s_t.shape, 0)
    return jnp.where(row < N_META, s_t, NEG_BIG)


def _diff_attn_kernel(q_ref, k_ref, v_ref, km_ref, vm_ref, slope_ref, lq1_ref, lk1_ref, lq2_ref,
                      lk2_ref, subln_ref, out_ref, vt_scr, bias_scr, *, tk, lam_init):
    tq = q_ref.shape[0]
    ratio = tk // tq
    nk = k_ref.shape[0] // tk
    hd = DIFF_HEAD_DIM
    qi = pl.program_id(2)
    slope2 = slope_ref[...] * LOG2E

    @pl.when(qi == 0)
    def _():
        vt_scr[:, :META_PAD] = _transpose_bf16(vm_ref[...])
        for c in range(nk):
            vt_scr[:, META_PAD + c * tk:META_PAD + (c + 1) * tk] = _transpose_bf16(
                v_ref[c * tk:(c + 1) * tk, :])
        d = (lax.broadcasted_iota(jnp.int32, (tk, tq), 1)
             - lax.broadcasted_iota(jnp.int32, (tk, tq), 0)).astype(F32) * slope2
        bias_scr[0] = -d
        bias_scr[1] = d
        for o in range(ratio):
            bias_scr[2 + o] = -jnp.abs(d + (o * tq) * slope2)

    q1 = q_ref[:, :hd]
    q2 = q_ref[:, hd:]

    def qk(j):
        kb = k_ref[j * tk:(j + 1) * tk, :]
        return _dot_nt(kb[:, :hd], q1), _dot_nt(kb[:, hd:], q2)

    vtm = vt_scr[:, :META_PAD]
    st1 = _softmax_first_t(_meta_mask_t(_dot_nt(km_ref[:, :hd], q1)), vtm)
    st2 = _softmax_first_t(_meta_mask_t(_dot_nt(km_ref[:, hd:], q2)), vtm)

    s1, s2 = qk(0)
    for j in range(nk):
        nxt = qk(j + 1) if j + 1 < nk else None
        off = qi - j * ratio
        bias = bias_scr[jnp.where(off < 0, 1, jnp.where(off >= ratio, 0, 2 + off))]
        gap = jnp.where(off < 0, j * tk - qi * tq, jnp.where(off >= ratio, qi * tq - j * tk, 0))
        shift = -gap.astype(F32) * slope2
        vt = vt_scr[:, META_PAD + j * tk:META_PAD + (j + 1) * tk]
        st1 = _softmax_next_t(s1 + bias, shift, vt, *st1)
        st2 = _softmax_next_t(s2 + bias, shift, vt, *st2)
        if nxt is not None:
            s1, s2 = nxt

    lam = (jnp.exp(jnp.sum(lq1_ref[...] * lk1_ref[...], axis=-1, keepdims=True))
           - jnp.exp(jnp.sum(lq2_ref[...] * lk2_ref[...], axis=-1, keepdims=True)) + lam_init)
    o_t = st1[2] / st1[1] - lam * (st2[2] / st2[1])
    out_ref[...] = (_rms(o_t.T, subln_ref[...]) * (1.0 - lam_init)).astype(out_ref.dtype)


def _diff_attn(p, pm, slopes, lq1, lk1, lq2, lk2, subln, *, batch, seq, tq, tk, lam_init, name):
    assert tk % tq == 0 and seq % tk == 0
    m = p.shape[0]
    hw = 2 * DIFF_HEAD_DIM
    nq = seq // tq
    koff = DIFF_HEADS
    voff = 2 * DIFF_HEADS
    kern = functools.partial(_diff_attn_kernel, tk=tk, lam_init=lam_init)
    n_bias = 2 + tk // tq
    vec = pl.BlockSpec((1, DIFF_HEAD_DIM), lambda b, h, i: (0, 0))
    blocks = (2 * _nbytes((tq, hw), BF16) + 2 * _nbytes((seq, hw), BF16)
              + 2 * _nbytes((META_PAD, hw), BF16))
    scratch = _nbytes((hw, seq + META_PAD), BF16) + n_bias * _nbytes((tk, tq), F32)
    return pl.pallas_call(
        kern,
        grid=(batch, DIFF_HEADS, nq),
        in_specs=[pl.BlockSpec((tq, hw), lambda b, h, i: (b * nq + i, h)),
                  pl.BlockSpec((seq, hw), lambda b, h, i: (b, koff + h)),
                  pl.BlockSpec((seq, hw), lambda b, h, i: (b, voff + h)),
                  pl.BlockSpec((META_PAD, hw), lambda b, h, i: (0, koff + h)),
                  pl.BlockSpec((META_PAD, hw), lambda b, h, i: (0, voff + h)),
                  pl.BlockSpec((None, 1, 1), lambda b, h, i: (h, 0, 0)),
                  vec, vec, vec, vec,
                  pl.BlockSpec((1, hw), lambda b, h, i: (0, 0))],
        out_specs=pl.BlockSpec((tq, hw), lambda b, h, i: (b * nq + i, h)),
        out_shape=jax.ShapeDtypeStruct((m, DIFF_HEADS * hw), BF16),
        scratch_shapes=[pltpu.VMEM((hw, seq + META_PAD), BF16),
                        pltpu.VMEM((n_bias, tk, tq), F32)],
        compiler_params=_params(("parallel", "parallel", "arbitrary"), blocks, scratch,
                                12 * _nbytes((tk, tq), F32)),
        name=name,
    )(p, p, p, pm, pm, slopes, lq1, lk1, lq2, lk2, subln)


def _mla_attn_kernel(q_ref, kv_ref, kpe_ref, kvm_ref, kpem_ref, out_ref, k_scr, vt_scr, *, tk):
    nk = kv_ref.shape[0] // tk

    @pl.when(pl.program_id(2) == 0)
    def _():
        k_scr[:META_PAD, :NOPE_D] = kvm_ref[:, :NOPE_D]
        k_scr[:META_PAD, NOPE_D:] = kpem_ref[...]
        k_scr[META_PAD:, :NOPE_D] = kv_ref[:, :NOPE_D]
        k_scr[META_PAD:, NOPE_D:] = kpe_ref[...]
        vt_scr[:, :META_PAD] = _transpose_bf16(kvm_ref[:, NOPE_D:])
        for c in range(nk):
            vt_scr[:, META_PAD + c * tk:META_PAD + (c + 1) * tk] = _transpose_bf16(
                kv_ref[c * tk:(c + 1) * tk, NOPE_D:])

    q = q_ref[...]

    def qk(j):
        return _dot_nt(k_scr[META_PAD + j * tk:META_PAD + (j + 1) * tk, :], q)

    st = _softmax_first_t(_meta_mask_t(_dot_nt(k_scr[:META_PAD, :], q)), vt_scr[:, :META_PAD])

    s = qk(0)
    for j in range(nk):
        nxt = qk(j + 1) if j + 1 < nk else None
        st = _softmax_next_t(s, 0.0, vt_scr[:, META_PAD + j * tk:META_PAD + (j + 1) * tk], *st)
        s = nxt

    out_ref[...] = (st[2] / st[1]).T.astype(out_ref.dtype)


def _mla_attn(q, kv, kpe, kvm, kpem, *, batch, seq, tq, tk, name):
    mrows = q.shape[0]
    nq = seq // tq
    kern = functools.partial(_mla_attn_kernel, tk=tk)
    blocks = (_nbytes((tq, MLA_SLOT), BF16) + _nbytes((seq, MLA_SLOT), BF16)
              + _nbytes((seq, LANES), BF16) + _nbytes((META_PAD, MLA_SLOT + LANES), BF16)
              + _nbytes((tq, MLA_V_D), BF16))
    scratch = _nbytes((seq + META_PAD, MLA_SLOT), BF16) + _nbytes((MLA_V_D, seq + META_PAD), BF16)
    return pl.pallas_call(
        kern,
        grid=(batch, MLA_HEADS, nq),
        in_specs=[pl.BlockSpec((tq, MLA_SLOT), lambda b, h, i: (b * nq + i, h)),
                  pl.BlockSpec((seq, MLA_SLOT), lambda b, h, i: (b, h)),
                  pl.BlockSpec((seq, LANES), lambda b, h, i: (b, 0)),
                  pl.BlockSpec((META_PAD, MLA_SLOT), lambda b, h, i: (0, h)),
                  pl.BlockSpec((META_PAD, LANES), lambda b, h, i: (0, 0))],
        out_specs=pl.BlockSpec((tq, MLA_V_D), lambda b, h, i: (b * nq + i, h)),
        out_shape=jax.ShapeDtypeStruct((mrows, MLA_HEADS * MLA_V_D), BF16),
        scratch_shapes=[pltpu.VMEM((seq + META_PAD, MLA_SLOT), BF16),
                        pltpu.VMEM((MLA_V_D, seq + META_PAD), BF16)],
        compiler_params=_params(("parallel", "parallel", "arbitrary"), blocks, scratch,
                                6 * _nbytes((tk, tq), F32)),
        name=name,
    )(q, kv, kpe, kvm, kpem)


def _merge_kernel(od_ref, om_ref, wa_ref, wb_ref, gd_ref, gm_ref, side_ref, out_ref, side_out_ref):
    ya = jnp.dot(od_ref[...], wa_ref[...], preferred_element_type=F32)
    yb = jnp.dot(om_ref[...], wb_ref[...], preferred_element_type=F32)
    out_ref[...] = (gd_ref[...].astype(F32) * ya + gm_ref[...].astype(F32) * yb).astype(out_ref.dtype)
    _side_cast_run([side_ref], [side_out_ref])


def _merge(od, om, wa, wb, gates, side, *, tm, tn, name):
    m, k = od.shape
    n = wa.shape[1]
    nj = n // tn
    grid = (m // tm, nj)
    side_in, side_out, side_shape, side_bytes = _side_cast_specs(side, grid)
    blocks = (2 * _nbytes((tm, k), BF16) + 2 * _nbytes((k, tn), BF16) + 3 * _nbytes((tm, tn), BF16)
              + side_bytes)
    return pl.pallas_call(
        _merge_kernel,
        grid=grid,
        in_specs=[pl.BlockSpec((tm, k), lambda i, j: (i, 0)),
                  pl.BlockSpec((tm, k), lambda i, j: (i, 0)),
                  pl.BlockSpec((k, tn), lambda i, j: (0, j)),
                  pl.BlockSpec((k, tn), lambda i, j: (0, j)),
                  pl.BlockSpec((tm, tn), lambda i, j: (i, j)),
                  pl.BlockSpec((tm, tn), lambda i, j: (i, nj + j)),
                  side_in],
        out_specs=[pl.BlockSpec((tm, tn), lambda i, j: (i, j)), side_out],
        out_shape=[jax.ShapeDtypeStruct((m, n), BF16), side_shape],
        compiler_params=_params(("parallel", "arbitrary"), blocks, 0, 4 * _nbytes((tm, tn), F32)),
        name=name,
    )(od, om, wa, wb, gates, gates, side.w)


def _rope_tables(t):
    inv_freq = 1.0 / (ROPE_THETA ** (jnp.arange(0, ROPE_D, 2, dtype=F32) / ROPE_D))
    ang = jnp.arange(t).astype(F32)[:, None] * inv_freq[None, :]
    pad = jnp.zeros((t, LANES - ROPE_D), F32)
    cos = jnp.concatenate([jnp.cos(ang), jnp.cos(ang), pad], axis=-1)
    sin = jnp.concatenate([jnp.sin(ang), jnp.sin(ang), pad], axis=-1)
    return cos, sin


def _rotate_cols(w):
    half = ROPE_D // 2
    return jnp.concatenate([-w[..., half:], w[..., :half]], axis=-1)


def kernel(x, meta_tokens, ffn1_norm, ffn1_w_gate, ffn1_w_up, ffn1_w_down, mix_norm, w_in, diff_lambda_q1, diff_lambda_k1, diff_lambda_q2, diff_lambda_k2, diff_subln, mla_q_norm, mla_w_uq, mla_kv_norm, mla_w_ukv, w_gate, b_gate, w_branch_diff, w_branch_mla, w_out, ffn2_norm, ffn2_w_gate, ffn2_w_up, ffn2_w_down, final_norm):
    batch, seq, d = x.shape
    depth = ffn1_norm.shape[0]
    m = batch * seq
    dq_w = DIFF_HEADS * 2 * DIFF_HEAD_DIM
    dv_w = DIFF_HEADS * DIFF_V_DIM
    main_w = 2 * dq_w + dv_w + Q_LORA + KV_LORA
    cq_block = (2 * dq_w + dv_w) // Q_LORA
    ckv_block = (2 * dq_w + dv_w + Q_LORA) // KV_LORA
    q_heads = 4

    cos, sin = _rope_tables(N_META + seq)
    cos_m, sin_m, cos_r, sin_r = cos[:N_META], sin[:N_META], cos[N_META:], sin[N_META:]
    slopes = jnp.asarray([2.0 ** (-8.0 * (h + 1) / DIFF_HEADS) for h in range(DIFF_HEADS)],
                         F32).reshape(DIFF_HEADS, 1, 1)
    col_scale = jnp.concatenate([jnp.full((1, dq_w), DIFF_HEAD_DIM ** -0.5 * LOG2E, F32),
                                 jnp.ones((1, main_w - dq_w), F32)], axis=-1)
    row = lambda v: v.reshape(1, -1).astype(F32)

    xs = x.reshape(m, d)
    xm = meta_tokens.astype(x.dtype)

    for l in range(depth):
        lam_init = 0.8 - 0.6 * math.exp(-0.3 * l)
        w_main = _cast_cols(w_in, l, main_w, tr=512, tc=768, name="cast_w_in")
        w_kr = _tail_cols(w_in, l, main_w, tr=512, name="w_in_rope_cols")
        zpad = jnp.zeros((d, LANES - ROPE_D), F32)
        w_kpe = jnp.concatenate([w_kr, zpad, _rotate_cols(w_kr), zpad], axis=-1).astype(BF16)

        wuq = mla_w_uq[l].reshape(Q_LORA, MLA_HEADS, NOPE_D + ROPE_D)
        w_q = jnp.concatenate([wuq, _rotate_cols(wuq[..., NOPE_D:])], axis=-1)
        w_q = w_q.reshape(Q_LORA, MLA_HEADS * MLA_SLOT).astype(BF16)
        w_ukv = mla_w_ukv[l].astype(BF16)
        w_pa = w_branch_diff[l].astype(BF16)
        w_pb = w_branch_mla[l].astype(BF16)
        w_o = w_out[l].astype(BF16)

        xm, wg1, wu1, wd1 = _ffn_meta(xm, row(ffn1_norm[l]), ffn1_w_gate, ffn1_w_up, ffn1_w_down, l,
                                      tf=256, name="ffn1_meta")
        xs = _ffn(xs, row(ffn1_norm[l]), wg1, wu1, wd1, row(final_norm),
                  tm=512, tf=256, final_norm=False, name="ffn1")

        hs = _norm(xs, row(mix_norm[l]), tm=512, name="mix_norm")
        hm = _norm(xm, row(mix_norm[l]), tm=N_META, name="mix_norm_meta")

        ps, wg2, w_g = _mm(hs, w_main, tm=1024, tn=768, out_dtype=BF16, scale=col_scale, name="proj",
                           sides=(_side(ffn2_w_gate, l, (m // 1024) * (main_w // 768)),
                                  _side(w_gate, l, (m // 1024) * (main_w // 768))))
        pm = _mm(hm, w_main, tm=N_META, tn=768, out_dtype=BF16, scale=col_scale, name="proj_meta")
        gates, wu2 = _mm(hs, w_g, tm=1024, tn=1024, out_dtype=BF16, bias=row(b_gate[l]), name="gates",
                         sides=(_side(ffn2_w_up, l, (m // 1024) * (2 * d // 1024)),))
        kpe_s = _kpe(hs, w_kpe, cos_r, sin_r, tm=1024, name="kpe")
        kpe_m = _kpe(hm, w_kpe, cos_m, sin_m, tm=N_META, name="kpe_meta")

        q_mla = _mlaq(ps, cq_block, row(mla_q_norm[l]), w_q, cos_r, sin_r, tm=512, heads=q_heads,
                      scale=(NOPE_D + ROPE_D) ** -0.5 * LOG2E, name="mla_q")
        kv_s = _mlakv(ps, ckv_block, row(mla_kv_norm[l]), w_ukv, tm=1024, tn=1024, name="mla_kv")
        kv_m = _mlakv(pm, ckv_block, row(mla_kv_norm[l]), w_ukv, tm=N_META, tn=1024, name="mla_kv_meta")

        pad_rows = lambda a: jnp.pad(a, ((0, META_PAD - N_META), (0, 0)))
        o_diff = _diff_attn(ps, pad_rows(pm), slopes, row(diff_lambda_q1[l]), row(diff_lambda_k1[l]),
                            row(diff_lambda_q2[l]), row(diff_lambda_k2[l]), row(diff_subln[l]),
                            batch=batch, seq=seq, tq=256, tk=512, lam_init=lam_init, name="diff_attn")
        o_mla = _mla_attn(q_mla, kv_s, kpe_s, pad_rows(kv_m), pad_rows(kpe_m),
                          batch=batch, seq=seq, tq=2048, tk=512, name="mla_attn")

        merged, wd2 = _merge(o_diff, o_mla, w_pa, w_pb, gates,
                             _side(ffn2_w_down, l, (m // 1024) * (d // 512), bc=512),
                             tm=1024, tn=512, name="merge")
        xs = _mm(merged, w_o, tm=1024, tn=512, out_dtype=F32, res=xs, name="out_proj")

        last = l == depth - 1
        xs = _ffn(xs, row(ffn2_norm[l]), wg2, wu2, wd2, row(final_norm),
                  tm=512, tf=256, final_norm=last, name="ffn2")
        if not last:
            raise NotImplementedError("DEPTH > 1 needs the meta-row query path")

    return xs.reshape(batch, seq, d)
```

```python
import functools
import math
from typing import NamedTuple

import numpy as np
import jax
import jax.numpy as jnp
from jax import lax
from jax.experimental import pallas as pl
from jax.experimental.pallas import tpu as pltpu

N_META = 16
EPS = 1e-6
DIFF_HEADS = 8
DIFF_HEAD_DIM = 128
DIFF_V_DIM = 256
MLA_HEADS = 16
Q_LORA = 1024
KV_LORA = 512
NOPE_D = 128
ROPE_D = 64
MLA_V_D = 128
ROPE_THETA = 10000.0

LANES = 128
MLA_SLOT = 256
META_PAD = 128
VMEM_CAP = 60000 * 1024
VMEM_PHYSICAL = 64 * 2**20
VMEM_RESERVE = 2 * 2**20
VMEM_FLOOR = 16 * 2**20
NEG_BIG = -1e30
LOG2E = math.log2(math.e)

ROWS = 1024
PROJ_COLS = 768
GATE_COLS = 1024
MERGE_COLS = 512
OUT_COLS = 512
FFN_ROWS = 512
FFN_CHUNK = 256
CAST_ROWS, CAST_COLS = 1024, 1536
MLAQ_HEADS = 8
MLAKV_COLS = 1024
DIFF_TQ, DIFF_TK = 256, 512
MLA_TQ, MLA_TK = 2048, 512

F32 = jnp.float32
BF16 = jnp.bfloat16


def _nbytes(shape, dtype):
    return int(np.prod(shape)) * jnp.dtype(dtype).itemsize


def _params(semantics, block_bytes, scratch_bytes, temp_bytes):
    need = 2 * block_bytes + scratch_bytes + temp_bytes
    return pltpu.CompilerParams(dimension_semantics=semantics,
                                vmem_limit_bytes=int(min(VMEM_CAP, max(need, VMEM_FLOOR))))


def _rms(x, gain):
    return x * lax.rsqrt(jnp.mean(x * x, axis=-1, keepdims=True) + EPS) * gain


def _ffn_kernel(x_ref, g_ref, wga_ref, wua_ref, wda_ref, wgb_ref, wub_ref, wdb_ref, *rest,
                final_norm, nchunks):
    out_ref, h_scr = rest[-2:]
    f = pl.program_id(1)
    last = pl.num_programs(1) - 1

    def act(wg_ref, wu_ref):
        h = h_scr[...]
        g = jnp.dot(h, wg_ref[...], preferred_element_type=F32)
        u = jnp.dot(h, wu_ref[...], preferred_element_type=F32)
        return (0.5 * (g * jax.nn.sigmoid(g)) * u).astype(BF16)

    def both():
        a = act(wga_ref, wua_ref)
        b = act(wgb_ref, wub_ref)
        return (jnp.dot(a, wda_ref[...], preferred_element_type=F32)
                + jnp.dot(b, wdb_ref[...], preferred_element_type=F32))

    @pl.when(f == 0)
    def _():
        x = x_ref[...]
        h_scr[...] = _rms(x, g_ref[...]).astype(BF16)
        out_ref[...] = x + both()

    @pl.when((f > 0) & (f < last) if nchunks % 2 else f > 0)
    def _():
        out_ref[...] += both()

    if nchunks % 2:
        @pl.when((f == last) & (f > 0))
        def _():
            out_ref[...] += jnp.dot(act(wga_ref, wua_ref), wda_ref[...], preferred_element_type=F32)

    if final_norm:
        gf_ref = rest[0]

        @pl.when(f == last)
        def _():
            out_ref[...] = _rms(out_ref[...], gf_ref[...])


def _ffn(x, gain, wg, wu, wd, *, tm, tf, name, final_gain=None):
    m, d = x.shape
    nchunks = wg.shape[1] // tf
    assert nchunks != 1, "the first grid step always handles two chunks"
    kern = functools.partial(_ffn_kernel, final_norm=final_gain is not None, nchunks=nchunks)
    vec = pl.BlockSpec((1, d), lambda i, f: (0, 0))
    finals = [] if final_gain is None else [final_gain]
    col_a = lambda i, f: (0, 2 * f)
    col_b = lambda i, f: (0, jnp.minimum(2 * f + 1, nchunks - 1))
    row_a = lambda i, f: (2 * f, 0)
    row_b = lambda i, f: (jnp.minimum(2 * f + 1, nchunks - 1), 0)
    weights = 2 * (2 * _nbytes((d, tf), BF16) + _nbytes((tf, d), BF16))
    need = (_nbytes((tm, d), F32) + 2 * _nbytes((tm, d), F32) + 2 * weights
            + _nbytes((tm, d), BF16) + 6 * _nbytes((tm, tf), F32) + _nbytes((tm, d), F32))
    return pl.pallas_call(
        kern,
        grid=(m // tm, (nchunks + 1) // 2),
        in_specs=[
            pl.BlockSpec((tm, d), lambda i, f: (i, 0), pipeline_mode=pl.Buffered(1)),
            vec,
            pl.BlockSpec((d, tf), col_a), pl.BlockSpec((d, tf), col_a), pl.BlockSpec((tf, d), row_a),
            pl.BlockSpec((d, tf), col_b), pl.BlockSpec((d, tf), col_b), pl.BlockSpec((tf, d), row_b),
        ] + [vec] * len(finals),
        out_specs=pl.BlockSpec((tm, d), lambda i, f: (i, 0)),
        out_shape=jax.ShapeDtypeStruct((m, d), F32),
        scratch_shapes=[pltpu.VMEM((tm, d), BF16)],
        compiler_params=pltpu.CompilerParams(
            dimension_semantics=("parallel", "arbitrary"),
            vmem_limit_bytes=int(min(VMEM_PHYSICAL - VMEM_RESERVE, max(need, VMEM_FLOOR)))),
        name=name,
    )(x, gain, wg, wu, wd, wg, wu, wd, *finals)


def _ffn_meta_kernel(x_ref, g_ref, wg_ref, wu_ref, wd_ref, out_ref, wg_out, wu_out, wd_out, h_scr):
    @pl.when(pl.program_id(0) == 0)
    def _():
        x = x_ref[...]
        h_scr[...] = _rms(x, g_ref[...]).astype(BF16)
        out_ref[...] = x

    wg = wg_ref[...].astype(BF16)
    wu = wu_ref[...].astype(BF16)
    wd = wd_ref[...].astype(BF16)
    wg_out[...] = wg
    wu_out[...] = wu
    wd_out[...] = wd
    h = h_scr[...]
    g = jnp.dot(h, wg, preferred_element_type=F32)
    u = jnp.dot(h, wu, preferred_element_type=F32)
    a = (0.5 * (g * jax.nn.sigmoid(g)) * u).astype(BF16)
    out_ref[...] += jnp.dot(a, wd, preferred_element_type=F32)


def _ffn_meta(x, gain, wg, wu, wd, l, *, tf, name):
    m, d = x.shape
    ff = wg.shape[2]
    blocks = (2 * _nbytes((m, d), F32) + 3 * _nbytes((d, tf), F32) + 3 * _nbytes((d, tf), BF16))
    return pl.pallas_call(
        _ffn_meta_kernel,
        grid=(ff // tf,),
        in_specs=[pl.BlockSpec((m, d), lambda f: (0, 0)),
                  pl.BlockSpec((1, d), lambda f: (0, 0)),
                  pl.BlockSpec((None, d, tf), lambda f: (l, 0, f)),
                  pl.BlockSpec((None, d, tf), lambda f: (l, 0, f)),
                  pl.BlockSpec((None, tf, d), lambda f: (l, f, 0))],
        out_specs=[pl.BlockSpec((m, d), lambda f: (0, 0)),
                   pl.BlockSpec((d, tf), lambda f: (0, f)),
                   pl.BlockSpec((d, tf), lambda f: (0, f)),
                   pl.BlockSpec((tf, d), lambda f: (f, 0))],
        out_shape=[jax.ShapeDtypeStruct((m, d), F32),
                   jax.ShapeDtypeStruct((d, ff), BF16),
                   jax.ShapeDtypeStruct((d, ff), BF16),
                   jax.ShapeDtypeStruct((ff, d), BF16)],
        scratch_shapes=[pltpu.VMEM((m, d), BF16)],
        compiler_params=_params(("arbitrary",), blocks, _nbytes((m, d), BF16),
                                3 * _nbytes((d, tf), BF16)),
        name=name,
    )(x, gain, wg, wu, wd)


def _norm_kernel(x_ref, g_ref, out_ref):
    out_ref[...] = _rms(x_ref[...], g_ref[...]).astype(out_ref.dtype)


def _norm(x, gain, *, tm, name):
    m, d = x.shape
    blocks = _nbytes((tm, d), F32) + _nbytes((tm, d), BF16)
    return pl.pallas_call(
        _norm_kernel,
        grid=(m // tm,),
        in_specs=[pl.BlockSpec((tm, d), lambda i: (i, 0)),
                  pl.BlockSpec((1, d), lambda i: (0, 0))],
        out_specs=pl.BlockSpec((tm, d), lambda i: (i, 0)),
        out_shape=jax.ShapeDtypeStruct((m, d), BF16),
        compiler_params=_params(("parallel",), blocks, 0, 2 * _nbytes((tm, d), F32)),
        name=name,
    )(x, gain)


def _cast_kernel(w_ref, out_ref):
    out_ref[...] = w_ref[...].astype(out_ref.dtype)


def _cast_cols(w, l, cols, *, tr, tc, name):
    rows = w.shape[1]
    blocks = _nbytes((tr, tc), w.dtype) + _nbytes((tr, tc), BF16)
    return pl.pallas_call(
        _cast_kernel,
        grid=(rows // tr, cols // tc),
        in_specs=[pl.BlockSpec((None, tr, tc), lambda i, j: (l, i, j))],
        out_specs=pl.BlockSpec((tr, tc), lambda i, j: (i, j)),
        out_shape=jax.ShapeDtypeStruct((rows, cols), BF16),
        compiler_params=_params(("parallel", "parallel"), blocks, 0, 0),
        name=name,
    )(w)


def _tail_cols(w, l, start, *, tr, name):
    rows, cols = w.shape[1:]
    assert start % LANES == 0 and cols - start <= LANES
    blocks = 2 * _nbytes((tr, LANES), F32)
    out = pl.pallas_call(
        _cast_kernel,
        grid=(rows // tr,),
        in_specs=[pl.BlockSpec((None, tr, LANES), lambda i: (l, i, start // LANES))],
        out_specs=pl.BlockSpec((tr, LANES), lambda i: (i, 0)),
        out_shape=jax.ShapeDtypeStruct((rows, LANES), F32),
        compiler_params=_params(("parallel",), blocks, 0, 0),
        name=name,
    )(w)
    return out[:, :cols - start]


class SideCast(NamedTuple):
    w: jax.Array
    l: int
    br: int
    bc: int


def _side(w, l, steps, bc=None):
    rows, cols = w.shape[1:]
    bc = cols if bc is None else bc
    sublanes = 16
    br = next(r for r in range(sublanes, rows + 1, sublanes)
              if rows % r == 0 and (rows // r) * (cols // bc) <= steps)
    return SideCast(w, l, br, bc)


def _side_cast_specs(side, grid):
    rows, cols = side.w.shape[1:]
    ncols = cols // side.bc
    nblocks = (rows // side.br) * ncols
    assert rows % side.br == 0 and cols % side.bc == 0 and nblocks <= math.prod(grid), (side.w.shape, grid)

    def block(*idx):
        step = 0
        for i, n in zip(idx, grid):
            step = step * n + i
        t = jnp.minimum(step, nblocks - 1)
        return t // ncols, t % ncols

    in_spec = pl.BlockSpec((None, side.br, side.bc), lambda *idx: (side.l, *block(*idx)))
    out_spec = pl.BlockSpec((side.br, side.bc), block)
    nbytes = _nbytes((side.br, side.bc), F32) + _nbytes((side.br, side.bc), BF16)
    return in_spec, out_spec, jax.ShapeDtypeStruct((rows, cols), BF16), nbytes


def _side_cast_run(side_in_refs, side_out_refs):
    for src, dst in zip(side_in_refs, side_out_refs):
        dst[...] = src[...].astype(dst.dtype)


def _mm_kernel(*refs, has_scale, has_bias, has_res, n_side):
    n_in = 2 + has_scale + has_bias + has_res + n_side
    a_ref, w_ref = refs[0], refs[1]
    extra = list(refs[2:n_in - n_side])
    out_ref = refs[n_in]
    y = jnp.dot(a_ref[...], w_ref[...], preferred_element_type=F32)
    if has_scale:
        y = y * extra.pop(0)[...]
    if has_bias:
        y = jax.nn.sigmoid(y + extra.pop(0)[...])
    if has_res:
        y = y + extra.pop(0)[...]
    out_ref[...] = y.astype(out_ref.dtype)
    _side_cast_run(refs[n_in - n_side:n_in], refs[n_in + 1:])


def _mm(a, w, *, tm, tn, out_dtype, name, scale=None, bias=None, res=None, sides=()):
    m, k = a.shape
    n = w.shape[1]
    grid = (m // tm, n // tn)
    ins = [a, w]
    in_specs = [pl.BlockSpec((tm, k), lambda i, j: (i, 0)),
                pl.BlockSpec((k, tn), lambda i, j: (0, j))]
    blocks = _nbytes((tm, k), a.dtype) + _nbytes((k, tn), w.dtype) + _nbytes((tm, tn), out_dtype)
    for vec in (scale, bias):
        if vec is not None:
            ins.append(vec)
            in_specs.append(pl.BlockSpec((1, tn), lambda i, j: (0, j)))
    if res is not None:
        ins.append(res)
        in_specs.append(pl.BlockSpec((tm, tn), lambda i, j: (i, j)))
        blocks += _nbytes((tm, tn), res.dtype)
    out_specs = [pl.BlockSpec((tm, tn), lambda i, j: (i, j))]
    out_shape = [jax.ShapeDtypeStruct((m, n), out_dtype)]
    for side in sides:
        in_spec, out_spec, shape, nbytes = _side_cast_specs(side, grid)
        ins.append(side.w)
        in_specs.append(in_spec)
        out_specs.append(out_spec)
        out_shape.append(shape)
        blocks += nbytes
    kern = functools.partial(_mm_kernel, has_scale=scale is not None, has_bias=bias is not None,
                             has_res=res is not None, n_side=len(sides))
    outs = pl.pallas_call(
        kern,
        grid=grid,
        in_specs=in_specs,
        out_specs=out_specs,
        out_shape=out_shape,
        compiler_params=_params(("parallel", "arbitrary"), blocks, 0, 4 * _nbytes((tm, tn), F32)),
        name=name,
    )(*ins)
    return outs if sides else outs[0]


def _kpe_kernel(h_ref, w_ref, cos_ref, sin_ref, out_ref):
    y = jnp.dot(h_ref[...], w_ref[...], preferred_element_type=F32)
    out_ref[...] = (y[:, :LANES] * cos_ref[...] + y[:, LANES:] * sin_ref[...]).astype(out_ref.dtype)


def _kpe(h, w, cos, sin, *, tm, name):
    m, d = h.shape
    nt = cos.shape[0] // tm
    blocks = _nbytes((tm, d), BF16) + _nbytes((d, 2 * LANES), BF16) + 3 * _nbytes((tm, LANES), F32)
    return pl.pallas_call(
        _kpe_kernel,
        grid=(m // tm,),
        in_specs=[pl.BlockSpec((tm, d), lambda i: (i, 0)),
                  pl.BlockSpec((d, 2 * LANES), lambda i: (0, 0)),
                  pl.BlockSpec((tm, LANES), lambda i: (i % nt, 0)),
                  pl.BlockSpec((tm, LANES), lambda i: (i % nt, 0))],
        out_specs=pl.BlockSpec((tm, LANES), lambda i: (i, 0)),
        out_shape=jax.ShapeDtypeStruct((m, LANES), BF16),
        compiler_params=_params(("parallel",), blocks, 0, _nbytes((tm, 2 * LANES), F32)),
        name=name,
    )(h, w, cos, sin)


def _mlaq_kernel(c_ref, g_ref, w_ref, cos_ref, sin_ref, out_ref, cn_scr, *, heads, scale):
    @pl.when(pl.program_id(1) == 0)
    def _():
        cn_scr[...] = _rms(c_ref[...].astype(F32), g_ref[...]).astype(BF16)

    y = jnp.dot(cn_scr[...], w_ref[...], preferred_element_type=F32)
    cos = cos_ref[...]
    sin = sin_ref[...]
    for h in range(heads):
        s0 = h * MLA_SLOT
        out_ref[:, s0:s0 + NOPE_D] = (y[:, s0:s0 + NOPE_D] * scale).astype(out_ref.dtype)
        hi = y[:, s0 + NOPE_D:s0 + MLA_SLOT]
        pe = hi * cos + pltpu.roll(hi, LANES // 2, axis=1) * sin
        out_ref[:, s0 + NOPE_D:s0 + MLA_SLOT] = (pe * scale).astype(out_ref.dtype)


def _mlaq(p, col_block, gain, w, cos, sin, *, tm, heads, scale, name):
    m = p.shape[0]
    k = gain.shape[1]
    wn = heads * MLA_SLOT
    groups = w.shape[1] // wn
    nt = cos.shape[0] // tm
    kern = functools.partial(_mlaq_kernel, heads=heads, scale=scale)
    blocks = (_nbytes((tm, k), BF16) + _nbytes((k, wn), BF16) + 2 * _nbytes((tm, LANES), F32)
              + _nbytes((tm, heads * MLA_SLOT), BF16))
    return pl.pallas_call(
        kern,
        grid=(m // tm, groups),
        in_specs=[pl.BlockSpec((tm, k), lambda i, j: (i, col_block)),
                  pl.BlockSpec((1, k), lambda i, j: (0, 0)),
                  pl.BlockSpec((k, wn), lambda i, j: (0, j)),
                  pl.BlockSpec((tm, LANES), lambda i, j: (i % nt, 0)),
                  pl.BlockSpec((tm, LANES), lambda i, j: (i % nt, 0))],
        out_specs=pl.BlockSpec((tm, heads * MLA_SLOT), lambda i, j: (i, j)),
        out_shape=jax.ShapeDtypeStruct((m, groups * heads * MLA_SLOT), BF16),
        scratch_shapes=[pltpu.VMEM((tm, k), BF16)],
        compiler_params=_params(("parallel", "arbitrary"), blocks, _nbytes((tm, k), BF16),
                                2 * _nbytes((tm, wn), F32)),
        name=name,
    )(p, gain, w, cos, sin)


def _mlakv_kernel(c_ref, g_ref, w_ref, out_ref, cn_scr):
    @pl.when(pl.program_id(1) == 0)
    def _():
        cn_scr[...] = _rms(c_ref[...].astype(F32), g_ref[...]).astype(BF16)

    out_ref[...] = jnp.dot(cn_scr[...], w_ref[...], preferred_element_type=F32).astype(out_ref.dtype)


def _mlakv(p, col_block, gain, w, *, tm, tn, name):
    m = p.shape[0]
    k, n = w.shape
    blocks = _nbytes((tm, k), BF16) + _nbytes((k, tn), BF16) + _nbytes((tm, tn), BF16)
    return pl.pallas_call(
        _mlakv_kernel,
        grid=(m // tm, n // tn),
        in_specs=[pl.BlockSpec((tm, k), lambda i, j: (i, col_block)),
                  pl.BlockSpec((1, k), lambda i, j: (0, 0)),
                  pl.BlockSpec((k, tn), lambda i, j: (0, j))],
        out_specs=pl.BlockSpec((tm, tn), lambda i, j: (i, j)),
        out_shape=jax.ShapeDtypeStruct((m, n), BF16),
        scratch_shapes=[pltpu.VMEM((tm, k), BF16)],
        compiler_params=_params(("parallel", "arbitrary"), blocks, _nbytes((tm, k), BF16),
                                2 * _nbytes((tm, tn), F32)),
        name=name,
    )(p, gain, w)


def _dot_nt(a, b):
    return lax.dot_general(a, b, (((1,), (1,)), ((), ())), preferred_element_type=F32)


def _transpose_bf16(x):
    return x.astype(F32).T.astype(BF16)


def _softmax_first_t(s_t, v_t):
    m = jnp.max(s_t, axis=0, keepdims=True)
    p = jnp.exp2(s_t - m)
    return m, jnp.sum(p, axis=0, keepdims=True), jnp.dot(v_t, p.astype(BF16), preferred_element_type=F32)


def _softmax_next_t(s_t, shift, v_t, m, l, acc):
    m_new = jnp.maximum(m, jnp.max(s_t, axis=0, keepdims=True) + shift)
    alpha = jnp.exp2(m - m_new)
    p = jnp.exp2(s_t - (m_new - shift))
    l = alpha * l + jnp.sum(p, axis=0, keepdims=True)
    acc = alpha * acc + jnp.dot(v_t, p.astype(BF16), preferred_element_type=F32)
    return m_new, l, acc


def _meta_mask_t(s_t):
    row = lax.broadcasted_iota(jnp.int32, s_t.shape, 0)
    return jnp.where(row < N_META, s_t, NEG_BIG)


def _diff_attn_kernel(q_ref, k_ref, v_ref, km_ref, vm_ref, slope_ref, lq1_ref, lk1_ref, lq2_ref,
                      lk2_ref, subln_ref, out_ref, vt_scr, bias_scr, *, tk, lam_init):
    tq = q_ref.shape[0]
    ratio = tk // tq
    nk = k_ref.shape[0] // tk
    hd = DIFF_HEAD_DIM
    qi = pl.program_id(2)
    slope2 = slope_ref[...] * LOG2E

    @pl.when(qi == 0)
    def _():
        vt_scr[:, :META_PAD] = _transpose_bf16(vm_ref[...])
        for c in range(nk):
            vt_scr[:, META_PAD + c * tk:META_PAD + (c + 1) * tk] = _transpose_bf16(
                v_ref[c * tk:(c + 1) * tk, :])
        d = (lax.broadcasted_iota(jnp.int32, (tk, tq), 1)
             - lax.broadcasted_iota(jnp.int32, (tk, tq), 0)).astype(F32) * slope2
        bias_scr[0] = -d
        bias_scr[1] = d
        for o in range(ratio):
            bias_scr[2 + o] = -jnp.abs(d + (o * tq) * slope2)

    q1 = q_ref[:, :hd]
    q2 = q_ref[:, hd:]

    def qk(j):
        kb = k_ref[j * tk:(j + 1) * tk, :]
        return _dot_nt(kb[:, :hd], q1), _dot_nt(kb[:, hd:], q2)

    vtm = vt_scr[:, :META_PAD]
    st1 = _softmax_first_t(_meta_mask_t(_dot_nt(km_ref[:, :hd], q1)), vtm)
    st2 = _softmax_first_t(_meta_mask_t(_dot_nt(km_ref[:, hd:], q2)), vtm)

    s1, s2 = qk(0)
    for j in range(nk):
        nxt = qk(j + 1) if j + 1 < nk else None
        off = qi - j * ratio
        bias = bias_scr[jnp.where(off < 0, 1, jnp.where(off >= ratio, 0, 2 + off))]
        gap = jnp.where(off < 0, j * tk - qi * tq, jnp.where(off >= ratio, qi * tq - j * tk, 0))
        shift = -gap.astype(F32) * slope2
        vt = vt_scr[:, META_PAD + j * tk:META_PAD + (j + 1) * tk]
        st1 = _softmax_next_t(s1 + bias, shift, vt, *st1)
        st2 = _softmax_next_t(s2 + bias, shift, vt, *st2)
        if nxt is not None:
            s1, s2 = nxt

    lam = (jnp.exp(jnp.sum(lq1_ref[...] * lk1_ref[...], axis=-1, keepdims=True))
           - jnp.exp(jnp.sum(lq2_ref[...] * lk2_ref[...], axis=-1, keepdims=True)) + lam_init)
    o_t = st1[2] / st1[1] - lam * (st2[2] / st2[1])
    out_ref[...] = (_rms(o_t.T, subln_ref[...]) * (1.0 - lam_init)).astype(out_ref.dtype)


def _diff_attn(p, pm, slopes, lq1, lk1, lq2, lk2, subln, *, batch, seq, tq, tk, lam_init, name):
    assert tk % tq == 0 and seq % tk == 0
    m = p.shape[0]
    hw = 2 * DIFF_HEAD_DIM
    nq = seq // tq
    koff = DIFF_HEADS
    voff = 2 * DIFF_HEADS
    kern = functools.partial(_diff_attn_kernel, tk=tk, lam_init=lam_init)
    n_bias = 2 + tk // tq
    vec = pl.BlockSpec((1, DIFF_HEAD_DIM), lambda b, h, i: (0, 0))
    blocks = (2 * _nbytes((tq, hw), BF16) + 2 * _nbytes((seq, hw), BF16)
              + 2 * _nbytes((META_PAD, hw), BF16))
    scratch = _nbytes((hw, seq + META_PAD), BF16) + n_bias * _nbytes((tk, tq), F32)
    return pl.pallas_call(
        kern,
        grid=(batch, DIFF_HEADS, nq),
        in_specs=[pl.BlockSpec((tq, hw), lambda b, h, i: (b * nq + i, h)),
                  pl.BlockSpec((seq, hw), lambda b, h, i: (b, koff + h)),
                  pl.BlockSpec((seq, hw), lambda b, h, i: (b, voff + h)),
                  pl.BlockSpec((META_PAD, hw), lambda b, h, i: (0, koff + h)),
                  pl.BlockSpec((META_PAD, hw), lambda b, h, i: (0, voff + h)),
                  pl.BlockSpec((None, 1, 1), lambda b, h, i: (h, 0, 0)),
                  vec, vec, vec, vec,
                  pl.BlockSpec((1, hw), lambda b, h, i: (0, 0))],
        out_specs=pl.BlockSpec((tq, hw), lambda b, h, i: (b * nq + i, h)),
        out_shape=jax.ShapeDtypeStruct((m, DIFF_HEADS * hw), BF16),
        scratch_shapes=[pltpu.VMEM((hw, seq + META_PAD), BF16),
                        pltpu.VMEM((n_bias, tk, tq), F32)],
        compiler_params=_params(("parallel", "parallel", "arbitrary"), blocks, scratch,
                                12 * _nbytes((tk, tq), F32)),
        name=name,
    )(p, p, p, pm, pm, slopes, lq1, lk1, lq2, lk2, subln)


def _mla_attn_kernel(q_ref, kv_ref, kpe_ref, kvm_ref, kpem_ref, out_ref, k_scr, vt_scr, *, tk):
    nk = kv_ref.shape[0] // tk

    @pl.when(pl.program_id(2) == 0)
    def _():
        k_scr[:META_PAD, :NOPE_D] = kvm_ref[:, :NOPE_D]
        k_scr[:META_PAD, NOPE_D:] = kpem_ref[...]
        k_scr[META_PAD:, :NOPE_D] = kv_ref[:, :NOPE_D]
        k_scr[META_PAD:, NOPE_D:] = kpe_ref[...]
        vt_scr[:, :META_PAD] = _transpose_bf16(kvm_ref[:, NOPE_D:])
        for c in range(nk):
            vt_scr[:, META_PAD + c * tk:META_PAD + (c + 1) * tk] = _transpose_bf16(
                kv_ref[c * tk:(c + 1) * tk, NOPE_D:])

    q = q_ref[...]

    def qk(j):
        return _dot_nt(k_scr[META_PAD + j * tk:META_PAD + (j + 1) * tk, :], q)

    st = _softmax_first_t(_meta_mask_t(_dot_nt(k_scr[:META_PAD, :], q)), vt_scr[:, :META_PAD])

    s = qk(0)
    for j in range(nk):
        nxt = qk(j + 1) if j + 1 < nk else None
        st = _softmax_next_t(s, 0.0, vt_scr[:, META_PAD + j * tk:META_PAD + (j + 1) * tk], *st)
        s = nxt

    out_ref[...] = (st[2] / st[1]).T.astype(out_ref.dtype)


def _mla_attn(q, kv, kpe, kvm, kpem, *, batch, seq, tq, tk, name):
    mrows = q.shape[0]
    nq = seq // tq
    kern = functools.partial(_mla_attn_kernel, tk=tk)
    blocks = (_nbytes((tq, MLA_SLOT), BF16) + _nbytes((seq, MLA_SLOT), BF16)
              + _nbytes((seq, LANES), BF16) + _nbytes((META_PAD, MLA_SLOT + LANES), BF16)
              + _nbytes((tq, MLA_V_D), BF16))
    scratch = _nbytes((seq + META_PAD, MLA_SLOT), BF16) + _nbytes((MLA_V_D, seq + META_PAD), BF16)
    return pl.pallas_call(
        kern,
        grid=(batch, MLA_HEADS, nq),
        in_specs=[pl.BlockSpec((tq, MLA_SLOT), lambda b, h, i: (b * nq + i, h)),
                  pl.BlockSpec((seq, MLA_SLOT), lambda b, h, i: (b, h)),
                  pl.BlockSpec((seq, LANES), lambda b, h, i: (b, 0)),
                  pl.BlockSpec((META_PAD, MLA_SLOT), lambda b, h, i: (0, h)),
                  pl.BlockSpec((META_PAD, LANES), lambda b, h, i: (0, 0))],
        out_specs=pl.BlockSpec((tq, MLA_V_D), lambda b, h, i: (b * nq + i, h)),
        out_shape=jax.ShapeDtypeStruct((mrows, MLA_HEADS * MLA_V_D), BF16),
        scratch_shapes=[pltpu.VMEM((seq + META_PAD, MLA_SLOT), BF16),
                        pltpu.VMEM((MLA_V_D, seq + META_PAD), BF16)],
        compiler_params=_params(("parallel", "parallel", "arbitrary"), blocks, scratch,
                                6 * _nbytes((tk, tq), F32)),
        name=name,
    )(q, kv, kpe, kvm, kpem)


def _merge_kernel(od_ref, om_ref, wa_ref, wb_ref, gd_ref, gm_ref, side_ref, out_ref, side_out_ref):
    ya = jnp.dot(od_ref[...], wa_ref[...], preferred_element_type=F32)
    yb = jnp.dot(om_ref[...], wb_ref[...], preferred_element_type=F32)
    out_ref[...] = (gd_ref[...].astype(F32) * ya + gm_ref[...].astype(F32) * yb).astype(out_ref.dtype)
    _side_cast_run([side_ref], [side_out_ref])


def _merge(od, om, wa, wb, gates, side, *, tm, tn, name):
    m, k = od.shape
    n = wa.shape[1]
    nj = n // tn
    grid = (m // tm, nj)
    side_in, side_out, side_shape, side_bytes = _side_cast_specs(side, grid)
    blocks = (2 * _nbytes((tm, k), BF16) + 2 * _nbytes((k, tn), BF16) + 3 * _nbytes((tm, tn), BF16)
              + side_bytes)
    return pl.pallas_call(
        _merge_kernel,
        grid=grid,
        in_specs=[pl.BlockSpec((tm, k), lambda i, j: (i, 0)),
                  pl.BlockSpec((tm, k), lambda i, j: (i, 0)),
                  pl.BlockSpec((k, tn), lambda i, j: (0, j)),
                  pl.BlockSpec((k, tn), lambda i, j: (0, j)),
                  pl.BlockSpec((tm, tn), lambda i, j: (i, j)),
                  pl.BlockSpec((tm, tn), lambda i, j: (i, nj + j)),
                  side_in],
        out_specs=[pl.BlockSpec((tm, tn), lambda i, j: (i, j)), side_out],
        out_shape=[jax.ShapeDtypeStruct((m, n), BF16), side_shape],
        compiler_params=_params(("parallel", "arbitrary"), blocks, 0, 4 * _nbytes((tm, tn), F32)),
        name=name,
    )(od, om, wa, wb, gates, gates, side.w)


def _rope_tables(t):
    inv_freq = 1.0 / (ROPE_THETA ** (jnp.arange(0, ROPE_D, 2, dtype=F32) / ROPE_D))
    ang = jnp.arange(t).astype(F32)[:, None] * inv_freq[None, :]
    pad = jnp.zeros((t, LANES - ROPE_D), F32)
    cos = jnp.concatenate([jnp.cos(ang), jnp.cos(ang), pad], axis=-1)
    sin = jnp.concatenate([jnp.sin(ang), jnp.sin(ang), pad], axis=-1)
    return cos, sin


def _rotate_cols(w):
    half = ROPE_D // 2
    return jnp.concatenate([-w[..., half:], w[..., :half]], axis=-1)


def kernel(x, meta_tokens, ffn1_norm, ffn1_w_gate, ffn1_w_up, ffn1_w_down, mix_norm, w_in, diff_lambda_q1, diff_lambda_k1, diff_lambda_q2, diff_lambda_k2, diff_subln, mla_q_norm, mla_w_uq, mla_kv_norm, mla_w_ukv, w_gate, b_gate, w_branch_diff, w_branch_mla, w_out, ffn2_norm, ffn2_w_gate, ffn2_w_up, ffn2_w_down, final_norm):
    batch, seq, d = x.shape
    depth = ffn1_norm.shape[0]
    m = batch * seq
    dq_w = DIFF_HEADS * 2 * DIFF_HEAD_DIM
    dv_w = DIFF_HEADS * DIFF_V_DIM
    main_w = 2 * dq_w + dv_w + Q_LORA + KV_LORA
    cq_block = (2 * dq_w + dv_w) // Q_LORA
    ckv_block = (2 * dq_w + dv_w + Q_LORA) // KV_LORA
    proj_steps = (m // ROWS) * (main_w // PROJ_COLS)
    gate_steps = (m // ROWS) * (2 * d // GATE_COLS)
    merge_steps = (m // ROWS) * (d // MERGE_COLS)

    cos, sin = _rope_tables(N_META + seq)
    cos_m, sin_m, cos_r, sin_r = cos[:N_META], sin[:N_META], cos[N_META:], sin[N_META:]
    slopes = jnp.asarray([2.0 ** (-8.0 * (h + 1) / DIFF_HEADS) for h in range(DIFF_HEADS)],
                         F32).reshape(DIFF_HEADS, 1, 1)
    col_scale = jnp.concatenate([jnp.full((1, dq_w), DIFF_HEAD_DIM ** -0.5 * LOG2E, F32),
                                 jnp.ones((1, main_w - dq_w), F32)], axis=-1)
    row = lambda v: v.reshape(1, -1).astype(F32)

    xs = x.reshape(m, d)
    xm = meta_tokens.astype(x.dtype)

    for l in range(depth):
        lam_init = 0.8 - 0.6 * math.exp(-0.3 * l)
        w_main = _cast_cols(w_in, l, main_w, tr=min(CAST_ROWS, d), tc=CAST_COLS, name="cast_w_in")
        w_kr = _tail_cols(w_in, l, main_w, tr=min(CAST_ROWS, d), name="w_in_rope_cols")
        zpad = jnp.zeros((d, LANES - ROPE_D), F32)
        w_kpe = jnp.concatenate([w_kr, zpad, _rotate_cols(w_kr), zpad], axis=-1).astype(BF16)

        wuq = mla_w_uq[l].reshape(Q_LORA, MLA_HEADS, NOPE_D + ROPE_D)
        w_q = jnp.concatenate([wuq, _rotate_cols(wuq[..., NOPE_D:])], axis=-1)
        w_q = w_q.reshape(Q_LORA, MLA_HEADS * MLA_SLOT).astype(BF16)
        w_ukv = mla_w_ukv[l].astype(BF16)
        w_pa = w_branch_diff[l].astype(BF16)
        w_pb = w_branch_mla[l].astype(BF16)
        w_o = w_out[l].astype(BF16)

        xm, wg1, wu1, wd1 = _ffn_meta(xm, row(ffn1_norm[l]), ffn1_w_gate, ffn1_w_up, ffn1_w_down, l,
                                      tf=FFN_CHUNK, name="ffn1_meta")
        xs = _ffn(xs, row(ffn1_norm[l]), wg1, wu1, wd1, tm=FFN_ROWS, tf=FFN_CHUNK, name="ffn1")

        hs = _norm(xs, row(mix_norm[l]), tm=FFN_ROWS, name="mix_norm")
        hm = _norm(xm, row(mix_norm[l]), tm=N_META, name="mix_norm_meta")

        ps, wg2, w_g = _mm(hs, w_main, tm=ROWS, tn=PROJ_COLS, out_dtype=BF16, scale=col_scale,
                           name="proj", sides=(_side(ffn2_w_gate, l, proj_steps),
                                               _side(w_gate, l, proj_steps)))
        pm = _mm(hm, w_main, tm=N_META, tn=PROJ_COLS, out_dtype=BF16, scale=col_scale, name="proj_meta")
        gates, wu2 = _mm(hs, w_g, tm=ROWS, tn=GATE_COLS, out_dtype=BF16, bias=row(b_gate[l]),
                         name="gates", sides=(_side(ffn2_w_up, l, gate_steps),))
        kpe_s = _kpe(hs, w_kpe, cos_r, sin_r, tm=ROWS, name="kpe")
        kpe_m = _kpe(hm, w_kpe, cos_m, sin_m, tm=N_META, name="kpe_meta")

        q_mla = _mlaq(ps, cq_block, row(mla_q_norm[l]), w_q, cos_r, sin_r, tm=ROWS, heads=MLAQ_HEADS,
                      scale=(NOPE_D + ROPE_D) ** -0.5 * LOG2E, name="mla_q")
        kv_s = _mlakv(ps, ckv_block, row(mla_kv_norm[l]), w_ukv, tm=ROWS, tn=MLAKV_COLS, name="mla_kv")
        kv_m = _mlakv(pm, ckv_block, row(mla_kv_norm[l]), w_ukv, tm=N_META, tn=MLAKV_COLS,
                      name="mla_kv_meta")

        pad_rows = lambda a: jnp.pad(a, ((0, META_PAD - N_META), (0, 0)))
        o_diff = _diff_attn(ps, pad_rows(pm), slopes, row(diff_lambda_q1[l]), row(diff_lambda_k1[l]),
                            row(diff_lambda_q2[l]), row(diff_lambda_k2[l]), row(diff_subln[l]),
                            batch=batch, seq=seq, tq=DIFF_TQ, tk=DIFF_TK, lam_init=lam_init,
                            name="diff_attn")
        o_mla = _mla_attn(q_mla, kv_s, kpe_s, pad_rows(kv_m), pad_rows(kpe_m),
                          batch=batch, seq=seq, tq=MLA_TQ, tk=MLA_TK, name="mla_attn")

        merged, wd2 = _merge(o_diff, o_mla, w_pa, w_pb, gates,
                             _side(ffn2_w_down, l, merge_steps, bc=MERGE_COLS),
                             tm=ROWS, tn=MERGE_COLS, name="merge")
        xs = _mm(merged, w_o, tm=ROWS, tn=OUT_COLS, out_dtype=F32, res=xs, name="out_proj")

        last = l == depth - 1
        xs = _ffn(xs, row(ffn2_norm[l]), wg2, wu2, wd2, tm=FFN_ROWS, tf=FFN_CHUNK,
                  final_gain=row(final_norm) if last else None, name="ffn2")
        if not last:
            raise NotImplementedError("DEPTH > 1 needs the meta-row query path")

    return xs.reshape(batch, seq, d)
```

```python
import functools
import math
from typing import NamedTuple

import numpy as np
import jax
import jax.numpy as jnp
from jax import lax
from jax.experimental import pallas as pl
from jax.experimental.pallas import tpu as pltpu

N_META = 16
EPS = 1e-6
DIFF_HEADS = 8
DIFF_HEAD_DIM = 128
DIFF_V_DIM = 256
MLA_HEADS = 16
Q_LORA = 1024
KV_LORA = 512
NOPE_D = 128
ROPE_D = 64
MLA_V_D = 128
ROPE_THETA = 10000.0

LANES = 128
MLA_SLOT = 256
META_PAD = 128
VMEM_CAP = 60000 * 1024
VMEM_PHYSICAL = 64 * 2**20
VMEM_RESERVE = 2 * 2**20
VMEM_FLOOR = 16 * 2**20
NEG_BIG = -1e30
LOG2E = math.log2(math.e)

ROWS = 1024
PROJ_COLS = 768
GATE_COLS = 1024
MERGE_COLS = 512
OUT_COLS = 512
FFN_ROWS = 512
FFN_CHUNK = 256
CAST_ROWS, CAST_COLS = 1024, 1536
MLAQ_HEADS = 8
MLAKV_COLS = 2048
DIFF_TQ, DIFF_TK = 256, 512
MLA_TQ, MLA_TK = 2048, 512

F32 = jnp.float32
BF16 = jnp.bfloat16


def _nbytes(shape, dtype):
    return int(np.prod(shape)) * jnp.dtype(dtype).itemsize


def _params(semantics, block_bytes, scratch_bytes, temp_bytes):
    need = 2 * block_bytes + scratch_bytes + temp_bytes
    return pltpu.CompilerParams(dimension_semantics=semantics,
                                vmem_limit_bytes=int(min(VMEM_CAP, max(need, VMEM_FLOOR))))


def _rms(x, gain):
    return x * lax.rsqrt(jnp.mean(x * x, axis=-1, keepdims=True) + EPS) * gain


def _ffn_kernel(x_ref, g_ref, wga_ref, wua_ref, wda_ref, wgb_ref, wub_ref, wdb_ref, *rest,
                final_norm, nchunks):
    out_ref, h_scr = rest[-2:]
    f = pl.program_id(1)
    last = pl.num_programs(1) - 1

    def act(wg_ref, wu_ref):
        h = h_scr[...]
        g = jnp.dot(h, wg_ref[...], preferred_element_type=F32)
        u = jnp.dot(h, wu_ref[...], preferred_element_type=F32)
        return (0.5 * (g * jax.nn.sigmoid(g)) * u).astype(BF16)

    def both():
        a = act(wga_ref, wua_ref)
        b = act(wgb_ref, wub_ref)
        return (jnp.dot(a, wda_ref[...], preferred_element_type=F32)
                + jnp.dot(b, wdb_ref[...], preferred_element_type=F32))

    @pl.when(f == 0)
    def _():
        x = x_ref[...]
        h_scr[...] = _rms(x, g_ref[...]).astype(BF16)
        out_ref[...] = x + both()

    @pl.when((f > 0) & (f < last) if nchunks % 2 else f > 0)
    def _():
        out_ref[...] += both()

    if nchunks % 2:
        @pl.when((f == last) & (f > 0))
        def _():
            out_ref[...] += jnp.dot(act(wga_ref, wua_ref), wda_ref[...], preferred_element_type=F32)

    if final_norm:
        gf_ref = rest[0]

        @pl.when(f == last)
        def _():
            out_ref[...] = _rms(out_ref[...], gf_ref[...])


def _ffn(x, gain, wg, wu, wd, *, tm, tf, name, final_gain=None):
    m, d = x.shape
    nchunks = wg.shape[1] // tf
    assert nchunks != 1, "the first grid step always handles two chunks"
    kern = functools.partial(_ffn_kernel, final_norm=final_gain is not None, nchunks=nchunks)
    vec = pl.BlockSpec((1, d), lambda i, f: (0, 0))
    finals = [] if final_gain is None else [final_gain]
    col_a = lambda i, f: (0, 2 * f)
    col_b = lambda i, f: (0, jnp.minimum(2 * f + 1, nchunks - 1))
    row_a = lambda i, f: (2 * f, 0)
    row_b = lambda i, f: (jnp.minimum(2 * f + 1, nchunks - 1), 0)
    weights = 2 * (2 * _nbytes((d, tf), BF16) + _nbytes((tf, d), BF16))
    need = (_nbytes((tm, d), F32) + 2 * _nbytes((tm, d), F32) + 2 * weights
            + _nbytes((tm, d), BF16) + 6 * _nbytes((tm, tf), F32) + _nbytes((tm, d), F32))
    return pl.pallas_call(
        kern,
        grid=(m // tm, (nchunks + 1) // 2),
        in_specs=[
            pl.BlockSpec((tm, d), lambda i, f: (i, 0), pipeline_mode=pl.Buffered(1)),
            vec,
            pl.BlockSpec((d, tf), col_a), pl.BlockSpec((d, tf), col_a), pl.BlockSpec((tf, d), row_a),
            pl.BlockSpec((d, tf), col_b), pl.BlockSpec((d, tf), col_b), pl.BlockSpec((tf, d), row_b),
        ] + [vec] * len(finals),
        out_specs=pl.BlockSpec((tm, d), lambda i, f: (i, 0)),
        out_shape=jax.ShapeDtypeStruct((m, d), F32),
        scratch_shapes=[pltpu.VMEM((tm, d), BF16)],
        compiler_params=pltpu.CompilerParams(
            dimension_semantics=("parallel", "arbitrary"),
            vmem_limit_bytes=int(min(VMEM_PHYSICAL - VMEM_RESERVE, max(need, VMEM_FLOOR)))),
        name=name,
    )(x, gain, wg, wu, wd, wg, wu, wd, *finals)


def _ffn_meta_kernel(x_ref, g_ref, wg_ref, wu_ref, wd_ref, out_ref, wg_out, wu_out, wd_out, h_scr):
    @pl.when(pl.program_id(0) == 0)
    def _():
        x = x_ref[...]
        h_scr[...] = _rms(x, g_ref[...]).astype(BF16)
        out_ref[...] = x

    wg = wg_ref[...].astype(BF16)
    wu = wu_ref[...].astype(BF16)
    wd = wd_ref[...].astype(BF16)
    wg_out[...] = wg
    wu_out[...] = wu
    wd_out[...] = wd
    h = h_scr[...]
    g = jnp.dot(h, wg, preferred_element_type=F32)
    u = jnp.dot(h, wu, preferred_element_type=F32)
    a = (0.5 * (g * jax.nn.sigmoid(g)) * u).astype(BF16)
    out_ref[...] += jnp.dot(a, wd, preferred_element_type=F32)


def _ffn_meta(x, gain, wg, wu, wd, l, *, tf, name):
    m, d = x.shape
    ff = wg.shape[2]
    blocks = (2 * _nbytes((m, d), F32) + 3 * _nbytes((d, tf), F32) + 3 * _nbytes((d, tf), BF16))
    return pl.pallas_call(
        _ffn_meta_kernel,
        grid=(ff // tf,),
        in_specs=[pl.BlockSpec((m, d), lambda f: (0, 0)),
                  pl.BlockSpec((1, d), lambda f: (0, 0)),
                  pl.BlockSpec((None, d, tf), lambda f: (l, 0, f)),
                  pl.BlockSpec((None, d, tf), lambda f: (l, 0, f)),
                  pl.BlockSpec((None, tf, d), lambda f: (l, f, 0))],
        out_specs=[pl.BlockSpec((m, d), lambda f: (0, 0)),
                   pl.BlockSpec((d, tf), lambda f: (0, f)),
                   pl.BlockSpec((d, tf), lambda f: (0, f)),
                   pl.BlockSpec((tf, d), lambda f: (f, 0))],
        out_shape=[jax.ShapeDtypeStruct((m, d), F32),
                   jax.ShapeDtypeStruct((d, ff), BF16),
                   jax.ShapeDtypeStruct((d, ff), BF16),
                   jax.ShapeDtypeStruct((ff, d), BF16)],
        scratch_shapes=[pltpu.VMEM((m, d), BF16)],
        compiler_params=_params(("arbitrary",), blocks, _nbytes((m, d), BF16),
                                3 * _nbytes((d, tf), BF16)),
        name=name,
    )(x, gain, wg, wu, wd)


def _norm_kernel(x_ref, g_ref, out_ref):
    out_ref[...] = _rms(x_ref[...], g_ref[...]).astype(out_ref.dtype)


def _norm(x, gain, *, tm, name):
    m, d = x.shape
    blocks = _nbytes((tm, d), F32) + _nbytes((tm, d), BF16)
    return pl.pallas_call(
        _norm_kernel,
        grid=(m // tm,),
        in_specs=[pl.BlockSpec((tm, d), lambda i: (i, 0)),
                  pl.BlockSpec((1, d), lambda i: (0, 0))],
        out_specs=pl.BlockSpec((tm, d), lambda i: (i, 0)),
        out_shape=jax.ShapeDtypeStruct((m, d), BF16),
        compiler_params=_params(("parallel",), blocks, 0, 2 * _nbytes((tm, d), F32)),
        name=name,
    )(x, gain)


def _cast_kernel(w_ref, out_ref):
    out_ref[...] = w_ref[...].astype(out_ref.dtype)


def _cast_cols(w, l, cols, *, tr, tc, name):
    rows = w.shape[1]
    blocks = _nbytes((tr, tc), w.dtype) + _nbytes((tr, tc), BF16)
    return pl.pallas_call(
        _cast_kernel,
        grid=(rows // tr, cols // tc),
        in_specs=[pl.BlockSpec((None, tr, tc), lambda i, j: (l, i, j))],
        out_specs=pl.BlockSpec((tr, tc), lambda i, j: (i, j)),
        out_shape=jax.ShapeDtypeStruct((rows, cols), BF16),
        compiler_params=_params(("parallel", "parallel"), blocks, 0, 0),
        name=name,
    )(w)


def _tail_cols(w, l, start, *, tr, name):
    rows, cols = w.shape[1:]
    assert start % LANES == 0 and cols - start <= LANES
    blocks = 2 * _nbytes((tr, LANES), F32)
    out = pl.pallas_call(
        _cast_kernel,
        grid=(rows // tr,),
        in_specs=[pl.BlockSpec((None, tr, LANES), lambda i: (l, i, start // LANES))],
        out_specs=pl.BlockSpec((tr, LANES), lambda i: (i, 0)),
        out_shape=jax.ShapeDtypeStruct((rows, LANES), F32),
        compiler_params=_params(("parallel",), blocks, 0, 0),
        name=name,
    )(w)
    return out[:, :cols - start]


class SideCast(NamedTuple):
    w: jax.Array
    l: int
    br: int
    bc: int


def _side(w, l, steps, bc=None):
    rows, cols = w.shape[1:]
    bc = cols if bc is None else bc
    sublanes = 16
    br = next(r for r in range(sublanes, rows + 1, sublanes)
              if rows % r == 0 and (rows // r) * (cols // bc) <= steps)
    return SideCast(w, l, br, bc)


def _side_cast_specs(side, grid):
    rows, cols = side.w.shape[1:]
    ncols = cols // side.bc
    nblocks = (rows // side.br) * ncols
    assert rows % side.br == 0 and cols % side.bc == 0 and nblocks <= math.prod(grid), (side.w.shape, grid)

    def block(*idx):
        step = 0
        for i, n in zip(idx, grid):
            step = step * n + i
        t = jnp.minimum(step, nblocks - 1)
        return t // ncols, t % ncols

    in_spec = pl.BlockSpec((None, side.br, side.bc), lambda *idx: (side.l, *block(*idx)))
    out_spec = pl.BlockSpec((side.br, side.bc), block)
    nbytes = _nbytes((side.br, side.bc), F32) + _nbytes((side.br, side.bc), BF16)
    return in_spec, out_spec, jax.ShapeDtypeStruct((rows, cols), BF16), nbytes


def _side_cast_run(side_in_refs, side_out_refs):
    for src, dst in zip(side_in_refs, side_out_refs):
        dst[...] = src[...].astype(dst.dtype)


def _mm_kernel(*refs, has_scale, has_bias, has_res, n_side):
    n_in = 2 + has_scale + has_bias + has_res + n_side
    a_ref, w_ref = refs[0], refs[1]
    extra = list(refs[2:n_in - n_side])
    out_ref = refs[n_in]
    y = jnp.dot(a_ref[...], w_ref[...], preferred_element_type=F32)
    if has_scale:
        y = y * extra.pop(0)[...]
    if has_bias:
        y = jax.nn.sigmoid(y + extra.pop(0)[...])
    if has_res:
        y = y + extra.pop(0)[...]
    out_ref[...] = y.astype(out_ref.dtype)
    _side_cast_run(refs[n_in - n_side:n_in], refs[n_in + 1:])


def _mm(a, w, *, tm, tn, out_dtype, name, scale=None, bias=None, res=None, sides=()):
    m, k = a.shape
    n = w.shape[1]
    grid = (m // tm, n // tn)
    ins = [a, w]
    in_specs = [pl.BlockSpec((tm, k), lambda i, j: (i, 0)),
                pl.BlockSpec((k, tn), lambda i, j: (0, j))]
    blocks = _nbytes((tm, k), a.dtype) + _nbytes((k, tn), w.dtype) + _nbytes((tm, tn), out_dtype)
    for vec in (scale, bias):
        if vec is not None:
            ins.append(vec)
            in_specs.append(pl.BlockSpec((1, tn), lambda i, j: (0, j)))
    if res is not None:
        ins.append(res)
        in_specs.append(pl.BlockSpec((tm, tn), lambda i, j: (i, j)))
        blocks += _nbytes((tm, tn), res.dtype)
    out_specs = [pl.BlockSpec((tm, tn), lambda i, j: (i, j))]
    out_shape = [jax.ShapeDtypeStruct((m, n), out_dtype)]
    for side in sides:
        in_spec, out_spec, shape, nbytes = _side_cast_specs(side, grid)
        ins.append(side.w)
        in_specs.append(in_spec)
        out_specs.append(out_spec)
        out_shape.append(shape)
        blocks += nbytes
    kern = functools.partial(_mm_kernel, has_scale=scale is not None, has_bias=bias is not None,
                             has_res=res is not None, n_side=len(sides))
    outs = pl.pallas_call(
        kern,
        grid=grid,
        in_specs=in_specs,
        out_specs=out_specs,
        out_shape=out_shape,
        compiler_params=_params(("parallel", "arbitrary"), blocks, 0, 4 * _nbytes((tm, tn), F32)),
        name=name,
    )(*ins)
    return outs if sides else outs[0]


def _kpe_kernel(h_ref, w_ref, cos_ref, sin_ref, out_ref):
    y = jnp.dot(h_ref[...], w_ref[...], preferred_element_type=F32)
    out_ref[...] = (y[:, :LANES] * cos_ref[...] + y[:, LANES:] * sin_ref[...]).astype(out_ref.dtype)


def _kpe(h, w, cos, sin, *, tm, name):
    m, d = h.shape
    nt = cos.shape[0] // tm
    blocks = _nbytes((tm, d), BF16) + _nbytes((d, 2 * LANES), BF16) + 3 * _nbytes((tm, LANES), F32)
    return pl.pallas_call(
        _kpe_kernel,
        grid=(m // tm,),
        in_specs=[pl.BlockSpec((tm, d), lambda i: (i, 0)),
                  pl.BlockSpec((d, 2 * LANES), lambda i: (0, 0)),
                  pl.BlockSpec((tm, LANES), lambda i: (i % nt, 0)),
                  pl.BlockSpec((tm, LANES), lambda i: (i % nt, 0))],
        out_specs=pl.BlockSpec((tm, LANES), lambda i: (i, 0)),
        out_shape=jax.ShapeDtypeStruct((m, LANES), BF16),
        compiler_params=_params(("parallel",), blocks, 0, _nbytes((tm, 2 * LANES), F32)),
        name=name,
    )(h, w, cos, sin)


def _mlaq_kernel(c_ref, g_ref, w_ref, cos_ref, sin_ref, out_ref, cn_scr, *, heads, scale):
    @pl.when(pl.program_id(1) == 0)
    def _():
        cn_scr[...] = _rms(c_ref[...].astype(F32), g_ref[...]).astype(BF16)

    y = jnp.dot(cn_scr[...], w_ref[...], preferred_element_type=F32)
    cos = cos_ref[...]
    sin = sin_ref[...]
    for h in range(heads):
        s0 = h * MLA_SLOT
        out_ref[:, s0:s0 + NOPE_D] = (y[:, s0:s0 + NOPE_D] * scale).astype(out_ref.dtype)
        hi = y[:, s0 + NOPE_D:s0 + MLA_SLOT]
        pe = hi * cos + pltpu.roll(hi, LANES // 2, axis=1) * sin
        out_ref[:, s0 + NOPE_D:s0 + MLA_SLOT] = (pe * scale).astype(out_ref.dtype)


def _mlaq(p, col_block, gain, w, cos, sin, *, tm, heads, scale, name):
    m = p.shape[0]
    k = gain.shape[1]
    wn = heads * MLA_SLOT
    groups = w.shape[1] // wn
    nt = cos.shape[0] // tm
    kern = functools.partial(_mlaq_kernel, heads=heads, scale=scale)
    blocks = (_nbytes((tm, k), BF16) + _nbytes((k, wn), BF16) + 2 * _nbytes((tm, LANES), F32)
              + _nbytes((tm, heads * MLA_SLOT), BF16))
    return pl.pallas_call(
        kern,
        grid=(m // tm, groups),
        in_specs=[pl.BlockSpec((tm, k), lambda i, j: (i, col_block)),
                  pl.BlockSpec((1, k), lambda i, j: (0, 0)),
                  pl.BlockSpec((k, wn), lambda i, j: (0, j)),
                  pl.BlockSpec((tm, LANES), lambda i, j: (i % nt, 0)),
                  pl.BlockSpec((tm, LANES), lambda i, j: (i % nt, 0))],
        out_specs=pl.BlockSpec((tm, heads * MLA_SLOT), lambda i, j: (i, j)),
        out_shape=jax.ShapeDtypeStruct((m, groups * heads * MLA_SLOT), BF16),
        scratch_shapes=[pltpu.VMEM((tm, k), BF16)],
        compiler_params=_params(("parallel", "arbitrary"), blocks, _nbytes((tm, k), BF16),
                                2 * _nbytes((tm, wn), F32)),
        name=name,
    )(p, gain, w, cos, sin)


def _mlakv_kernel(c_ref, g_ref, w_ref, out_ref, cn_scr):
    @pl.when(pl.program_id(1) == 0)
    def _():
        cn_scr[...] = _rms(c_ref[...].astype(F32), g_ref[...]).astype(BF16)

    out_ref[...] = jnp.dot(cn_scr[...], w_ref[...], preferred_element_type=F32).astype(out_ref.dtype)


def _mlakv(p, col_block, gain, w, *, tm, tn, name):
    m = p.shape[0]
    k, n = w.shape
    blocks = _nbytes((tm, k), BF16) + _nbytes((k, tn), BF16) + _nbytes((tm, tn), BF16)
    return pl.pallas_call(
        _mlakv_kernel,
        grid=(m // tm, n // tn),
        in_specs=[pl.BlockSpec((tm, k), lambda i, j: (i, col_block)),
                  pl.BlockSpec((1, k), lambda i, j: (0, 0)),
                  pl.BlockSpec((k, tn), lambda i, j: (0, j))],
        out_specs=pl.BlockSpec((tm, tn), lambda i, j: (i, j)),
        out_shape=jax.ShapeDtypeStruct((m, n), BF16),
        scratch_shapes=[pltpu.VMEM((tm, k), BF16)],
        compiler_params=_params(("parallel", "arbitrary"), blocks, _nbytes((tm, k), BF16),
                                2 * _nbytes((tm, tn), F32)),
        name=name,
    )(p, gain, w)


def _dot_nt(a, b):
    return lax.dot_general(a, b, (((1,), (1,)), ((), ())), preferred_element_type=F32)


def _transpose_bf16(x):
    return x.astype(F32).T.astype(BF16)


def _softmax_first_t(s_t, v_t):
    m = jnp.max(s_t, axis=0, keepdims=True)
    p = jnp.exp2(s_t - m)
    return m, jnp.sum(p, axis=0, keepdims=True), jnp.dot(v_t, p.astype(BF16), preferred_element_type=F32)


def _softmax_next_t(s_t, shift, v_t, m, l, acc):
    m_new = jnp.maximum(m, jnp.max(s_t, axis=0, keepdims=True) + shift)
    alpha = jnp.exp2(m - m_new)
    p = jnp.exp2(s_t - (m_new - shift))
    l = alpha * l + jnp.sum(p, axis=0, keepdims=True)
    acc = alpha * acc + jnp.dot(v_t, p.astype(BF16), preferred_element_type=F32)
    return m_new, l, acc


def _meta_mask_t(s_t):
    row = lax.broadcasted_iota(jnp.int32, s_t.shape, 0)
    return jnp.where(row < N_META, s_t, NEG_BIG)


def _diff_attn_kernel(q_ref, k_ref, v_ref, km_ref, vm_ref, slope_ref, lq1_ref, lk1_ref, lq2_ref,
                      lk2_ref, subln_ref, out_ref, vt_scr, bias_scr, *, tk, lam_init):
    tq = q_ref.shape[0]
    ratio = tk // tq
    nk = k_ref.shape[0] // tk
    hd = DIFF_HEAD_DIM
    qi = pl.program_id(2)
    slope2 = slope_ref[...] * LOG2E

    @pl.when(qi == 0)
    def _():
        vt_scr[:, :META_PAD] = _transpose_bf16(vm_ref[...])
        for c in range(nk):
            vt_scr[:, META_PAD + c * tk:META_PAD + (c + 1) * tk] = _transpose_bf16(
                v_ref[c * tk:(c + 1) * tk, :])
        d = (lax.broadcasted_iota(jnp.int32, (tk, tq), 1)
             - lax.broadcasted_iota(jnp.int32, (tk, tq), 0)).astype(F32) * slope2
        bias_scr[0] = -d
        bias_scr[1] = d
        for o in range(ratio):
            bias_scr[2 + o] = -jnp.abs(d + (o * tq) * slope2)

    q1 = q_ref[:, :hd]
    q2 = q_ref[:, hd:]

    def qk(j):
        kb = k_ref[j * tk:(j + 1) * tk, :]
        return _dot_nt(kb[:, :hd], q1), _dot_nt(kb[:, hd:], q2)

    vtm = vt_scr[:, :META_PAD]
    st1 = _softmax_first_t(_meta_mask_t(_dot_nt(km_ref[:, :hd], q1)), vtm)
    st2 = _softmax_first_t(_meta_mask_t(_dot_nt(km_ref[:, hd:], q2)), vtm)

    s1, s2 = qk(0)
    for j in range(nk):
        nxt = qk(j + 1) if j + 1 < nk else None
        off = qi - j * ratio
        bias = bias_scr[jnp.where(off < 0, 1, jnp.where(off >= ratio, 0, 2 + off))]
        gap = jnp.where(off < 0, j * tk - qi * tq, jnp.where(off >= ratio, qi * tq - j * tk, 0))
        shift = -gap.astype(F32) * slope2
        vt = vt_scr[:, META_PAD + j * tk:META_PAD + (j + 1) * tk]
        st1 = _softmax_next_t(s1 + bias, shift, vt, *st1)
        st2 = _softmax_next_t(s2 + bias, shift, vt, *st2)
        if nxt is not None:
            s1, s2 = nxt

    lam = (jnp.exp(jnp.sum(lq1_ref[...] * lk1_ref[...], axis=-1, keepdims=True))
           - jnp.exp(jnp.sum(lq2_ref[...] * lk2_ref[...], axis=-1, keepdims=True)) + lam_init)
    o_t = st1[2] / st1[1] - lam * (st2[2] / st2[1])
    out_ref[...] = (_rms(o_t.T, subln_ref[...]) * (1.0 - lam_init)).astype(out_ref.dtype)


def _diff_attn(p, pm, slopes, lq1, lk1, lq2, lk2, subln, *, batch, seq, tq, tk, lam_init, name):
    assert tk % tq == 0 and seq % tk == 0
    m = p.shape[0]
    hw = 2 * DIFF_HEAD_DIM
    nq = seq // tq
    koff = DIFF_HEADS
    voff = 2 * DIFF_HEADS
    kern = functools.partial(_diff_attn_kernel, tk=tk, lam_init=lam_init)
    n_bias = 2 + tk // tq
    vec = pl.BlockSpec((1, DIFF_HEAD_DIM), lambda b, h, i: (0, 0))
    blocks = (2 * _nbytes((tq, hw), BF16) + 2 * _nbytes((seq, hw), BF16)
              + 2 * _nbytes((META_PAD, hw), BF16))
    scratch = _nbytes((hw, seq + META_PAD), BF16) + n_bias * _nbytes((tk, tq), F32)
    return pl.pallas_call(
        kern,
        grid=(batch, DIFF_HEADS, nq),
        in_specs=[pl.BlockSpec((tq, hw), lambda b, h, i: (b * nq + i, h)),
                  pl.BlockSpec((seq, hw), lambda b, h, i: (b, koff + h)),
                  pl.BlockSpec((seq, hw), lambda b, h, i: (b, voff + h)),
                  pl.BlockSpec((META_PAD, hw), lambda b, h, i: (0, koff + h)),
                  pl.BlockSpec((META_PAD, hw), lambda b, h, i: (0, voff + h)),
                  pl.BlockSpec((None, 1, 1), lambda b, h, i: (h, 0, 0)),
                  vec, vec, vec, vec,
                  pl.BlockSpec((1, hw), lambda b, h, i: (0, 0))],
        out_specs=pl.BlockSpec((tq, hw), lambda b, h, i: (b * nq + i, h)),
        out_shape=jax.ShapeDtypeStruct((m, DIFF_HEADS * hw), BF16),
        scratch_shapes=[pltpu.VMEM((hw, seq + META_PAD), BF16),
                        pltpu.VMEM((n_bias, tk, tq), F32)],
        compiler_params=_params(("parallel", "parallel", "arbitrary"), blocks, scratch,
                                12 * _nbytes((tk, tq), F32)),
        name=name,
    )(p, p, p, pm, pm, slopes, lq1, lk1, lq2, lk2, subln)


def _mla_attn_kernel(q_ref, kv_ref, kpe_ref, kvm_ref, kpem_ref, out_ref, k_scr, vt_scr, *, tk):
    nk = kv_ref.shape[0] // tk

    @pl.when(pl.program_id(2) == 0)
    def _():
        k_scr[:META_PAD, :NOPE_D] = kvm_ref[:, :NOPE_D]
        k_scr[:META_PAD, NOPE_D:] = kpem_ref[...]
        k_scr[META_PAD:, :NOPE_D] = kv_ref[:, :NOPE_D]
        k_scr[META_PAD:, NOPE_D:] = kpe_ref[...]
        vt_scr[:, :META_PAD] = _transpose_bf16(kvm_ref[:, NOPE_D:])
        for c in range(nk):
            vt_scr[:, META_PAD + c * tk:META_PAD + (c + 1) * tk] = _transpose_bf16(
                kv_ref[c * tk:(c + 1) * tk, NOPE_D:])

    q = q_ref[...]

    def qk(j):
        return _dot_nt(k_scr[META_PAD + j * tk:META_PAD + (j + 1) * tk, :], q)

    st = _softmax_first_t(_meta_mask_t(_dot_nt(k_scr[:META_PAD, :], q)), vt_scr[:, :META_PAD])

    s = qk(0)
    for j in range(nk):
        nxt = qk(j + 1) if j + 1 < nk else None
        st = _softmax_next_t(s, 0.0, vt_scr[:, META_PAD + j * tk:META_PAD + (j + 1) * tk], *st)
        s = nxt

    out_ref[...] = (st[2] / st[1]).T.astype(out_ref.dtype)


def _mla_attn(q, kv, kpe, kvm, kpem, *, batch, seq, tq, tk, name):
    mrows = q.shape[0]
    nq = seq // tq
    kern = functools.partial(_mla_attn_kernel, tk=tk)
    blocks = (_nbytes((tq, MLA_SLOT), BF16) + _nbytes((seq, MLA_SLOT), BF16)
              + _nbytes((seq, LANES), BF16) + _nbytes((META_PAD, MLA_SLOT + LANES), BF16)
              + _nbytes((tq, MLA_V_D), BF16))
    scratch = _nbytes((seq + META_PAD, MLA_SLOT), BF16) + _nbytes((MLA_V_D, seq + META_PAD), BF16)
    return pl.pallas_call(
        kern,
        grid=(batch, MLA_HEADS, nq),
        in_specs=[pl.BlockSpec((tq, MLA_SLOT), lambda b, h, i: (b * nq + i, h)),
                  pl.BlockSpec((seq, MLA_SLOT), lambda b, h, i: (b, h)),
                  pl.BlockSpec((seq, LANES), lambda b, h, i: (b, 0)),
                  pl.BlockSpec((META_PAD, MLA_SLOT), lambda b, h, i: (0, h)),
                  pl.BlockSpec((META_PAD, LANES), lambda b, h, i: (0, 0))],
        out_specs=pl.BlockSpec((tq, MLA_V_D), lambda b, h, i: (b * nq + i, h)),
        out_shape=jax.ShapeDtypeStruct((mrows, MLA_HEADS * MLA_V_D), BF16),
        scratch_shapes=[pltpu.VMEM((seq + META_PAD, MLA_SLOT), BF16),
                        pltpu.VMEM((MLA_V_D, seq + META_PAD), BF16)],
        compiler_params=_params(("parallel", "parallel", "arbitrary"), blocks, scratch,
                                6 * _nbytes((tk, tq), F32)),
        name=name,
    )(q, kv, kpe, kvm, kpem)


def _merge_kernel(od_ref, om_ref, wa_ref, wb_ref, gd_ref, gm_ref, side_ref, out_ref, side_out_ref):
    ya = jnp.dot(od_ref[...], wa_ref[...], preferred_element_type=F32)
    yb = jnp.dot(om_ref[...], wb_ref[...], preferred_element_type=F32)
    out_ref[...] = (gd_ref[...].astype(F32) * ya + gm_ref[...].astype(F32) * yb).astype(out_ref.dtype)
    _side_cast_run([side_ref], [side_out_ref])


def _merge(od, om, wa, wb, gates, side, *, tm, tn, name):
    m, k = od.shape
    n = wa.shape[1]
    nj = n // tn
    grid = (m // tm, nj)
    side_in, side_out, side_shape, side_bytes = _side_cast_specs(side, grid)
    blocks = (2 * _nbytes((tm, k), BF16) + 2 * _nbytes((k, tn), BF16) + 3 * _nbytes((tm, tn), BF16)
              + side_bytes)
    return pl.pallas_call(
        _merge_kernel,
        grid=grid,
        in_specs=[pl.BlockSpec((tm, k), lambda i, j: (i, 0)),
                  pl.BlockSpec((tm, k), lambda i, j: (i, 0)),
                  pl.BlockSpec((k, tn), lambda i, j: (0, j)),
                  pl.BlockSpec((k, tn), lambda i, j: (0, j)),
                  pl.BlockSpec((tm, tn), lambda i, j: (i, j)),
                  pl.BlockSpec((tm, tn), lambda i, j: (i, nj + j)),
                  side_in],
        out_specs=[pl.BlockSpec((tm, tn), lambda i, j: (i, j)), side_out],
        out_shape=[jax.ShapeDtypeStruct((m, n), BF16), side_shape],
        compiler_params=_params(("parallel", "arbitrary"), blocks, 0, 4 * _nbytes((tm, tn), F32)),
        name=name,
    )(od, om, wa, wb, gates, gates, side.w)


def _rope_tables(t):
    inv_freq = 1.0 / (ROPE_THETA ** (jnp.arange(0, ROPE_D, 2, dtype=F32) / ROPE_D))
    ang = jnp.arange(t).astype(F32)[:, None] * inv_freq[None, :]
    pad = jnp.zeros((t, LANES - ROPE_D), F32)
    cos = jnp.concatenate([jnp.cos(ang), jnp.cos(ang), pad], axis=-1)
    sin = jnp.concatenate([jnp.sin(ang), jnp.sin(ang), pad], axis=-1)
    return cos, sin


def _rotate_cols(w):
    half = ROPE_D // 2
    return jnp.concatenate([-w[..., half:], w[..., :half]], axis=-1)


def kernel(x, meta_tokens, ffn1_norm, ffn1_w_gate, ffn1_w_up, ffn1_w_down, mix_norm, w_in, diff_lambda_q1, diff_lambda_k1, diff_lambda_q2, diff_lambda_k2, diff_subln, mla_q_norm, mla_w_uq, mla_kv_norm, mla_w_ukv, w_gate, b_gate, w_branch_diff, w_branch_mla, w_out, ffn2_norm, ffn2_w_gate, ffn2_w_up, ffn2_w_down, final_norm):
    batch, seq, d = x.shape
    depth = ffn1_norm.shape[0]
    m = batch * seq
    dq_w = DIFF_HEADS * 2 * DIFF_HEAD_DIM
    dv_w = DIFF_HEADS * DIFF_V_DIM
    main_w = 2 * dq_w + dv_w + Q_LORA + KV_LORA
    cq_block = (2 * dq_w + dv_w) // Q_LORA
    ckv_block = (2 * dq_w + dv_w + Q_LORA) // KV_LORA
    proj_steps = (m // ROWS) * (main_w // PROJ_COLS)
    gate_steps = (m // ROWS) * (2 * d // GATE_COLS)
    merge_steps = (m // ROWS) * (d // MERGE_COLS)

    cos, sin = _rope_tables(N_META + seq)
    cos_m, sin_m, cos_r, sin_r = cos[:N_META], sin[:N_META], cos[N_META:], sin[N_META:]
    slopes = jnp.asarray([2.0 ** (-8.0 * (h + 1) / DIFF_HEADS) for h in range(DIFF_HEADS)],
                         F32).reshape(DIFF_HEADS, 1, 1)
    col_scale = jnp.concatenate([jnp.full((1, dq_w), DIFF_HEAD_DIM ** -0.5 * LOG2E, F32),
                                 jnp.ones((1, main_w - dq_w), F32)], axis=-1)
    row = lambda v: v.reshape(1, -1).astype(F32)

    xs = x.reshape(m, d)
    xm = meta_tokens.astype(x.dtype)

    for l in range(depth):
        lam_init = 0.8 - 0.6 * math.exp(-0.3 * l)
        w_main = _cast_cols(w_in, l, main_w, tr=min(CAST_ROWS, d), tc=CAST_COLS, name="cast_w_in")
        w_kr = _tail_cols(w_in, l, main_w, tr=min(CAST_ROWS, d), name="w_in_rope_cols")
        zpad = jnp.zeros((d, LANES - ROPE_D), F32)
        w_kpe = jnp.concatenate([w_kr, zpad, _rotate_cols(w_kr), zpad], axis=-1).astype(BF16)

        wuq = mla_w_uq[l].reshape(Q_LORA, MLA_HEADS, NOPE_D + ROPE_D)
        w_q = jnp.concatenate([wuq, _rotate_cols(wuq[..., NOPE_D:])], axis=-1)
        w_q = w_q.reshape(Q_LORA, MLA_HEADS * MLA_SLOT).astype(BF16)
        w_ukv = mla_w_ukv[l].astype(BF16)

        xm, wg1, wu1, wd1 = _ffn_meta(xm, row(ffn1_norm[l]), ffn1_w_gate, ffn1_w_up, ffn1_w_down, l,
                                      tf=FFN_CHUNK, name="ffn1_meta")
        xs = _ffn(xs, row(ffn1_norm[l]), wg1, wu1, wd1, tm=FFN_ROWS, tf=FFN_CHUNK, name="ffn1")

        hs = _norm(xs, row(mix_norm[l]), tm=FFN_ROWS, name="mix_norm")
        hm = _norm(xm, row(mix_norm[l]), tm=N_META, name="mix_norm_meta")

        ps, wg2, w_g, w_o = _mm(hs, w_main, tm=ROWS, tn=PROJ_COLS, out_dtype=BF16, scale=col_scale,
                                name="proj", sides=(_side(ffn2_w_gate, l, proj_steps),
                                                    _side(w_gate, l, proj_steps),
                                                    _side(w_out, l, proj_steps)))
        pm = _mm(hm, w_main, tm=N_META, tn=PROJ_COLS, out_dtype=BF16, scale=col_scale, name="proj_meta")
        gates, wu2, w_pa, w_pb = _mm(hs, w_g, tm=ROWS, tn=GATE_COLS, out_dtype=BF16,
                                     bias=row(b_gate[l]), name="gates",
                                     sides=(_side(ffn2_w_up, l, gate_steps),
                                            _side(w_branch_diff, l, gate_steps),
                                            _side(w_branch_mla, l, gate_steps)))
        kpe_s = _kpe(hs, w_kpe, cos_r, sin_r, tm=ROWS, name="kpe")
        kpe_m = _kpe(hm, w_kpe, cos_m, sin_m, tm=N_META, name="kpe_meta")

        q_mla = _mlaq(ps, cq_block, row(mla_q_norm[l]), w_q, cos_r, sin_r, tm=ROWS, heads=MLAQ_HEADS,
                      scale=(NOPE_D + ROPE_D) ** -0.5 * LOG2E, name="mla_q")
        kv_s = _mlakv(ps, ckv_block, row(mla_kv_norm[l]), w_ukv, tm=ROWS, tn=MLAKV_COLS, name="mla_kv")
        kv_m = _mlakv(pm, ckv_block, row(mla_kv_norm[l]), w_ukv, tm=N_META, tn=MLAKV_COLS,
                      name="mla_kv_meta")

        pad_rows = lambda a: jnp.pad(a, ((0, META_PAD - N_META), (0, 0)))
        o_diff = _diff_attn(ps, pad_rows(pm), slopes, row(diff_lambda_q1[l]), row(diff_lambda_k1[l]),
                            row(diff_lambda_q2[l]), row(diff_lambda_k2[l]), row(diff_subln[l]),
                            batch=batch, seq=seq, tq=DIFF_TQ, tk=DIFF_TK, lam_init=lam_init,
                            name="diff_attn")
        o_mla = _mla_attn(q_mla, kv_s, kpe_s, pad_rows(kv_m), pad_rows(kpe_m),
                          batch=batch, seq=seq, tq=MLA_TQ, tk=MLA_TK, name="mla_attn")

        merged, wd2 = _merge(o_diff, o_mla, w_pa, w_pb, gates,
                             _side(ffn2_w_down, l, merge_steps, bc=MERGE_COLS),
                             tm=ROWS, tn=MERGE_COLS, name="merge")
        xs = _mm(merged, w_o, tm=ROWS, tn=OUT_COLS, out_dtype=F32, res=xs, name="out_proj")

        last = l == depth - 1
        xs = _ffn(xs, row(ffn2_norm[l]), wg2, wu2, wd2, tm=FFN_ROWS, tf=FFN_CHUNK,
                  final_gain=row(final_norm) if last else None, name="ffn2")
        if not last:
            raise NotImplementedError("DEPTH > 1 needs the meta-row query path")

    return xs.reshape(batch, seq, d)
```

```python
import functools
import math
from typing import NamedTuple

import numpy as np
import jax
import jax.numpy as jnp
from jax import lax
from jax.experimental import pallas as pl
from jax.experimental.pallas import tpu as pltpu

N_META = 16
EPS = 1e-6
DIFF_HEADS = 8
DIFF_HEAD_DIM = 128
DIFF_V_DIM = 256
MLA_HEADS = 16
Q_LORA = 1024
KV_LORA = 512
NOPE_D = 128
ROPE_D = 64
MLA_V_D = 128
ROPE_THETA = 10000.0

LANES = 128
MLA_SLOT = 256
META_PAD = 128
VMEM_CAP = 60000 * 1024
VMEM_PHYSICAL = 64 * 2**20
VMEM_RESERVE = 2 * 2**20
VMEM_FLOOR = 16 * 2**20
NEG_BIG = -1e30
LOG2E = math.log2(math.e)

ROWS = 1024
PROJ_COLS = 768
GATE_COLS = 1024
MERGE_COLS = 512
OUT_COLS = 512
FFN_ROWS = 512
FFN_CHUNK = 256
CAST_ROWS, CAST_COLS = 1024, 1536
MLAQ_HEADS = 8
MLAKV_COLS = 2048
DIFF_TQ, DIFF_TK = 256, 512
MLA_TQ, MLA_TK = 2048, 512

F32 = jnp.float32
BF16 = jnp.bfloat16


def _nbytes(shape, dtype):
    return int(np.prod(shape)) * jnp.dtype(dtype).itemsize


def _params(semantics, block_bytes, scratch_bytes, temp_bytes):
    need = 2 * block_bytes + scratch_bytes + temp_bytes
    return pltpu.CompilerParams(dimension_semantics=semantics,
                                vmem_limit_bytes=int(min(VMEM_CAP, max(need, VMEM_FLOOR))))


def _rms(x, gain):
    return x * lax.rsqrt(jnp.mean(x * x, axis=-1, keepdims=True) + EPS) * gain


def _ffn_kernel(x_ref, g_ref, wga_ref, wua_ref, wda_ref, wgb_ref, wub_ref, wdb_ref, *rest,
                final_norm, nchunks):
    out_ref, h_scr, x_scr, x_sem = rest[-4:]
    i = pl.program_id(0)
    f = pl.program_id(1)
    last = pl.num_programs(1) - 1
    tm = x_scr.shape[0]

    def x_copy(block):
        return pltpu.make_async_copy(x_ref.at[pl.ds(block * tm, tm), :], x_scr, x_sem)

    def act(wg_ref, wu_ref):
        h = h_scr[...]
        g = jnp.dot(h, wg_ref[...], preferred_element_type=F32)
        u = jnp.dot(h, wu_ref[...], preferred_element_type=F32)
        return (0.5 * (g * jax.nn.sigmoid(g)) * u).astype(BF16)

    def both():
        a = act(wga_ref, wua_ref)
        b = act(wgb_ref, wub_ref)
        return (jnp.dot(a, wda_ref[...], preferred_element_type=F32)
                + jnp.dot(b, wdb_ref[...], preferred_element_type=F32))

    @pl.when((f == 0) & (i == 0))
    def _():
        x_copy(0).start()

    @pl.when(f == 0)
    def _():
        x_copy(i).wait()
        x = x_scr[...]
        h_scr[...] = _rms(x, g_ref[...]).astype(BF16)
        out_ref[...] = x + both()

    @pl.when((f == 1) & (i + 1 < pl.num_programs(0)))
    def _():
        x_copy(i + 1).start()

    @pl.when((f > 0) & (f < last) if nchunks % 2 else f > 0)
    def _():
        out_ref[...] += both()

    if nchunks % 2:
        @pl.when((f == last) & (f > 0))
        def _():
            out_ref[...] += jnp.dot(act(wga_ref, wua_ref), wda_ref[...], preferred_element_type=F32)

    if final_norm:
        gf_ref = rest[0]

        @pl.when(f == last)
        def _():
            out_ref[...] = _rms(out_ref[...], gf_ref[...])


def _ffn(x, gain, wg, wu, wd, *, tm, tf, name, final_gain=None):
    m, d = x.shape
    nchunks = wg.shape[1] // tf
    assert nchunks > 2, "step 0 handles two chunks and step 1 prefetches the next rows"
    kern = functools.partial(_ffn_kernel, final_norm=final_gain is not None, nchunks=nchunks)
    vec = pl.BlockSpec((1, d), lambda i, f: (0, 0))
    finals = [] if final_gain is None else [final_gain]
    col_a = lambda i, f: (0, 2 * f)
    col_b = lambda i, f: (0, jnp.minimum(2 * f + 1, nchunks - 1))
    row_a = lambda i, f: (2 * f, 0)
    row_b = lambda i, f: (jnp.minimum(2 * f + 1, nchunks - 1), 0)
    weights = 2 * (2 * _nbytes((d, tf), BF16) + _nbytes((tf, d), BF16))
    need = (_nbytes((tm, d), F32) + 2 * _nbytes((tm, d), F32) + 2 * weights
            + _nbytes((tm, d), BF16) + 6 * _nbytes((tm, tf), F32) + _nbytes((tm, d), F32))
    return pl.pallas_call(
        kern,
        grid=(m // tm, (nchunks + 1) // 2),
        in_specs=[
            pl.BlockSpec(memory_space=pl.ANY),
            vec,
            pl.BlockSpec((d, tf), col_a), pl.BlockSpec((d, tf), col_a), pl.BlockSpec((tf, d), row_a),
            pl.BlockSpec((d, tf), col_b), pl.BlockSpec((d, tf), col_b), pl.BlockSpec((tf, d), row_b),
        ] + [vec] * len(finals),
        out_specs=pl.BlockSpec((tm, d), lambda i, f: (i, 0)),
        out_shape=jax.ShapeDtypeStruct((m, d), F32),
        scratch_shapes=[pltpu.VMEM((tm, d), BF16), pltpu.VMEM((tm, d), F32),
                        pltpu.SemaphoreType.DMA(())],
        compiler_params=pltpu.CompilerParams(
            dimension_semantics=("arbitrary", "arbitrary"),
            vmem_limit_bytes=int(min(VMEM_PHYSICAL - VMEM_RESERVE, max(need, VMEM_FLOOR)))),
        name=name,
    )(x, gain, wg, wu, wd, wg, wu, wd, *finals)


def _ffn_meta_kernel(x_ref, g_ref, wg_ref, wu_ref, wd_ref, out_ref, wg_out, wu_out, wd_out, h_scr):
    @pl.when(pl.program_id(0) == 0)
    def _():
        x = x_ref[...]
        h_scr[...] = _rms(x, g_ref[...]).astype(BF16)
        out_ref[...] = x

    wg = wg_ref[...].astype(BF16)
    wu = wu_ref[...].astype(BF16)
    wd = wd_ref[...].astype(BF16)
    wg_out[...] = wg
    wu_out[...] = wu
    wd_out[...] = wd
    h = h_scr[...]
    g = jnp.dot(h, wg, preferred_element_type=F32)
    u = jnp.dot(h, wu, preferred_element_type=F32)
    a = (0.5 * (g * jax.nn.sigmoid(g)) * u).astype(BF16)
    out_ref[...] += jnp.dot(a, wd, preferred_element_type=F32)


def _ffn_meta(x, gain, wg, wu, wd, l, *, tf, name):
    m, d = x.shape
    ff = wg.shape[2]
    blocks = (2 * _nbytes((m, d), F32) + 3 * _nbytes((d, tf), F32) + 3 * _nbytes((d, tf), BF16))
    return pl.pallas_call(
        _ffn_meta_kernel,
        grid=(ff // tf,),
        in_specs=[pl.BlockSpec((m, d), lambda f: (0, 0)),
                  pl.BlockSpec((1, d), lambda f: (0, 0)),
                  pl.BlockSpec((None, d, tf), lambda f: (l, 0, f)),
                  pl.BlockSpec((None, d, tf), lambda f: (l, 0, f)),
                  pl.BlockSpec((None, tf, d), lambda f: (l, f, 0))],
        out_specs=[pl.BlockSpec((m, d), lambda f: (0, 0)),
                   pl.BlockSpec((d, tf), lambda f: (0, f)),
                   pl.BlockSpec((d, tf), lambda f: (0, f)),
                   pl.BlockSpec((tf, d), lambda f: (f, 0))],
        out_shape=[jax.ShapeDtypeStruct((m, d), F32),
                   jax.ShapeDtypeStruct((d, ff), BF16),
                   jax.ShapeDtypeStruct((d, ff), BF16),
                   jax.ShapeDtypeStruct((ff, d), BF16)],
        scratch_shapes=[pltpu.VMEM((m, d), BF16)],
        compiler_params=_params(("arbitrary",), blocks, _nbytes((m, d), BF16),
                                3 * _nbytes((d, tf), BF16)),
        name=name,
    )(x, gain, wg, wu, wd)


def _norm_kernel(x_ref, g_ref, out_ref):
    out_ref[...] = _rms(x_ref[...], g_ref[...]).astype(out_ref.dtype)


def _norm(x, gain, *, tm, name):
    m, d = x.shape
    blocks = _nbytes((tm, d), F32) + _nbytes((tm, d), BF16)
    return pl.pallas_call(
        _norm_kernel,
        grid=(m // tm,),
        in_specs=[pl.BlockSpec((tm, d), lambda i: (i, 0)),
                  pl.BlockSpec((1, d), lambda i: (0, 0))],
        out_specs=pl.BlockSpec((tm, d), lambda i: (i, 0)),
        out_shape=jax.ShapeDtypeStruct((m, d), BF16),
        compiler_params=_params(("parallel",), blocks, 0, 2 * _nbytes((tm, d), F32)),
        name=name,
    )(x, gain)


def _cast_kernel(w_ref, out_ref):
    out_ref[...] = w_ref[...].astype(out_ref.dtype)


def _cast_cols(w, l, cols, *, tr, tc, name):
    rows = w.shape[1]
    blocks = _nbytes((tr, tc), w.dtype) + _nbytes((tr, tc), BF16)
    return pl.pallas_call(
        _cast_kernel,
        grid=(rows // tr, cols // tc),
        in_specs=[pl.BlockSpec((None, tr, tc), lambda i, j: (l, i, j))],
        out_specs=pl.BlockSpec((tr, tc), lambda i, j: (i, j)),
        out_shape=jax.ShapeDtypeStruct((rows, cols), BF16),
        compiler_params=_params(("parallel", "parallel"), blocks, 0, 0),
        name=name,
    )(w)


def _tail_kernel(w_ref, out_ref, *, valid):
    lane = lax.broadcasted_iota(jnp.int32, out_ref.shape, 1)
    out_ref[...] = jnp.where(lane < valid, w_ref[...], 0.0)


def _tail_cols(w, l, start, *, tr, name):
    rows, cols = w.shape[1:]
    assert start % LANES == 0 and cols - start <= LANES
    blocks = 2 * _nbytes((tr, LANES), F32)
    out = pl.pallas_call(
        functools.partial(_tail_kernel, valid=cols - start),
        grid=(rows // tr,),
        in_specs=[pl.BlockSpec((None, tr, LANES), lambda i: (l, i, start // LANES))],
        out_specs=pl.BlockSpec((tr, LANES), lambda i: (i, 0)),
        out_shape=jax.ShapeDtypeStruct((rows, LANES), F32),
        compiler_params=_params(("parallel",), blocks, 0, 0),
        name=name,
    )(w)
    return out[:, :cols - start]


class SideCast(NamedTuple):
    w: jax.Array
    l: int
    br: int
    bc: int


def _side(w, l, steps, bc=None):
    rows, cols = w.shape[1:]
    bc = cols if bc is None else bc
    sublanes = 16
    br = next(r for r in range(sublanes, rows + 1, sublanes)
              if rows % r == 0 and (rows // r) * (cols // bc) <= steps)
    return SideCast(w, l, br, bc)


def _side_cast_specs(side, grid):
    rows, cols = side.w.shape[1:]
    ncols = cols // side.bc
    nblocks = (rows // side.br) * ncols
    assert rows % side.br == 0 and cols % side.bc == 0 and nblocks <= math.prod(grid), (side.w.shape, grid)

    def block(*idx):
        step = 0
        for i, n in zip(idx, grid):
            step = step * n + i
        t = jnp.minimum(step, nblocks - 1)
        return t // ncols, t % ncols

    in_spec = pl.BlockSpec((None, side.br, side.bc), lambda *idx: (side.l, *block(*idx)))
    out_spec = pl.BlockSpec((side.br, side.bc), block)
    nbytes = _nbytes((side.br, side.bc), F32) + _nbytes((side.br, side.bc), BF16)
    return in_spec, out_spec, jax.ShapeDtypeStruct((rows, cols), BF16), nbytes


def _side_cast_run(side_in_refs, side_out_refs):
    for src, dst in zip(side_in_refs, side_out_refs):
        dst[...] = src[...].astype(dst.dtype)


def _mm_kernel(*refs, has_scale, has_bias, has_res, n_side):
    n_in = 2 + has_scale + has_bias + has_res + n_side
    a_ref, w_ref = refs[0], refs[1]
    extra = list(refs[2:n_in - n_side])
    out_ref = refs[n_in]
    y = jnp.dot(a_ref[...], w_ref[...], preferred_element_type=F32)
    if has_scale:
        y = y * extra.pop(0)[...]
    if has_bias:
        y = jax.nn.sigmoid(y + extra.pop(0)[...])
    if has_res:
        y = y + extra.pop(0)[...]
    out_ref[...] = y.astype(out_ref.dtype)
    _side_cast_run(refs[n_in - n_side:n_in], refs[n_in + 1:])


def _mm(a, w, *, tm, tn, out_dtype, name, scale=None, bias=None, res=None, sides=()):
    m, k = a.shape
    n = w.shape[1]
    grid = (m // tm, n // tn)
    ins = [a, w]
    in_specs = [pl.BlockSpec((tm, k), lambda i, j: (i, 0)),
                pl.BlockSpec((k, tn), lambda i, j: (0, j))]
    blocks = _nbytes((tm, k), a.dtype) + _nbytes((k, tn), w.dtype) + _nbytes((tm, tn), out_dtype)
    for vec in (scale, bias):
        if vec is not None:
            ins.append(vec)
            in_specs.append(pl.BlockSpec((1, tn), lambda i, j: (0, j)))
    if res is not None:
        ins.append(res)
        in_specs.append(pl.BlockSpec((tm, tn), lambda i, j: (i, j)))
        blocks += _nbytes((tm, tn), res.dtype)
    out_specs = [pl.BlockSpec((tm, tn), lambda i, j: (i, j))]
    out_shape = [jax.ShapeDtypeStruct((m, n), out_dtype)]
    for side in sides:
        in_spec, out_spec, shape, nbytes = _side_cast_specs(side, grid)
        ins.append(side.w)
        in_specs.append(in_spec)
        out_specs.append(out_spec)
        out_shape.append(shape)
        blocks += nbytes
    kern = functools.partial(_mm_kernel, has_scale=scale is not None, has_bias=bias is not None,
                             has_res=res is not None, n_side=len(sides))
    outs = pl.pallas_call(
        kern,
        grid=grid,
        in_specs=in_specs,
        out_specs=out_specs,
        out_shape=out_shape,
        compiler_params=_params(("parallel", "arbitrary"), blocks, 0, 4 * _nbytes((tm, tn), F32)),
        name=name,
    )(*ins)
    return outs if sides else outs[0]


def _kpe_kernel(h_ref, w_ref, cos_ref, sin_ref, out_ref):
    y = jnp.dot(h_ref[...], w_ref[...], preferred_element_type=F32)
    out_ref[...] = (y[:, :LANES] * cos_ref[...] + y[:, LANES:] * sin_ref[...]).astype(out_ref.dtype)


def _kpe(h, w, cos, sin, *, tm, name):
    m, d = h.shape
    nt = cos.shape[0] // tm
    blocks = _nbytes((tm, d), BF16) + _nbytes((d, 2 * LANES), BF16) + 3 * _nbytes((tm, LANES), F32)
    return pl.pallas_call(
        _kpe_kernel,
        grid=(m // tm,),
        in_specs=[pl.BlockSpec((tm, d), lambda i: (i, 0)),
                  pl.BlockSpec((d, 2 * LANES), lambda i: (0, 0)),
                  pl.BlockSpec((tm, LANES), lambda i: (i % nt, 0)),
                  pl.BlockSpec((tm, LANES), lambda i: (i % nt, 0))],
        out_specs=pl.BlockSpec((tm, LANES), lambda i: (i, 0)),
        out_shape=jax.ShapeDtypeStruct((m, LANES), BF16),
        compiler_params=_params(("parallel",), blocks, 0, _nbytes((tm, 2 * LANES), F32)),
        name=name,
    )(h, w, cos, sin)


def _mlaq_kernel(c_ref, g_ref, w_ref, cos_ref, sin_ref, out_ref, cn_scr, *, heads, scale):
    @pl.when(pl.program_id(1) == 0)
    def _():
        cn_scr[...] = _rms(c_ref[...].astype(F32), g_ref[...]).astype(BF16)

    y = jnp.dot(cn_scr[...], w_ref[...], preferred_element_type=F32)
    cos = cos_ref[...]
    sin = sin_ref[...]
    for h in range(heads):
        s0 = h * MLA_SLOT
        out_ref[:, s0:s0 + NOPE_D] = (y[:, s0:s0 + NOPE_D] * scale).astype(out_ref.dtype)
        hi = y[:, s0 + NOPE_D:s0 + MLA_SLOT]
        pe = hi * cos + pltpu.roll(hi, LANES // 2, axis=1) * sin
        out_ref[:, s0 + NOPE_D:s0 + MLA_SLOT] = (pe * scale).astype(out_ref.dtype)


def _mlaq(p, col_block, gain, w, cos, sin, *, tm, heads, scale, name):
    m = p.shape[0]
    k = gain.shape[1]
    wn = heads * MLA_SLOT
    groups = w.shape[1] // wn
    nt = cos.shape[0] // tm
    kern = functools.partial(_mlaq_kernel, heads=heads, scale=scale)
    blocks = (_nbytes((tm, k), BF16) + _nbytes((k, wn), BF16) + 2 * _nbytes((tm, LANES), F32)
              + _nbytes((tm, heads * MLA_SLOT), BF16))
    return pl.pallas_call(
        kern,
        grid=(m // tm, groups),
        in_specs=[pl.BlockSpec((tm, k), lambda i, j: (i, col_block)),
                  pl.BlockSpec((1, k), lambda i, j: (0, 0)),
                  pl.BlockSpec((k, wn), lambda i, j: (0, j)),
                  pl.BlockSpec((tm, LANES), lambda i, j: (i % nt, 0)),
                  pl.BlockSpec((tm, LANES), lambda i, j: (i % nt, 0))],
        out_specs=pl.BlockSpec((tm, heads * MLA_SLOT), lambda i, j: (i, j)),
        out_shape=jax.ShapeDtypeStruct((m, groups * heads * MLA_SLOT), BF16),
        scratch_shapes=[pltpu.VMEM((tm, k), BF16)],
        compiler_params=_params(("parallel", "arbitrary"), blocks, _nbytes((tm, k), BF16),
                                2 * _nbytes((tm, wn), F32)),
        name=name,
    )(p, gain, w, cos, sin)


def _mlakv_kernel(c_ref, g_ref, w_ref, out_ref, cn_scr):
    @pl.when(pl.program_id(1) == 0)
    def _():
        cn_scr[...] = _rms(c_ref[...].astype(F32), g_ref[...]).astype(BF16)

    out_ref[...] = jnp.dot(cn_scr[...], w_ref[...], preferred_element_type=F32).astype(out_ref.dtype)


def _mlakv(p, col_block, gain, w, *, tm, tn, name):
    m = p.shape[0]
    k, n = w.shape
    blocks = _nbytes((tm, k), BF16) + _nbytes((k, tn), BF16) + _nbytes((tm, tn), BF16)
    return pl.pallas_call(
        _mlakv_kernel,
        grid=(m // tm, n // tn),
        in_specs=[pl.BlockSpec((tm, k), lambda i, j: (i, col_block)),
                  pl.BlockSpec((1, k), lambda i, j: (0, 0)),
                  pl.BlockSpec((k, tn), lambda i, j: (0, j))],
        out_specs=pl.BlockSpec((tm, tn), lambda i, j: (i, j)),
        out_shape=jax.ShapeDtypeStruct((m, n), BF16),
        scratch_shapes=[pltpu.VMEM((tm, k), BF16)],
        compiler_params=_params(("parallel", "arbitrary"), blocks, _nbytes((tm, k), BF16),
                                2 * _nbytes((tm, tn), F32)),
        name=name,
    )(p, gain, w)


def _dot_nt(a, b):
    return lax.dot_general(a, b, (((1,), (1,)), ((), ())), preferred_element_type=F32)


def _transpose_bf16(x):
    return x.astype(F32).T.astype(BF16)


def _softmax_first_t(s_t, v_t):
    m = jnp.max(s_t, axis=0, keepdims=True)
    p = jnp.exp2(s_t - m)
    return m, jnp.sum(p, axis=0, keepdims=True), jnp.dot(v_t, p.astype(BF16), preferred_element_type=F32)


def _softmax_next_t(s_t, shift, v_t, m, l, acc):
    m_new = jnp.maximum(m, jnp.max(s_t, axis=0, keepdims=True) + shift)
    alpha = jnp.exp2(m - m_new)
    p = jnp.exp2(s_t - (m_new - shift))
    l = alpha * l + jnp.sum(p, axis=0, keepdims=True)
    acc = alpha * acc + jnp.dot(v_t, p.astype(BF16), preferred_element_type=F32)
    return m_new, l, acc


def _meta_mask_t(s_t):
    row = lax.broadcasted_iota(jnp.int32, s_t.shape, 0)
    return jnp.where(row < N_META, s_t, NEG_BIG)


def _diff_attn_kernel(q_ref, k_ref, v_ref, km_ref, vm_ref, slope_ref, lq1_ref, lk1_ref, lq2_ref,
                      lk2_ref, subln_ref, out_ref, vt_scr, bias_scr, *, tk, lam_init):
    tq = q_ref.shape[0]
    ratio = tk // tq
    nk = k_ref.shape[0] // tk
    hd = DIFF_HEAD_DIM
    qi = pl.program_id(2)
    slope2 = slope_ref[...] * LOG2E

    @pl.when(qi == 0)
    def _():
        vt_scr[:, :META_PAD] = _transpose_bf16(vm_ref[...])
        for c in range(nk):
            vt_scr[:, META_PAD + c * tk:META_PAD + (c + 1) * tk] = _transpose_bf16(
                v_ref[c * tk:(c + 1) * tk, :])
        d = (lax.broadcasted_iota(jnp.int32, (tk, tq), 1)
             - lax.broadcasted_iota(jnp.int32, (tk, tq), 0)).astype(F32) * slope2
        bias_scr[0] = -d
        bias_scr[1] = d
        for o in range(ratio):
            bias_scr[2 + o] = -jnp.abs(d + (o * tq) * slope2)

    q1 = q_ref[:, :hd]
    q2 = q_ref[:, hd:]

    def qk(j):
        kb = k_ref[j * tk:(j + 1) * tk, :]
        return _dot_nt(kb[:, :hd], q1), _dot_nt(kb[:, hd:], q2)

    vtm = vt_scr[:, :META_PAD]
    st1 = _softmax_first_t(_meta_mask_t(_dot_nt(km_ref[:, :hd], q1)), vtm)
    st2 = _softmax_first_t(_meta_mask_t(_dot_nt(km_ref[:, hd:], q2)), vtm)

    s1, s2 = qk(0)
    for j in range(nk):
        nxt = qk(j + 1) if j + 1 < nk else None
        off = qi - j * ratio
        bias = bias_scr[jnp.where(off < 0, 1, jnp.where(off >= ratio, 0, 2 + off))]
        gap = jnp.where(off < 0, j * tk - qi * tq, jnp.where(off >= ratio, qi * tq - j * tk, 0))
        shift = -gap.astype(F32) * slope2
        vt = vt_scr[:, META_PAD + j * tk:META_PAD + (j + 1) * tk]
        st1 = _softmax_next_t(s1 + bias, shift, vt, *st1)
        st2 = _softmax_next_t(s2 + bias, shift, vt, *st2)
        if nxt is not None:
            s1, s2 = nxt

    lam = (jnp.exp(jnp.sum(lq1_ref[...] * lk1_ref[...], axis=-1, keepdims=True))
           - jnp.exp(jnp.sum(lq2_ref[...] * lk2_ref[...], axis=-1, keepdims=True)) + lam_init)
    o_t = st1[2] / st1[1] - lam * (st2[2] / st2[1])
    out_ref[...] = (_rms(o_t.T, subln_ref[...]) * (1.0 - lam_init)).astype(out_ref.dtype)


def _diff_attn(p, pm, slopes, lq1, lk1, lq2, lk2, subln, *, batch, seq, tq, tk, lam_init, name):
    assert tk % tq == 0 and seq % tk == 0
    m = p.shape[0]
    hw = 2 * DIFF_HEAD_DIM
    nq = seq // tq
    koff = DIFF_HEADS
    voff = 2 * DIFF_HEADS
    kern = functools.partial(_diff_attn_kernel, tk=tk, lam_init=lam_init)
    n_bias = 2 + tk // tq
    vec = pl.BlockSpec((1, DIFF_HEAD_DIM), lambda b, h, i: (0, 0))
    blocks = (2 * _nbytes((tq, hw), BF16) + 2 * _nbytes((seq, hw), BF16)
              + 2 * _nbytes((META_PAD, hw), BF16))
    scratch = _nbytes((hw, seq + META_PAD), BF16) + n_bias * _nbytes((tk, tq), F32)
    return pl.pallas_call(
        kern,
        grid=(batch, DIFF_HEADS, nq),
        in_specs=[pl.BlockSpec((tq, hw), lambda b, h, i: (b * nq + i, h)),
                  pl.BlockSpec((seq, hw), lambda b, h, i: (b, koff + h)),
                  pl.BlockSpec((seq, hw), lambda b, h, i: (b, voff + h)),
                  pl.BlockSpec((META_PAD, hw), lambda b, h, i: (0, koff + h)),
                  pl.BlockSpec((META_PAD, hw), lambda b, h, i: (0, voff + h)),
                  pl.BlockSpec((None, 1, 1), lambda b, h, i: (h, 0, 0)),
                  vec, vec, vec, vec,
                  pl.BlockSpec((1, hw), lambda b, h, i: (0, 0))],
        out_specs=pl.BlockSpec((tq, hw), lambda b, h, i: (b * nq + i, h)),
        out_shape=jax.ShapeDtypeStruct((m, DIFF_HEADS * hw), BF16),
        scratch_shapes=[pltpu.VMEM((hw, seq + META_PAD), BF16),
                        pltpu.VMEM((n_bias, tk, tq), F32)],
        compiler_params=_params(("parallel", "parallel", "arbitrary"), blocks, scratch,
                                12 * _nbytes((tk, tq), F32)),
        name=name,
    )(p, p, p, pm, pm, slopes, lq1, lk1, lq2, lk2, subln)


def _mla_attn_kernel(q_ref, kv_ref, kpe_ref, kvm_ref, kpem_ref, out_ref, k_scr, vt_scr, *, tk):
    nk = kv_ref.shape[0] // tk

    @pl.when(pl.program_id(2) == 0)
    def _():
        k_scr[:META_PAD, :NOPE_D] = kvm_ref[:, :NOPE_D]
        k_scr[:META_PAD, NOPE_D:] = kpem_ref[...]
        k_scr[META_PAD:, :NOPE_D] = kv_ref[:, :NOPE_D]
        k_scr[META_PAD:, NOPE_D:] = kpe_ref[...]
        vt_scr[:, :META_PAD] = _transpose_bf16(kvm_ref[:, NOPE_D:])
        for c in range(nk):
            vt_scr[:, META_PAD + c * tk:META_PAD + (c + 1) * tk] = _transpose_bf16(
                kv_ref[c * tk:(c + 1) * tk, NOPE_D:])

    q = q_ref[...]

    def qk(j):
        return _dot_nt(k_scr[META_PAD + j * tk:META_PAD + (j + 1) * tk, :], q)

    st = _softmax_first_t(_meta_mask_t(_dot_nt(k_scr[:META_PAD, :], q)), vt_scr[:, :META_PAD])

    s = qk(0)
    for j in range(nk):
        nxt = qk(j + 1) if j + 1 < nk else None
        st = _softmax_next_t(s, 0.0, vt_scr[:, META_PAD + j * tk:META_PAD + (j + 1) * tk], *st)
        s = nxt

    out_ref[...] = (st[2] / st[1]).T.astype(out_ref.dtype)


def _mla_attn(q, kv, kpe, kvm, kpem, *, batch, seq, tq, tk, name):
    mrows = q.shape[0]
    nq = seq // tq
    kern = functools.partial(_mla_attn_kernel, tk=tk)
    blocks = (_nbytes((tq, MLA_SLOT), BF16) + _nbytes((seq, MLA_SLOT), BF16)
              + _nbytes((seq, LANES), BF16) + _nbytes((META_PAD, MLA_SLOT + LANES), BF16)
              + _nbytes((tq, MLA_V_D), BF16))
    scratch = _nbytes((seq + META_PAD, MLA_SLOT), BF16) + _nbytes((MLA_V_D, seq + META_PAD), BF16)
    return pl.pallas_call(
        kern,
        grid=(batch, MLA_HEADS, nq),
        in_specs=[pl.BlockSpec((tq, MLA_SLOT), lambda b, h, i: (b * nq + i, h)),
                  pl.BlockSpec((seq, MLA_SLOT), lambda b, h, i: (b, h)),
                  pl.BlockSpec((seq, LANES), lambda b, h, i: (b, 0)),
                  pl.BlockSpec((META_PAD, MLA_SLOT), lambda b, h, i: (0, h)),
                  pl.BlockSpec((META_PAD, LANES), lambda b, h, i: (0, 0))],
        out_specs=pl.BlockSpec((tq, MLA_V_D), lambda b, h, i: (b * nq + i, h)),
        out_shape=jax.ShapeDtypeStruct((mrows, MLA_HEADS * MLA_V_D), BF16),
        scratch_shapes=[pltpu.VMEM((seq + META_PAD, MLA_SLOT), BF16),
                        pltpu.VMEM((MLA_V_D, seq + META_PAD), BF16)],
        compiler_params=_params(("parallel", "parallel", "arbitrary"), blocks, scratch,
                                6 * _nbytes((tk, tq), F32)),
        name=name,
    )(q, kv, kpe, kvm, kpem)


def _merge_kernel(od_ref, om_ref, wa_ref, wb_ref, gd_ref, gm_ref, side_ref, out_ref, side_out_ref):
    ya = jnp.dot(od_ref[...], wa_ref[...], preferred_element_type=F32)
    yb = jnp.dot(om_ref[...], wb_ref[...], preferred_element_type=F32)
    out_ref[...] = (gd_ref[...].astype(F32) * ya + gm_ref[...].astype(F32) * yb).astype(out_ref.dtype)
    _side_cast_run([side_ref], [side_out_ref])


def _merge(od, om, wa, wb, gates, side, *, tm, tn, name):
    m, k = od.shape
    n = wa.shape[1]
    nj = n // tn
    grid = (m // tm, nj)
    side_in, side_out, side_shape, side_bytes = _side_cast_specs(side, grid)
    blocks = (2 * _nbytes((tm, k), BF16) + 2 * _nbytes((k, tn), BF16) + 3 * _nbytes((tm, tn), BF16)
              + side_bytes)
    return pl.pallas_call(
        _merge_kernel,
        grid=grid,
        in_specs=[pl.BlockSpec((tm, k), lambda i, j: (i, 0)),
                  pl.BlockSpec((tm, k), lambda i, j: (i, 0)),
                  pl.BlockSpec((k, tn), lambda i, j: (0, j)),
                  pl.BlockSpec((k, tn), lambda i, j: (0, j)),
                  pl.BlockSpec((tm, tn), lambda i, j: (i, j)),
                  pl.BlockSpec((tm, tn), lambda i, j: (i, nj + j)),
                  side_in],
        out_specs=[pl.BlockSpec((tm, tn), lambda i, j: (i, j)), side_out],
        out_shape=[jax.ShapeDtypeStruct((m, n), BF16), side_shape],
        compiler_params=_params(("parallel", "arbitrary"), blocks, 0, 4 * _nbytes((tm, tn), F32)),
        name=name,
    )(od, om, wa, wb, gates, gates, side.w)


def _rope_tables(t):
    inv_freq = 1.0 / (ROPE_THETA ** (jnp.arange(0, ROPE_D, 2, dtype=F32) / ROPE_D))
    ang = jnp.arange(t).astype(F32)[:, None] * inv_freq[None, :]
    pad = jnp.zeros((t, LANES - ROPE_D), F32)
    cos = jnp.concatenate([jnp.cos(ang), jnp.cos(ang), pad], axis=-1)
    sin = jnp.concatenate([jnp.sin(ang), jnp.sin(ang), pad], axis=-1)
    return cos, sin


def _rotate_cols(w):
    half = ROPE_D // 2
    return jnp.concatenate([-w[..., half:], w[..., :half]], axis=-1)


def kernel(x, meta_tokens, ffn1_norm, ffn1_w_gate, ffn1_w_up, ffn1_w_down, mix_norm, w_in, diff_lambda_q1, diff_lambda_k1, diff_lambda_q2, diff_lambda_k2, diff_subln, mla_q_norm, mla_w_uq, mla_kv_norm, mla_w_ukv, w_gate, b_gate, w_branch_diff, w_branch_mla, w_out, ffn2_norm, ffn2_w_gate, ffn2_w_up, ffn2_w_down, final_norm):
    batch, seq, d = x.shape
    depth = ffn1_norm.shape[0]
    m = batch * seq
    dq_w = DIFF_HEADS * 2 * DIFF_HEAD_DIM
    dv_w = DIFF_HEADS * DIFF_V_DIM
    main_w = 2 * dq_w + dv_w + Q_LORA + KV_LORA
    cq_block = (2 * dq_w + dv_w) // Q_LORA
    ckv_block = (2 * dq_w + dv_w + Q_LORA) // KV_LORA
    proj_steps = (m // ROWS) * (main_w // PROJ_COLS)
    gate_steps = (m // ROWS) * (2 * d // GATE_COLS)
    merge_steps = (m // ROWS) * (d // MERGE_COLS)

    cos, sin = _rope_tables(N_META + seq)
    cos_m, sin_m, cos_r, sin_r = cos[:N_META], sin[:N_META], cos[N_META:], sin[N_META:]
    slopes = jnp.asarray([2.0 ** (-8.0 * (h + 1) / DIFF_HEADS) for h in range(DIFF_HEADS)],
                         F32).reshape(DIFF_HEADS, 1, 1)
    col_scale = jnp.concatenate([jnp.full((1, dq_w), DIFF_HEAD_DIM ** -0.5 * LOG2E, F32),
                                 jnp.ones((1, main_w - dq_w), F32)], axis=-1)
    row = lambda v: v.reshape(1, -1).astype(F32)

    xs = x.reshape(m, d)
    xm = meta_tokens.astype(x.dtype)

    for l in range(depth):
        lam_init = 0.8 - 0.6 * math.exp(-0.3 * l)
        w_main = _cast_cols(w_in, l, main_w, tr=min(CAST_ROWS, d), tc=CAST_COLS, name="cast_w_in")
        w_kr = _tail_cols(w_in, l, main_w, tr=min(CAST_ROWS, d), name="w_in_rope_cols")
        zpad = jnp.zeros((d, LANES - ROPE_D), F32)
        w_kpe = jnp.concatenate([w_kr, zpad, _rotate_cols(w_kr), zpad], axis=-1).astype(BF16)

        wuq = mla_w_uq[l].reshape(Q_LORA, MLA_HEADS, NOPE_D + ROPE_D)
        w_q = jnp.concatenate([wuq, _rotate_cols(wuq[..., NOPE_D:])], axis=-1)
        w_q = w_q.reshape(Q_LORA, MLA_HEADS * MLA_SLOT).astype(BF16)
        w_ukv = mla_w_ukv[l].astype(BF16)

        xm, wg1, wu1, wd1 = _ffn_meta(xm, row(ffn1_norm[l]), ffn1_w_gate, ffn1_w_up, ffn1_w_down, l,
                                      tf=FFN_CHUNK, name="ffn1_meta")
        xs = _ffn(xs, row(ffn1_norm[l]), wg1, wu1, wd1, tm=FFN_ROWS, tf=FFN_CHUNK, name="ffn1")

        hs = _norm(xs, row(mix_norm[l]), tm=FFN_ROWS, name="mix_norm")
        hm = _norm(xm, row(mix_norm[l]), tm=N_META, name="mix_norm_meta")

        ps, wg2, w_g, w_o = _mm(hs, w_main, tm=ROWS, tn=PROJ_COLS, out_dtype=BF16, scale=col_scale,
                                name="proj", sides=(_side(ffn2_w_gate, l, proj_steps),
                                                    _side(w_gate, l, proj_steps),
                                                    _side(w_out, l, proj_steps)))
        pm = _mm(hm, w_main, tm=N_META, tn=PROJ_COLS, out_dtype=BF16, scale=col_scale, name="proj_meta")
        gates, wu2, w_pa, w_pb = _mm(hs, w_g, tm=ROWS, tn=GATE_COLS, out_dtype=BF16,
                                     bias=row(b_gate[l]), name="gates",
                                     sides=(_side(ffn2_w_up, l, gate_steps),
                                            _side(w_branch_diff, l, gate_steps),
                                            _side(w_branch_mla, l, gate_steps)))
        kpe_s = _kpe(hs, w_kpe, cos_r, sin_r, tm=ROWS, name="kpe")
        kpe_m = _kpe(hm, w_kpe, cos_m, sin_m, tm=N_META, name="kpe_meta")

        q_mla = _mlaq(ps, cq_block, row(mla_q_norm[l]), w_q, cos_r, sin_r, tm=ROWS, heads=MLAQ_HEADS,
                      scale=(NOPE_D + ROPE_D) ** -0.5 * LOG2E, name="mla_q")
        kv_s = _mlakv(ps, ckv_block, row(mla_kv_norm[l]), w_ukv, tm=ROWS, tn=MLAKV_COLS, name="mla_kv")
        kv_m = _mlakv(pm, ckv_block, row(mla_kv_norm[l]), w_ukv, tm=N_META, tn=MLAKV_COLS,
                      name="mla_kv_meta")

        pad_rows = lambda a: jnp.pad(a, ((0, META_PAD - N_META), (0, 0)))
        o_diff = _diff_attn(ps, pad_rows(pm), slopes, row(diff_lambda_q1[l]), row(diff_lambda_k1[l]),
                            row(diff_lambda_q2[l]), row(diff_lambda_k2[l]), row(diff_subln[l]),
                            batch=batch, seq=seq, tq=DIFF_TQ, tk=DIFF_TK, lam_init=lam_init,
                            name="diff_attn")
        o_mla = _mla_attn(q_mla, kv_s, kpe_s, pad_rows(kv_m), pad_rows(kpe_m),
                          batch=batch, seq=seq, tq=MLA_TQ, tk=MLA_TK, name="mla_attn")

        merged, wd2 = _merge(o_diff, o_mla, w_pa, w_pb, gates,
                             _side(ffn2_w_down, l, merge_steps, bc=MERGE_COLS),
                             tm=ROWS, tn=MERGE_COLS, name="merge")
        xs = _mm(merged, w_o, tm=ROWS, tn=OUT_COLS, out_dtype=F32, res=xs, name="out_proj")

        last = l == depth - 1
        xs = _ffn(xs, row(ffn2_norm[l]), wg2, wu2, wd2, tm=FFN_ROWS, tf=FFN_CHUNK,
                  final_gain=row(final_norm) if last else None, name="ffn2")
        if not last:
            raise NotImplementedError("DEPTH > 1 needs the meta-row query path")

    return xs.reshape(batch, seq, d)
```

```python
import functools
import math
from typing import NamedTuple

import numpy as np
import jax
import jax.numpy as jnp
from jax import lax
from jax.experimental import pallas as pl
from jax.experimental.pallas import tpu as pltpu

N_META = 16
EPS = 1e-6
DIFF_HEADS = 8
DIFF_HEAD_DIM = 128
DIFF_V_DIM = 256
MLA_HEADS = 16
Q_LORA = 1024
KV_LORA = 512
NOPE_D = 128
ROPE_D = 64
MLA_V_D = 128
ROPE_THETA = 10000.0

LANES = 128
MLA_SLOT = 256
META_PAD = 128
VMEM_CAP = 60000 * 1024
VMEM_PHYSICAL = 64 * 2**20
VMEM_RESERVE = 2 * 2**20
VMEM_FLOOR = 16 * 2**20
NEG_BIG = -1e30
LOG2E = math.log2(math.e)

ROWS = 1024
PROJ_COLS = 768
GATE_COLS = 1024
MERGE_COLS = 512
OUT_COLS = 512
FFN_ROWS = 512
FFN_CHUNK = 256
CAST_ROWS, CAST_COLS = 1024, 1536
MLAQ_HEADS = 8
MLAKV_COLS = 2048
DIFF_TQ, DIFF_TK = 256, 512
MLA_TQ, MLA_TK = 2048, 512

F32 = jnp.float32
BF16 = jnp.bfloat16


def _nbytes(shape, dtype):
    return int(np.prod(shape)) * jnp.dtype(dtype).itemsize


def _params(semantics, block_bytes, scratch_bytes, temp_bytes):
    need = 2 * block_bytes + scratch_bytes + temp_bytes
    return pltpu.CompilerParams(dimension_semantics=semantics,
                                vmem_limit_bytes=int(min(VMEM_CAP, max(need, VMEM_FLOOR))))


def _rms(x, gain):
    return x * lax.rsqrt(jnp.mean(x * x, axis=-1, keepdims=True) + EPS) * gain


def _ffn_kernel(x_ref, g_ref, wga_ref, wua_ref, wda_ref, wgb_ref, wub_ref, wdb_ref, *rest,
                final_norm, nchunks):
    out_ref, h_scr, x_scr, x_sem = rest[-4:]
    i = pl.program_id(0)
    f = pl.program_id(1)
    last = pl.num_programs(1) - 1
    tm = x_scr.shape[0]

    def x_copy(block):
        return pltpu.make_async_copy(x_ref.at[pl.ds(block * tm, tm), :], x_scr, x_sem)

    def act(wg_ref, wu_ref):
        h = h_scr[...]
        g = jnp.dot(h, wg_ref[...], preferred_element_type=F32)
        u = jnp.dot(h, wu_ref[...], preferred_element_type=F32)
        return (0.5 * (g * jax.nn.sigmoid(g)) * u).astype(BF16)

    def both():
        a = act(wga_ref, wua_ref)
        b = act(wgb_ref, wub_ref)
        return (jnp.dot(a, wda_ref[...], preferred_element_type=F32)
                + jnp.dot(b, wdb_ref[...], preferred_element_type=F32))

    @pl.when((f == 0) & (i == 0))
    def _():
        x_copy(0).start()

    @pl.when(f == 0)
    def _():
        x_copy(i).wait()
        x = x_scr[...]
        h_scr[...] = _rms(x, g_ref[...]).astype(BF16)
        out_ref[...] = x + both()

    @pl.when((f == 1) & (i + 1 < pl.num_programs(0)))
    def _():
        x_copy(i + 1).start()

    @pl.when((f > 0) & (f < last))
    def _():
        out_ref[...] += both()

    @pl.when((f == last) & (f > 0))
    def _():
        if nchunks % 2:
            tail = jnp.dot(act(wga_ref, wua_ref), wda_ref[...], preferred_element_type=F32)
        else:
            tail = both()
        y = out_ref[...] + tail
        out_ref[...] = _rms(y, rest[0][...]) if final_norm else y


def _ffn(x, gain, wg, wu, wd, *, tm, tf, name, final_gain=None):
    m, d = x.shape
    nchunks = wg.shape[1] // tf
    assert nchunks > 2, "step 0 handles two chunks and step 1 prefetches the next rows"
    kern = functools.partial(_ffn_kernel, final_norm=final_gain is not None, nchunks=nchunks)
    vec = pl.BlockSpec((1, d), lambda i, f: (0, 0))
    finals = [] if final_gain is None else [final_gain]
    col_a = lambda i, f: (0, 2 * f)
    col_b = lambda i, f: (0, jnp.minimum(2 * f + 1, nchunks - 1))
    row_a = lambda i, f: (2 * f, 0)
    row_b = lambda i, f: (jnp.minimum(2 * f + 1, nchunks - 1), 0)
    weights = 2 * (2 * _nbytes((d, tf), BF16) + _nbytes((tf, d), BF16))
    need = (_nbytes((tm, d), F32) + 2 * _nbytes((tm, d), F32) + 2 * weights
            + _nbytes((tm, d), BF16) + 6 * _nbytes((tm, tf), F32) + _nbytes((tm, d), F32))
    return pl.pallas_call(
        kern,
        grid=(m // tm, (nchunks + 1) // 2),
        in_specs=[
            pl.BlockSpec(memory_space=pl.ANY),
            vec,
            pl.BlockSpec((d, tf), col_a), pl.BlockSpec((d, tf), col_a), pl.BlockSpec((tf, d), row_a),
            pl.BlockSpec((d, tf), col_b), pl.BlockSpec((d, tf), col_b), pl.BlockSpec((tf, d), row_b),
        ] + [vec] * len(finals),
        out_specs=pl.BlockSpec((tm, d), lambda i, f: (i, 0)),
        out_shape=jax.ShapeDtypeStruct((m, d), F32),
        scratch_shapes=[pltpu.VMEM((tm, d), BF16), pltpu.VMEM((tm, d), F32),
                        pltpu.SemaphoreType.DMA(())],
        compiler_params=pltpu.CompilerParams(
            dimension_semantics=("arbitrary", "arbitrary"),
            vmem_limit_bytes=int(min(VMEM_PHYSICAL - VMEM_RESERVE, max(need, VMEM_FLOOR)))),
        name=name,
    )(x, gain, wg, wu, wd, wg, wu, wd, *finals)


def _ffn_meta_kernel(x_ref, g_ref, wg_ref, wu_ref, wd_ref, out_ref, wg_out, wu_out, wd_out, h_scr):
    @pl.when(pl.program_id(0) == 0)
    def _():
        x = x_ref[...]
        h_scr[...] = _rms(x, g_ref[...]).astype(BF16)
        out_ref[...] = x

    wg = wg_ref[...].astype(BF16)
    wu = wu_ref[...].astype(BF16)
    wd = wd_ref[...].astype(BF16)
    wg_out[...] = wg
    wu_out[...] = wu
    wd_out[...] = wd
    h = h_scr[...]
    g = jnp.dot(h, wg, preferred_element_type=F32)
    u = jnp.dot(h, wu, preferred_element_type=F32)
    a = (0.5 * (g * jax.nn.sigmoid(g)) * u).astype(BF16)
    out_ref[...] += jnp.dot(a, wd, preferred_element_type=F32)


def _ffn_meta(x, gain, wg, wu, wd, l, *, tf, name):
    m, d = x.shape
    ff = wg.shape[2]
    blocks = (2 * _nbytes((m, d), F32) + 3 * _nbytes((d, tf), F32) + 3 * _nbytes((d, tf), BF16))
    return pl.pallas_call(
        _ffn_meta_kernel,
        grid=(ff // tf,),
        in_specs=[pl.BlockSpec((m, d), lambda f: (0, 0)),
                  pl.BlockSpec((1, d), lambda f: (0, 0)),
                  pl.BlockSpec((None, d, tf), lambda f: (l, 0, f)),
                  pl.BlockSpec((None, d, tf), lambda f: (l, 0, f)),
                  pl.BlockSpec((None, tf, d), lambda f: (l, f, 0))],
        out_specs=[pl.BlockSpec((m, d), lambda f: (0, 0)),
                   pl.BlockSpec((d, tf), lambda f: (0, f)),
                   pl.BlockSpec((d, tf), lambda f: (0, f)),
                   pl.BlockSpec((tf, d), lambda f: (f, 0))],
        out_shape=[jax.ShapeDtypeStruct((m, d), F32),
                   jax.ShapeDtypeStruct((d, ff), BF16),
                   jax.ShapeDtypeStruct((d, ff), BF16),
                   jax.ShapeDtypeStruct((ff, d), BF16)],
        scratch_shapes=[pltpu.VMEM((m, d), BF16)],
        compiler_params=_params(("arbitrary",), blocks, _nbytes((m, d), BF16),
                                3 * _nbytes((d, tf), BF16)),
        name=name,
    )(x, gain, wg, wu, wd)


def _norm_kernel(x_ref, g_ref, out_ref):
    out_ref[...] = _rms(x_ref[...], g_ref[...]).astype(out_ref.dtype)


def _norm(x, gain, *, tm, name):
    m, d = x.shape
    blocks = _nbytes((tm, d), F32) + _nbytes((tm, d), BF16)
    return pl.pallas_call(
        _norm_kernel,
        grid=(m // tm,),
        in_specs=[pl.BlockSpec((tm, d), lambda i: (i, 0)),
                  pl.BlockSpec((1, d), lambda i: (0, 0))],
        out_specs=pl.BlockSpec((tm, d), lambda i: (i, 0)),
        out_shape=jax.ShapeDtypeStruct((m, d), BF16),
        compiler_params=_params(("parallel",), blocks, 0, 2 * _nbytes((tm, d), F32)),
        name=name,
    )(x, gain)


def _cast_kernel(w_ref, out_ref):
    out_ref[...] = w_ref[...].astype(out_ref.dtype)


def _cast_cols(w, l, cols, *, tr, tc, name):
    rows = w.shape[1]
    blocks = _nbytes((tr, tc), w.dtype) + _nbytes((tr, tc), BF16)
    return pl.pallas_call(
        _cast_kernel,
        grid=(rows // tr, cols // tc),
        in_specs=[pl.BlockSpec((None, tr, tc), lambda i, j: (l, i, j))],
        out_specs=pl.BlockSpec((tr, tc), lambda i, j: (i, j)),
        out_shape=jax.ShapeDtypeStruct((rows, cols), BF16),
        compiler_params=_params(("parallel", "parallel"), blocks, 0, 0),
        name=name,
    )(w)


def _tail_kernel(w_ref, out_ref, *, valid):
    lane = lax.broadcasted_iota(jnp.int32, out_ref.shape, 1)
    out_ref[...] = jnp.where(lane < valid, w_ref[...], 0.0)


def _tail_cols(w, l, start, *, tr, name):
    rows, cols = w.shape[1:]
    assert start % LANES == 0 and cols - start <= LANES
    blocks = 2 * _nbytes((tr, LANES), F32)
    out = pl.pallas_call(
        functools.partial(_tail_kernel, valid=cols - start),
        grid=(rows // tr,),
        in_specs=[pl.BlockSpec((None, tr, LANES), lambda i: (l, i, start // LANES))],
        out_specs=pl.BlockSpec((tr, LANES), lambda i: (i, 0)),
        out_shape=jax.ShapeDtypeStruct((rows, LANES), F32),
        compiler_params=_params(("parallel",), blocks, 0, 0),
        name=name,
    )(w)
    return out[:, :cols - start]


class SideCast(NamedTuple):
    w: jax.Array
    l: int
    br: int
    bc: int


def _side(w, l, steps, bc=None):
    rows, cols = w.shape[1:]
    bc = cols if bc is None else bc
    sublanes = 16
    br = next(r for r in range(sublanes, rows + 1, sublanes)
              if rows % r == 0 and (rows // r) * (cols // bc) <= steps)
    return SideCast(w, l, br, bc)


def _side_cast_specs(side, grid):
    rows, cols = side.w.shape[1:]
    ncols = cols // side.bc
    nblocks = (rows // side.br) * ncols
    assert rows % side.br == 0 and cols % side.bc == 0 and nblocks <= math.prod(grid), (side.w.shape, grid)

    def block(*idx):
        step = 0
        for i, n in zip(idx, grid):
            step = step * n + i
        t = jnp.minimum(step, nblocks - 1)
        return t // ncols, t % ncols

    in_spec = pl.BlockSpec((None, side.br, side.bc), lambda *idx: (side.l, *block(*idx)))
    out_spec = pl.BlockSpec((side.br, side.bc), block)
    nbytes = _nbytes((side.br, side.bc), F32) + _nbytes((side.br, side.bc), BF16)
    return in_spec, out_spec, jax.ShapeDtypeStruct((rows, cols), BF16), nbytes


def _side_cast_run(side_in_refs, side_out_refs):
    for src, dst in zip(side_in_refs, side_out_refs):
        dst[...] = src[...].astype(dst.dtype)


def _mm_kernel(*refs, has_scale, has_bias, has_res, n_side):
    n_in = 2 + has_scale + has_bias + has_res + n_side
    a_ref, w_ref = refs[0], refs[1]
    extra = list(refs[2:n_in - n_side])
    out_ref = refs[n_in]
    y = jnp.dot(a_ref[...], w_ref[...], preferred_element_type=F32)
    if has_scale:
        y = y * extra.pop(0)[...]
    if has_bias:
        y = jax.nn.sigmoid(y + extra.pop(0)[...])
    if has_res:
        y = y + extra.pop(0)[...]
    out_ref[...] = y.astype(out_ref.dtype)
    _side_cast_run(refs[n_in - n_side:n_in], refs[n_in + 1:])


def _mm(a, w, *, tm, tn, out_dtype, name, scale=None, bias=None, res=None, sides=()):
    m, k = a.shape
    n = w.shape[1]
    grid = (m // tm, n // tn)
    ins = [a, w]
    in_specs = [pl.BlockSpec((tm, k), lambda i, j: (i, 0)),
                pl.BlockSpec((k, tn), lambda i, j: (0, j))]
    blocks = _nbytes((tm, k), a.dtype) + _nbytes((k, tn), w.dtype) + _nbytes((tm, tn), out_dtype)
    for vec in (scale, bias):
        if vec is not None:
            ins.append(vec)
            in_specs.append(pl.BlockSpec((1, tn), lambda i, j: (0, j)))
    if res is not None:
        ins.append(res)
        in_specs.append(pl.BlockSpec((tm, tn), lambda i, j: (i, j)))
        blocks += _nbytes((tm, tn), res.dtype)
    out_specs = [pl.BlockSpec((tm, tn), lambda i, j: (i, j))]
    out_shape = [jax.ShapeDtypeStruct((m, n), out_dtype)]
    for side in sides:
        in_spec, out_spec, shape, nbytes = _side_cast_specs(side, grid)
        ins.append(side.w)
        in_specs.append(in_spec)
        out_specs.append(out_spec)
        out_shape.append(shape)
        blocks += nbytes
    kern = functools.partial(_mm_kernel, has_scale=scale is not None, has_bias=bias is not None,
                             has_res=res is not None, n_side=len(sides))
    outs = pl.pallas_call(
        kern,
        grid=grid,
        in_specs=in_specs,
        out_specs=out_specs,
        out_shape=out_shape,
        compiler_params=_params(("parallel", "arbitrary"), blocks, 0, 4 * _nbytes((tm, tn), F32)),
        name=name,
    )(*ins)
    return outs if sides else outs[0]


def _kpe_kernel(h_ref, w_ref, cos_ref, sin_ref, out_ref):
    y = jnp.dot(h_ref[...], w_ref[...], preferred_element_type=F32)
    out_ref[...] = (y[:, :LANES] * cos_ref[...] + y[:, LANES:] * sin_ref[...]).astype(out_ref.dtype)


def _kpe(h, w, cos, sin, *, tm, name):
    m, d = h.shape
    nt = cos.shape[0] // tm
    blocks = _nbytes((tm, d), BF16) + _nbytes((d, 2 * LANES), BF16) + 3 * _nbytes((tm, LANES), F32)
    return pl.pallas_call(
        _kpe_kernel,
        grid=(m // tm,),
        in_specs=[pl.BlockSpec((tm, d), lambda i: (i, 0)),
                  pl.BlockSpec((d, 2 * LANES), lambda i: (0, 0)),
                  pl.BlockSpec((tm, LANES), lambda i: (i % nt, 0)),
                  pl.BlockSpec((tm, LANES), lambda i: (i % nt, 0))],
        out_specs=pl.BlockSpec((tm, LANES), lambda i: (i, 0)),
        out_shape=jax.ShapeDtypeStruct((m, LANES), BF16),
        compiler_params=_params(("parallel",), blocks, 0, _nbytes((tm, 2 * LANES), F32)),
        name=name,
    )(h, w, cos, sin)


def _mlaq_kernel(c_ref, g_ref, w_ref, cos_ref, sin_ref, out_ref, cn_scr, *, heads, scale):
    @pl.when(pl.program_id(1) == 0)
    def _():
        cn_scr[...] = _rms(c_ref[...].astype(F32), g_ref[...]).astype(BF16)

    y = jnp.dot(cn_scr[...], w_ref[...], preferred_element_type=F32)
    cos = cos_ref[...]
    sin = sin_ref[...]
    for h in range(heads):
        s0 = h * MLA_SLOT
        out_ref[:, s0:s0 + NOPE_D] = (y[:, s0:s0 + NOPE_D] * scale).astype(out_ref.dtype)
        hi = y[:, s0 + NOPE_D:s0 + MLA_SLOT]
        pe = hi * cos + pltpu.roll(hi, LANES // 2, axis=1) * sin
        out_ref[:, s0 + NOPE_D:s0 + MLA_SLOT] = (pe * scale).astype(out_ref.dtype)


def _mlaq(p, col_block, gain, w, cos, sin, *, tm, heads, scale, name):
    m = p.shape[0]
    k = gain.shape[1]
    wn = heads * MLA_SLOT
    groups = w.shape[1] // wn
    nt = cos.shape[0] // tm
    kern = functools.partial(_mlaq_kernel, heads=heads, scale=scale)
    blocks = (_nbytes((tm, k), BF16) + _nbytes((k, wn), BF16) + 2 * _nbytes((tm, LANES), F32)
              + _nbytes((tm, heads * MLA_SLOT), BF16))
    return pl.pallas_call(
        kern,
        grid=(m // tm, groups),
        in_specs=[pl.BlockSpec((tm, k), lambda i, j: (i, col_block)),
                  pl.BlockSpec((1, k), lambda i, j: (0, 0)),
                  pl.BlockSpec((k, wn), lambda i, j: (0, j)),
                  pl.BlockSpec((tm, LANES), lambda i, j: (i % nt, 0)),
                  pl.BlockSpec((tm, LANES), lambda i, j: (i % nt, 0))],
        out_specs=pl.BlockSpec((tm, heads * MLA_SLOT), lambda i, j: (i, j)),
        out_shape=jax.ShapeDtypeStruct((m, groups * heads * MLA_SLOT), BF16),
        scratch_shapes=[pltpu.VMEM((tm, k), BF16)],
        compiler_params=_params(("parallel", "arbitrary"), blocks, _nbytes((tm, k), BF16),
                                2 * _nbytes((tm, wn), F32)),
        name=name,
    )(p, gain, w, cos, sin)


def _mlakv_kernel(c_ref, g_ref, w_ref, out_ref, cn_scr):
    @pl.when(pl.program_id(1) == 0)
    def _():
        cn_scr[...] = _rms(c_ref[...].astype(F32), g_ref[...]).astype(BF16)

    out_ref[...] = jnp.dot(cn_scr[...], w_ref[...], preferred_element_type=F32).astype(out_ref.dtype)


def _mlakv(p, col_block, gain, w, *, tm, tn, name):
    m = p.shape[0]
    k, n = w.shape
    blocks = _nbytes((tm, k), BF16) + _nbytes((k, tn), BF16) + _nbytes((tm, tn), BF16)
    return pl.pallas_call(
        _mlakv_kernel,
        grid=(m // tm, n // tn),
        in_specs=[pl.BlockSpec((tm, k), lambda i, j: (i, col_block)),
                  pl.BlockSpec((1, k), lambda i, j: (0, 0)),
                  pl.BlockSpec((k, tn), lambda i, j: (0, j))],
        out_specs=pl.BlockSpec((tm, tn), lambda i, j: (i, j)),
        out_shape=jax.ShapeDtypeStruct((m, n), BF16),
        scratch_shapes=[pltpu.VMEM((tm, k), BF16)],
        compiler_params=_params(("parallel", "arbitrary"), blocks, _nbytes((tm, k), BF16),
                                2 * _nbytes((tm, tn), F32)),
        name=name,
    )(p, gain, w)


def _dot_nt(a, b):
    return lax.dot_general(a, b, (((1,), (1,)), ((), ())), preferred_element_type=F32)


def _transpose_bf16(x):
    return x.astype(F32).T.astype(BF16)


def _softmax_first_t(s_t, v_t):
    m = jnp.max(s_t, axis=0, keepdims=True)
    p = jnp.exp2(s_t - m)
    return m, jnp.sum(p, axis=0, keepdims=True), jnp.dot(v_t, p.astype(BF16), preferred_element_type=F32)


def _softmax_next_t(s_t, shift, v_t, m, l, acc):
    m_new = jnp.maximum(m, jnp.max(s_t, axis=0, keepdims=True) + shift)
    alpha = jnp.exp2(m - m_new)
    p = jnp.exp2(s_t - (m_new - shift))
    l = alpha * l + jnp.sum(p, axis=0, keepdims=True)
    acc = alpha * acc + jnp.dot(v_t, p.astype(BF16), preferred_element_type=F32)
    return m_new, l, acc


def _meta_mask_t(s_t):
    row = lax.broadcasted_iota(jnp.int32, s_t.shape, 0)
    return jnp.where(row < N_META, s_t, NEG_BIG)


def _diff_attn_kernel(q_ref, k_ref, v_ref, km_ref, vm_ref, slope_ref, lq1_ref, lk1_ref, lq2_ref,
                      lk2_ref, subln_ref, out_ref, vt_scr, bias_scr, *, tk, lam_init):
    tq = q_ref.shape[0]
    ratio = tk // tq
    nk = k_ref.shape[0] // tk
    hd = DIFF_HEAD_DIM
    qi = pl.program_id(2)
    slope2 = slope_ref[...] * LOG2E

    @pl.when(qi == 0)
    def _():
        vt_scr[:, :META_PAD] = _transpose_bf16(vm_ref[...])
        for c in range(nk):
            vt_scr[:, META_PAD + c * tk:META_PAD + (c + 1) * tk] = _transpose_bf16(
                v_ref[c * tk:(c + 1) * tk, :])
        d = (lax.broadcasted_iota(jnp.int32, (tk, tq), 1)
             - lax.broadcasted_iota(jnp.int32, (tk, tq), 0)).astype(F32) * slope2
        bias_scr[0] = -d
        bias_scr[1] = d
        for o in range(ratio):
            bias_scr[2 + o] = -jnp.abs(d + (o * tq) * slope2)

    q1 = q_ref[:, :hd]
    q2 = q_ref[:, hd:]

    def qk(j):
        kb = k_ref[j * tk:(j + 1) * tk, :]
        return _dot_nt(kb[:, :hd], q1), _dot_nt(kb[:, hd:], q2)

    vtm = vt_scr[:, :META_PAD]
    st1 = _softmax_first_t(_meta_mask_t(_dot_nt(km_ref[:, :hd], q1)), vtm)
    st2 = _softmax_first_t(_meta_mask_t(_dot_nt(km_ref[:, hd:], q2)), vtm)

    s1, s2 = qk(0)
    for j in range(nk):
        nxt = qk(j + 1) if j + 1 < nk else None
        off = qi - j * ratio
        bias = bias_scr[jnp.where(off < 0, 1, jnp.where(off >= ratio, 0, 2 + off))]
        gap = jnp.where(off < 0, j * tk - qi * tq, jnp.where(off >= ratio, qi * tq - j * tk, 0))
        shift = -gap.astype(F32) * slope2
        vt = vt_scr[:, META_PAD + j * tk:META_PAD + (j + 1) * tk]
        st1 = _softmax_next_t(s1 + bias, shift, vt, *st1)
        st2 = _softmax_next_t(s2 + bias, shift, vt, *st2)
        if nxt is not None:
            s1, s2 = nxt

    lam = (jnp.exp(jnp.sum(lq1_ref[...] * lk1_ref[...], axis=-1, keepdims=True))
           - jnp.exp(jnp.sum(lq2_ref[...] * lk2_ref[...], axis=-1, keepdims=True)) + lam_init)
    o_t = st1[2] / st1[1] - lam * (st2[2] / st2[1])
    out_ref[...] = (_rms(o_t.T, subln_ref[...]) * (1.0 - lam_init)).astype(out_ref.dtype)


def _diff_attn(p, pm, slopes, lq1, lk1, lq2, lk2, subln, *, batch, seq, tq, tk, lam_init, name):
    assert tk % tq == 0 and seq % tk == 0
    m = p.shape[0]
    hw = 2 * DIFF_HEAD_DIM
    nq = seq // tq
    koff = DIFF_HEADS
    voff = 2 * DIFF_HEADS
    kern = functools.partial(_diff_attn_kernel, tk=tk, lam_init=lam_init)
    n_bias = 2 + tk // tq
    vec = pl.BlockSpec((1, DIFF_HEAD_DIM), lambda b, h, i: (0, 0))
    blocks = (2 * _nbytes((tq, hw), BF16) + 2 * _nbytes((seq, hw), BF16)
              + 2 * _nbytes((META_PAD, hw), BF16))
    scratch = _nbytes((hw, seq + META_PAD), BF16) + n_bias * _nbytes((tk, tq), F32)
    return pl.pallas_call(
        kern,
        grid=(batch, DIFF_HEADS, nq),
        in_specs=[pl.BlockSpec((tq, hw), lambda b, h, i: (b * nq + i, h)),
                  pl.BlockSpec((seq, hw), lambda b, h, i: (b, koff + h)),
                  pl.BlockSpec((seq, hw), lambda b, h, i: (b, voff + h)),
                  pl.BlockSpec((META_PAD, hw), lambda b, h, i: (0, koff + h)),
                  pl.BlockSpec((META_PAD, hw), lambda b, h, i: (0, voff + h)),
                  pl.BlockSpec((None, 1, 1), lambda b, h, i: (h, 0, 0)),
                  vec, vec, vec, vec,
                  pl.BlockSpec((1, hw), lambda b, h, i: (0, 0))],
        out_specs=pl.BlockSpec((tq, hw), lambda b, h, i: (b * nq + i, h)),
        out_shape=jax.ShapeDtypeStruct((m, DIFF_HEADS * hw), BF16),
        scratch_shapes=[pltpu.VMEM((hw, seq + META_PAD), BF16),
                        pltpu.VMEM((n_bias, tk, tq), F32)],
        compiler_params=_params(("parallel", "parallel", "arbitrary"), blocks, scratch,
                                12 * _nbytes((tk, tq), F32)),
        name=name,
    )(p, p, p, pm, pm, slopes, lq1, lk1, lq2, lk2, subln)


def _mla_attn_kernel(q_ref, kv_ref, kpe_ref, kvm_ref, kpem_ref, out_ref, k_scr, vt_scr, *, tk):
    nk = kv_ref.shape[0] // tk

    @pl.when(pl.program_id(2) == 0)
    def _():
        k_scr[:META_PAD, :NOPE_D] = kvm_ref[:, :NOPE_D]
        k_scr[:META_PAD, NOPE_D:] = kpem_ref[...]
        k_scr[META_PAD:, :NOPE_D] = kv_ref[:, :NOPE_D]
        k_scr[META_PAD:, NOPE_D:] = kpe_ref[...]
        vt_scr[:, :META_PAD] = _transpose_bf16(kvm_ref[:, NOPE_D:])
        for c in range(nk):
            vt_scr[:, META_PAD + c * tk:META_PAD + (c + 1) * tk] = _transpose_bf16(
                kv_ref[c * tk:(c + 1) * tk, NOPE_D:])

    q = q_ref[...]

    def qk(j):
        return _dot_nt(k_scr[META_PAD + j * tk:META_PAD + (j + 1) * tk, :], q)

    st = _softmax_first_t(_meta_mask_t(_dot_nt(k_scr[:META_PAD, :], q)), vt_scr[:, :META_PAD])

    s = qk(0)
    for j in range(nk):
        nxt = qk(j + 1) if j + 1 < nk else None
        st = _softmax_next_t(s, 0.0, vt_scr[:, META_PAD + j * tk:META_PAD + (j + 1) * tk], *st)
        s = nxt

    out_ref[...] = (st[2] / st[1]).T.astype(out_ref.dtype)


def _mla_attn(q, kv, kpe, kvm, kpem, *, batch, seq, tq, tk, name):
    mrows = q.shape[0]
    nq = seq // tq
    kern = functools.partial(_mla_attn_kernel, tk=tk)
    blocks = (_nbytes((tq, MLA_SLOT), BF16) + _nbytes((seq, MLA_SLOT), BF16)
              + _nbytes((seq, LANES), BF16) + _nbytes((META_PAD, MLA_SLOT + LANES), BF16)
              + _nbytes((tq, MLA_V_D), BF16))
    scratch = _nbytes((seq + META_PAD, MLA_SLOT), BF16) + _nbytes((MLA_V_D, seq + META_PAD), BF16)
    return pl.pallas_call(
        kern,
        grid=(batch, MLA_HEADS, nq),
        in_specs=[pl.BlockSpec((tq, MLA_SLOT), lambda b, h, i: (b * nq + i, h)),
                  pl.BlockSpec((seq, MLA_SLOT), lambda b, h, i: (b, h)),
                  pl.BlockSpec((seq, LANES), lambda b, h, i: (b, 0)),
                  pl.BlockSpec((META_PAD, MLA_SLOT), lambda b, h, i: (0, h)),
                  pl.BlockSpec((META_PAD, LANES), lambda b, h, i: (0, 0))],
        out_specs=pl.BlockSpec((tq, MLA_V_D), lambda b, h, i: (b * nq + i, h)),
        out_shape=jax.ShapeDtypeStruct((mrows, MLA_HEADS * MLA_V_D), BF16),
        scratch_shapes=[pltpu.VMEM((seq + META_PAD, MLA_SLOT), BF16),
                        pltpu.VMEM((MLA_V_D, seq + META_PAD), BF16)],
        compiler_params=_params(("parallel", "parallel", "arbitrary"), blocks, scratch,
                                6 * _nbytes((tk, tq), F32)),
        name=name,
    )(q, kv, kpe, kvm, kpem)


def _merge_kernel(od_ref, om_ref, wa_ref, wb_ref, gd_ref, gm_ref, side_ref, out_ref, side_out_ref):
    ya = jnp.dot(od_ref[...], wa_ref[...], preferred_element_type=F32)
    yb = jnp.dot(om_ref[...], wb_ref[...], preferred_element_type=F32)
    out_ref[...] = (gd_ref[...].astype(F32) * ya + gm_ref[...].astype(F32) * yb).astype(out_ref.dtype)
    _side_cast_run([side_ref], [side_out_ref])


def _merge(od, om, wa, wb, gates, side, *, tm, tn, name):
    m, k = od.shape
    n = wa.shape[1]
    nj = n // tn
    grid = (m // tm, nj)
    side_in, side_out, side_shape, side_bytes = _side_cast_specs(side, grid)
    blocks = (2 * _nbytes((tm, k), BF16) + 2 * _nbytes((k, tn), BF16) + 3 * _nbytes((tm, tn), BF16)
              + side_bytes)
    return pl.pallas_call(
        _merge_kernel,
        grid=grid,
        in_specs=[pl.BlockSpec((tm, k), lambda i, j: (i, 0)),
                  pl.BlockSpec((tm, k), lambda i, j: (i, 0)),
                  pl.BlockSpec((k, tn), lambda i, j: (0, j)),
                  pl.BlockSpec((k, tn), lambda i, j: (0, j)),
                  pl.BlockSpec((tm, tn), lambda i, j: (i, j)),
                  pl.BlockSpec((tm, tn), lambda i, j: (i, nj + j)),
                  side_in],
        out_specs=[pl.BlockSpec((tm, tn), lambda i, j: (i, j)), side_out],
        out_shape=[jax.ShapeDtypeStruct((m, n), BF16), side_shape],
        compiler_params=_params(("parallel", "arbitrary"), blocks, 0, 4 * _nbytes((tm, tn), F32)),
        name=name,
    )(od, om, wa, wb, gates, gates, side.w)


def _rope_tables(t):
    inv_freq = 1.0 / (ROPE_THETA ** (jnp.arange(0, ROPE_D, 2, dtype=F32) / ROPE_D))
    ang = jnp.arange(t).astype(F32)[:, None] * inv_freq[None, :]
    pad = jnp.zeros((t, LANES - ROPE_D), F32)
    cos = jnp.concatenate([jnp.cos(ang), jnp.cos(ang), pad], axis=-1)
    sin = jnp.concatenate([jnp.sin(ang), jnp.sin(ang), pad], axis=-1)
    return cos, sin


def _rotate_cols(w):
    half = ROPE_D // 2
    return jnp.concatenate([-w[..., half:], w[..., :half]], axis=-1)


def kernel(x, meta_tokens, ffn1_norm, ffn1_w_gate, ffn1_w_up, ffn1_w_down, mix_norm, w_in, diff_lambda_q1, diff_lambda_k1, diff_lambda_q2, diff_lambda_k2, diff_subln, mla_q_norm, mla_w_uq, mla_kv_norm, mla_w_ukv, w_gate, b_gate, w_branch_diff, w_branch_mla, w_out, ffn2_norm, ffn2_w_gate, ffn2_w_up, ffn2_w_down, final_norm):
    batch, seq, d = x.shape
    depth = ffn1_norm.shape[0]
    m = batch * seq
    dq_w = DIFF_HEADS * 2 * DIFF_HEAD_DIM
    dv_w = DIFF_HEADS * DIFF_V_DIM
    main_w = 2 * dq_w + dv_w + Q_LORA + KV_LORA
    cq_block = (2 * dq_w + dv_w) // Q_LORA
    ckv_block = (2 * dq_w + dv_w + Q_LORA) // KV_LORA
    proj_steps = (m // ROWS) * (main_w // PROJ_COLS)
    gate_steps = (m // ROWS) * (2 * d // GATE_COLS)
    merge_steps = (m // ROWS) * (d // MERGE_COLS)

    cos, sin = _rope_tables(N_META + seq)
    cos_m, sin_m, cos_r, sin_r = cos[:N_META], sin[:N_META], cos[N_META:], sin[N_META:]
    slopes = jnp.asarray([2.0 ** (-8.0 * (h + 1) / DIFF_HEADS) for h in range(DIFF_HEADS)],
                         F32).reshape(DIFF_HEADS, 1, 1)
    col_scale = jnp.concatenate([jnp.full((1, dq_w), DIFF_HEAD_DIM ** -0.5 * LOG2E, F32),
                                 jnp.ones((1, main_w - dq_w), F32)], axis=-1)
    row = lambda v: v.reshape(1, -1).astype(F32)

    xs = x.reshape(m, d)
    xm = meta_tokens.astype(x.dtype)

    for l in range(depth):
        lam_init = 0.8 - 0.6 * math.exp(-0.3 * l)
        w_main = _cast_cols(w_in, l, main_w, tr=min(CAST_ROWS, d), tc=CAST_COLS, name="cast_w_in")
        w_kr = _tail_cols(w_in, l, main_w, tr=min(CAST_ROWS, d), name="w_in_rope_cols")
        zpad = jnp.zeros((d, LANES - ROPE_D), F32)
        w_kpe = jnp.concatenate([w_kr, zpad, _rotate_cols(w_kr), zpad], axis=-1).astype(BF16)

        wuq = mla_w_uq[l].reshape(Q_LORA, MLA_HEADS, NOPE_D + ROPE_D)
        w_q = jnp.concatenate([wuq, _rotate_cols(wuq[..., NOPE_D:])], axis=-1)
        w_q = w_q.reshape(Q_LORA, MLA_HEADS * MLA_SLOT).astype(BF16)
        w_ukv = mla_w_ukv[l].astype(BF16)

        xm, wg1, wu1, wd1 = _ffn_meta(xm, row(ffn1_norm[l]), ffn1_w_gate, ffn1_w_up, ffn1_w_down, l,
                                      tf=FFN_CHUNK, name="ffn1_meta")
        xs = _ffn(xs, row(ffn1_norm[l]), wg1, wu1, wd1, tm=FFN_ROWS, tf=FFN_CHUNK, name="ffn1")

        hs = _norm(xs, row(mix_norm[l]), tm=FFN_ROWS, name="mix_norm")
        hm = _norm(xm, row(mix_norm[l]), tm=N_META, name="mix_norm_meta")

        ps, wg2, w_g, w_o = _mm(hs, w_main, tm=ROWS, tn=PROJ_COLS, out_dtype=BF16, scale=col_scale,
                                name="proj", sides=(_side(ffn2_w_gate, l, proj_steps),
                                                    _side(w_gate, l, proj_steps),
                                                    _side(w_out, l, proj_steps)))
        pm = _mm(hm, w_main, tm=N_META, tn=PROJ_COLS, out_dtype=BF16, scale=col_scale, name="proj_meta")
        gates, wu2, w_pa, w_pb = _mm(hs, w_g, tm=ROWS, tn=GATE_COLS, out_dtype=BF16,
                                     bias=row(b_gate[l]), name="gates",
                                     sides=(_side(ffn2_w_up, l, gate_steps),
                                            _side(w_branch_diff, l, gate_steps),
                                            _side(w_branch_mla, l, gate_steps)))
        kpe_s = _kpe(hs, w_kpe, cos_r, sin_r, tm=ROWS, name="kpe")
        kpe_m = _kpe(hm, w_kpe, cos_m, sin_m, tm=N_META, name="kpe_meta")

        q_mla = _mlaq(ps, cq_block, row(mla_q_norm[l]), w_q, cos_r, sin_r, tm=ROWS, heads=MLAQ_HEADS,
                      scale=(NOPE_D + ROPE_D) ** -0.5 * LOG2E, name="mla_q")
        kv_s = _mlakv(ps, ckv_block, row(mla_kv_norm[l]), w_ukv, tm=ROWS, tn=MLAKV_COLS, name="mla_kv")
        kv_m = _mlakv(pm, ckv_block, row(mla_kv_norm[l]), w_ukv, tm=N_META, tn=MLAKV_COLS,
                      name="mla_kv_meta")

        pad_rows = lambda a: jnp.pad(a, ((0, META_PAD - N_META), (0, 0)))
        o_diff = _diff_attn(ps, pad_rows(pm), slopes, row(diff_lambda_q1[l]), row(diff_lambda_k1[l]),
                            row(diff_lambda_q2[l]), row(diff_lambda_k2[l]), row(diff_subln[l]),
                            batch=batch, seq=seq, tq=DIFF_TQ, tk=DIFF_TK, lam_init=lam_init,
                            name="diff_attn")
        o_mla = _mla_attn(q_mla, kv_s, kpe_s, pad_rows(kv_m), pad_rows(kpe_m),
                          batch=batch, seq=seq, tq=MLA_TQ, tk=MLA_TK, name="mla_attn")

        merged, wd2 = _merge(o_diff, o_mla, w_pa, w_pb, gates,
                             _side(ffn2_w_down, l, merge_steps, bc=MERGE_COLS),
                             tm=ROWS, tn=MERGE_COLS, name="merge")
        xs = _mm(merged, w_o, tm=ROWS, tn=OUT_COLS, out_dtype=F32, res=xs, name="out_proj")

        last = l == depth - 1
        xs = _ffn(xs, row(ffn2_norm[l]), wg2, wu2, wd2, tm=FFN_ROWS, tf=FFN_CHUNK,
                  final_gain=row(final_norm) if last else None, name="ffn2")
        if not last:
            raise NotImplementedError("DEPTH > 1 needs the meta-row query path")

    return xs.reshape(batch, seq, d)
```

```python
import functools
import math
from typing import NamedTuple

import numpy as np
import jax
import jax.numpy as jnp
from jax import lax
from jax.experimental import pallas as pl
from jax.experimental.pallas import tpu as pltpu

N_META = 16
EPS = 1e-6
DIFF_HEADS = 8
DIFF_HEAD_DIM = 128
DIFF_V_DIM = 256
MLA_HEADS = 16
Q_LORA = 1024
KV_LORA = 512
NOPE_D = 128
ROPE_D = 64
MLA_V_D = 128
ROPE_THETA = 10000.0

LANES = 128
MLA_SLOT = 256
META_PAD = 128
VMEM_CAP = 60000 * 1024
VMEM_PHYSICAL = 64 * 2**20
VMEM_RESERVE = 2 * 2**20
VMEM_FLOOR = 16 * 2**20
NEG_BIG = -1e30
LOG2E = math.log2(math.e)

ROWS = 1024
PROJ_COLS = 768
GATE_COLS = 1024
MERGE_COLS = 512
OUT_COLS = 512
FFN_ROWS = 512
FFN_CHUNK = 256
CAST_ROWS, CAST_COLS = 1024, 1536
MLAQ_HEADS = 8
MLAKV_COLS = 2048
DIFF_TQ, DIFF_TK = 256, 512
MLA_TQ, MLA_TK = 2048, 512

F32 = jnp.float32
BF16 = jnp.bfloat16


def _nbytes(shape, dtype):
    return int(np.prod(shape)) * jnp.dtype(dtype).itemsize


def _params(semantics, block_bytes, scratch_bytes, temp_bytes):
    need = 2 * block_bytes + scratch_bytes + temp_bytes
    return pltpu.CompilerParams(dimension_semantics=semantics,
                                vmem_limit_bytes=int(min(VMEM_CAP, max(need, VMEM_FLOOR))))


def _rms(x, gain):
    return x * lax.rsqrt(jnp.mean(x * x, axis=-1, keepdims=True) + EPS) * gain


def _ffn_kernel(x_ref, g_ref, wga_ref, wua_ref, wda_ref, wgb_ref, wub_ref, wdb_ref, *rest,
                final_norm, nchunks):
    out_ref, h_scr, x_scr, x_sem = rest[-4:]
    i = pl.program_id(0)
    f = pl.program_id(1)
    last = pl.num_programs(1) - 1
    tm = x_scr.shape[0]

    def x_copy(block):
        return pltpu.make_async_copy(x_ref.at[pl.ds(block * tm, tm), :], x_scr, x_sem)

    def act(wg_ref, wu_ref):
        h = h_scr[...]
        g = jnp.dot(h, wg_ref[...], preferred_element_type=F32)
        u = jnp.dot(h, wu_ref[...], preferred_element_type=F32)
        return (0.5 * (g * jax.nn.sigmoid(g)) * u).astype(BF16)

    def both():
        a = act(wga_ref, wua_ref)
        b = act(wgb_ref, wub_ref)
        return (jnp.dot(a, wda_ref[...], preferred_element_type=F32)
                + jnp.dot(b, wdb_ref[...], preferred_element_type=F32))

    @pl.when((f == 0) & (i == 0))
    def _():
        x_copy(0).start()

    @pl.when(f == 0)
    def _():
        x_copy(i).wait()
        x = x_scr[...]
        h_scr[...] = _rms(x, g_ref[...]).astype(BF16)
        out_ref[...] = x + both()

    @pl.when((f == 1) & (i + 1 < pl.num_programs(0)))
    def _():
        x_copy(i + 1).start()

    @pl.when((f > 0) & (f < last))
    def _():
        out_ref[...] += both()

    @pl.when((f == last) & (f > 0))
    def _():
        if nchunks % 2:
            tail = jnp.dot(act(wga_ref, wua_ref), wda_ref[...], preferred_element_type=F32)
        else:
            tail = both()
        y = out_ref[...] + tail
        out_ref[...] = _rms(y, rest[0][...]) if final_norm else y


def _ffn(x, gain, wg, wu, wd, *, tm, tf, name, final_gain=None):
    m, d = x.shape
    nchunks = wg.shape[1] // tf
    assert nchunks > 2, "step 0 handles two chunks and step 1 prefetches the next rows"
    kern = functools.partial(_ffn_kernel, final_norm=final_gain is not None, nchunks=nchunks)
    vec = pl.BlockSpec((1, d), lambda i, f: (0, 0))
    finals = [] if final_gain is None else [final_gain]
    col_a = lambda i, f: (0, 2 * f)
    col_b = lambda i, f: (0, jnp.minimum(2 * f + 1, nchunks - 1))
    row_a = lambda i, f: (2 * f, 0)
    row_b = lambda i, f: (jnp.minimum(2 * f + 1, nchunks - 1), 0)
    weights = 2 * (2 * _nbytes((d, tf), BF16) + _nbytes((tf, d), BF16))
    need = (_nbytes((tm, d), F32) + 2 * _nbytes((tm, d), F32) + 2 * weights
            + _nbytes((tm, d), BF16) + 6 * _nbytes((tm, tf), F32) + _nbytes((tm, d), F32))
    return pl.pallas_call(
        kern,
        grid=(m // tm, (nchunks + 1) // 2),
        in_specs=[
            pl.BlockSpec(memory_space=pl.ANY),
            vec,
            pl.BlockSpec((d, tf), col_a), pl.BlockSpec((d, tf), col_a), pl.BlockSpec((tf, d), row_a),
            pl.BlockSpec((d, tf), col_b), pl.BlockSpec((d, tf), col_b), pl.BlockSpec((tf, d), row_b),
        ] + [vec] * len(finals),
        out_specs=pl.BlockSpec((tm, d), lambda i, f: (i, 0)),
        out_shape=jax.ShapeDtypeStruct((m, d), F32),
        scratch_shapes=[pltpu.VMEM((tm, d), BF16), pltpu.VMEM((tm, d), F32),
                        pltpu.SemaphoreType.DMA(())],
        compiler_params=pltpu.CompilerParams(
            dimension_semantics=("arbitrary", "arbitrary"),
            vmem_limit_bytes=int(min(VMEM_PHYSICAL - VMEM_RESERVE, max(need, VMEM_FLOOR)))),
        name=name,
    )(x, gain, wg, wu, wd, wg, wu, wd, *finals)


def _ffn_meta_kernel(x_ref, g_ref, wg_ref, wu_ref, wd_ref, out_ref, wg_out, wu_out, wd_out, h_scr):
    @pl.when(pl.program_id(0) == 0)
    def _():
        x = x_ref[...]
        h_scr[...] = _rms(x, g_ref[...]).astype(BF16)
        out_ref[...] = x

    wg = wg_ref[...].astype(BF16)
    wu = wu_ref[...].astype(BF16)
    wd = wd_ref[...].astype(BF16)
    wg_out[...] = wg
    wu_out[...] = wu
    wd_out[...] = wd
    h = h_scr[...]
    g = jnp.dot(h, wg, preferred_element_type=F32)
    u = jnp.dot(h, wu, preferred_element_type=F32)
    a = (0.5 * (g * jax.nn.sigmoid(g)) * u).astype(BF16)
    out_ref[...] += jnp.dot(a, wd, preferred_element_type=F32)


def _ffn_meta(x, gain, wg, wu, wd, l, *, tf, name):
    m, d = x.shape
    ff = wg.shape[2]
    blocks = (2 * _nbytes((m, d), F32) + 3 * _nbytes((d, tf), F32) + 3 * _nbytes((d, tf), BF16))
    return pl.pallas_call(
        _ffn_meta_kernel,
        grid=(ff // tf,),
        in_specs=[pl.BlockSpec((m, d), lambda f: (0, 0)),
                  pl.BlockSpec((1, d), lambda f: (0, 0)),
                  pl.BlockSpec((None, d, tf), lambda f: (l, 0, f)),
                  pl.BlockSpec((None, d, tf), lambda f: (l, 0, f)),
                  pl.BlockSpec((None, tf, d), lambda f: (l, f, 0))],
        out_specs=[pl.BlockSpec((m, d), lambda f: (0, 0)),
                   pl.BlockSpec((d, tf), lambda f: (0, f)),
                   pl.BlockSpec((d, tf), lambda f: (0, f)),
                   pl.BlockSpec((tf, d), lambda f: (f, 0))],
        out_shape=[jax.ShapeDtypeStruct((m, d), F32),
                   jax.ShapeDtypeStruct((d, ff), BF16),
                   jax.ShapeDtypeStruct((d, ff), BF16),
                   jax.ShapeDtypeStruct((ff, d), BF16)],
        scratch_shapes=[pltpu.VMEM((m, d), BF16)],
        compiler_params=_params(("arbitrary",), blocks, _nbytes((m, d), BF16),
                                3 * _nbytes((d, tf), BF16)),
        name=name,
    )(x, gain, wg, wu, wd)


def _norm_kpe_kernel(x_ref, g_ref, w_ref, cos_ref, sin_ref, h_ref, kpe_ref):
    h = _rms(x_ref[...], g_ref[...]).astype(h_ref.dtype)
    h_ref[...] = h
    y = jnp.dot(h, w_ref[...], preferred_element_type=F32)
    kpe_ref[...] = (y[:, :LANES] * cos_ref[...] + y[:, LANES:] * sin_ref[...]).astype(kpe_ref.dtype)


def _norm_kpe(x, gain, w, cos, sin, *, tm, name):
    m, d = x.shape
    nt = cos.shape[0] // tm
    blocks = (_nbytes((tm, d), F32) + _nbytes((tm, d), BF16) + _nbytes((d, 2 * LANES), BF16)
              + 3 * _nbytes((tm, LANES), F32))
    return pl.pallas_call(
        _norm_kpe_kernel,
        grid=(m // tm,),
        in_specs=[pl.BlockSpec((tm, d), lambda i: (i, 0)),
                  pl.BlockSpec((1, d), lambda i: (0, 0)),
                  pl.BlockSpec((d, 2 * LANES), lambda i: (0, 0)),
                  pl.BlockSpec((tm, LANES), lambda i: (i % nt, 0)),
                  pl.BlockSpec((tm, LANES), lambda i: (i % nt, 0))],
        out_specs=[pl.BlockSpec((tm, d), lambda i: (i, 0)),
                   pl.BlockSpec((tm, LANES), lambda i: (i, 0))],
        out_shape=[jax.ShapeDtypeStruct((m, d), BF16), jax.ShapeDtypeStruct((m, LANES), BF16)],
        compiler_params=_params(("parallel",), blocks, 0, 2 * _nbytes((tm, d), F32)),
        name=name,
    )(x, gain, w, cos, sin)


def _cast_kernel(w_ref, out_ref):
    out_ref[...] = w_ref[...].astype(out_ref.dtype)


def _cast_cols(w, l, cols, *, tr, tc, name):
    rows = w.shape[1]
    blocks = _nbytes((tr, tc), w.dtype) + _nbytes((tr, tc), BF16)
    return pl.pallas_call(
        _cast_kernel,
        grid=(rows // tr, cols // tc),
        in_specs=[pl.BlockSpec((None, tr, tc), lambda i, j: (l, i, j))],
        out_specs=pl.BlockSpec((tr, tc), lambda i, j: (i, j)),
        out_shape=jax.ShapeDtypeStruct((rows, cols), BF16),
        compiler_params=_params(("parallel", "parallel"), blocks, 0, 0),
        name=name,
    )(w)


def _tail_kernel(w_ref, out_ref, *, valid):
    lane = lax.broadcasted_iota(jnp.int32, out_ref.shape, 1)
    out_ref[...] = jnp.where(lane < valid, w_ref[...], 0.0)


def _tail_cols(w, l, start, *, tr, name):
    rows, cols = w.shape[1:]
    assert start % LANES == 0 and cols - start <= LANES
    blocks = 2 * _nbytes((tr, LANES), F32)
    out = pl.pallas_call(
        functools.partial(_tail_kernel, valid=cols - start),
        grid=(rows // tr,),
        in_specs=[pl.BlockSpec((None, tr, LANES), lambda i: (l, i, start // LANES))],
        out_specs=pl.BlockSpec((tr, LANES), lambda i: (i, 0)),
        out_shape=jax.ShapeDtypeStruct((rows, LANES), F32),
        compiler_params=_params(("parallel",), blocks, 0, 0),
        name=name,
    )(w)
    return out[:, :cols - start]


class SideCast(NamedTuple):
    w: jax.Array
    l: int
    br: int
    bc: int


def _side(w, l, steps, bc=None):
    rows, cols = w.shape[1:]
    bc = cols if bc is None else bc
    sublanes = 16
    br = next(r for r in range(sublanes, rows + 1, sublanes)
              if rows % r == 0 and (rows // r) * (cols // bc) <= steps)
    return SideCast(w, l, br, bc)


def _side_cast_specs(side, grid):
    rows, cols = side.w.shape[1:]
    ncols = cols // side.bc
    nblocks = (rows // side.br) * ncols
    assert rows % side.br == 0 and cols % side.bc == 0 and nblocks <= math.prod(grid), (side.w.shape, grid)

    def block(*idx):
        step = 0
        for i, n in zip(idx, grid):
            step = step * n + i
        t = jnp.minimum(step, nblocks - 1)
        return t // ncols, t % ncols

    in_spec = pl.BlockSpec((None, side.br, side.bc), lambda *idx: (side.l, *block(*idx)))
    out_spec = pl.BlockSpec((side.br, side.bc), block)
    nbytes = _nbytes((side.br, side.bc), F32) + _nbytes((side.br, side.bc), BF16)
    return in_spec, out_spec, jax.ShapeDtypeStruct((rows, cols), BF16), nbytes


def _side_cast_run(side_in_refs, side_out_refs):
    for src, dst in zip(side_in_refs, side_out_refs):
        dst[...] = src[...].astype(dst.dtype)


def _mm_kernel(*refs, has_scale, has_bias, has_res, n_side):
    n_in = 2 + has_scale + has_bias + has_res + n_side
    a_ref, w_ref = refs[0], refs[1]
    extra = list(refs[2:n_in - n_side])
    out_ref = refs[n_in]
    y = jnp.dot(a_ref[...], w_ref[...], preferred_element_type=F32)
    if has_scale:
        y = y * extra.pop(0)[...]
    if has_bias:
        y = jax.nn.sigmoid(y + extra.pop(0)[...])
    if has_res:
        y = y + extra.pop(0)[...]
    out_ref[...] = y.astype(out_ref.dtype)
    _side_cast_run(refs[n_in - n_side:n_in], refs[n_in + 1:])


def _mm(a, w, *, tm, tn, out_dtype, name, scale=None, bias=None, res=None, sides=()):
    m, k = a.shape
    n = w.shape[1]
    grid = (m // tm, n // tn)
    ins = [a, w]
    in_specs = [pl.BlockSpec((tm, k), lambda i, j: (i, 0)),
                pl.BlockSpec((k, tn), lambda i, j: (0, j))]
    blocks = _nbytes((tm, k), a.dtype) + _nbytes((k, tn), w.dtype) + _nbytes((tm, tn), out_dtype)
    for vec in (scale, bias):
        if vec is not None:
            ins.append(vec)
            in_specs.append(pl.BlockSpec((1, tn), lambda i, j: (0, j)))
    if res is not None:
        ins.append(res)
        in_specs.append(pl.BlockSpec((tm, tn), lambda i, j: (i, j)))
        blocks += _nbytes((tm, tn), res.dtype)
    out_specs = [pl.BlockSpec((tm, tn), lambda i, j: (i, j))]
    out_shape = [jax.ShapeDtypeStruct((m, n), out_dtype)]
    for side in sides:
        in_spec, out_spec, shape, nbytes = _side_cast_specs(side, grid)
        ins.append(side.w)
        in_specs.append(in_spec)
        out_specs.append(out_spec)
        out_shape.append(shape)
        blocks += nbytes
    kern = functools.partial(_mm_kernel, has_scale=scale is not None, has_bias=bias is not None,
                             has_res=res is not None, n_side=len(sides))
    outs = pl.pallas_call(
        kern,
        grid=grid,
        in_specs=in_specs,
        out_specs=out_specs,
        out_shape=out_shape,
        compiler_params=_params(("parallel", "arbitrary"), blocks, 0, 4 * _nbytes((tm, tn), F32)),
        name=name,
    )(*ins)
    return outs if sides else outs[0]


def _mlaq_kernel(c_ref, g_ref, w_ref, cos_ref, sin_ref, out_ref, cn_scr, *, heads, scale):
    @pl.when(pl.program_id(1) == 0)
    def _():
        cn_scr[...] = _rms(c_ref[...].astype(F32), g_ref[...]).astype(BF16)

    y = jnp.dot(cn_scr[...], w_ref[...], preferred_element_type=F32)
    cos = cos_ref[...]
    sin = sin_ref[...]
    for h in range(heads):
        s0 = h * MLA_SLOT
        out_ref[:, s0:s0 + NOPE_D] = (y[:, s0:s0 + NOPE_D] * scale).astype(out_ref.dtype)
        hi = y[:, s0 + NOPE_D:s0 + MLA_SLOT]
        pe = hi * cos + pltpu.roll(hi, LANES // 2, axis=1) * sin
        out_ref[:, s0 + NOPE_D:s0 + MLA_SLOT] = (pe * scale).astype(out_ref.dtype)


def _mlaq(p, col_block, gain, w, cos, sin, *, tm, heads, scale, name):
    m = p.shape[0]
    k = gain.shape[1]
    wn = heads * MLA_SLOT
    groups = w.shape[1] // wn
    nt = cos.shape[0] // tm
    kern = functools.partial(_mlaq_kernel, heads=heads, scale=scale)
    blocks = (_nbytes((tm, k), BF16) + _nbytes((k, wn), BF16) + 2 * _nbytes((tm, LANES), F32)
              + _nbytes((tm, heads * MLA_SLOT), BF16))
    return pl.pallas_call(
        kern,
        grid=(m // tm, groups),
        in_specs=[pl.BlockSpec((tm, k), lambda i, j: (i, col_block)),
                  pl.BlockSpec((1, k), lambda i, j: (0, 0)),
                  pl.BlockSpec((k, wn), lambda i, j: (0, j)),
                  pl.BlockSpec((tm, LANES), lambda i, j: (i % nt, 0)),
                  pl.BlockSpec((tm, LANES), lambda i, j: (i % nt, 0))],
        out_specs=pl.BlockSpec((tm, heads * MLA_SLOT), lambda i, j: (i, j)),
        out_shape=jax.ShapeDtypeStruct((m, groups * heads * MLA_SLOT), BF16),
        scratch_shapes=[pltpu.VMEM((tm, k), BF16)],
        compiler_params=_params(("parallel", "arbitrary"), blocks, _nbytes((tm, k), BF16),
                                2 * _nbytes((tm, wn), F32)),
        name=name,
    )(p, gain, w, cos, sin)


def _mlakv_kernel(c_ref, g_ref, w_ref, out_ref, cn_scr):
    @pl.when(pl.program_id(1) == 0)
    def _():
        cn_scr[...] = _rms(c_ref[...].astype(F32), g_ref[...]).astype(BF16)

    out_ref[...] = jnp.dot(cn_scr[...], w_ref[...], preferred_element_type=F32).astype(out_ref.dtype)


def _mlakv(p, col_block, gain, w, *, tm, tn, name):
    m = p.shape[0]
    k, n = w.shape
    blocks = _nbytes((tm, k), BF16) + _nbytes((k, tn), BF16) + _nbytes((tm, tn), BF16)
    return pl.pallas_call(
        _mlakv_kernel,
        grid=(m // tm, n // tn),
        in_specs=[pl.BlockSpec((tm, k), lambda i, j: (i, col_block)),
                  pl.BlockSpec((1, k), lambda i, j: (0, 0)),
                  pl.BlockSpec((k, tn), lambda i, j: (0, j))],
        out_specs=pl.BlockSpec((tm, tn), lambda i, j: (i, j)),
        out_shape=jax.ShapeDtypeStruct((m, n), BF16),
        scratch_shapes=[pltpu.VMEM((tm, k), BF16)],
        compiler_params=_params(("parallel", "arbitrary"), blocks, _nbytes((tm, k), BF16),
                                2 * _nbytes((tm, tn), F32)),
        name=name,
    )(p, gain, w)


def _dot_nt(a, b):
    return lax.dot_general(a, b, (((1,), (1,)), ((), ())), preferred_element_type=F32)


def _transpose_bf16(x):
    return x.astype(F32).T.astype(BF16)


def _softmax_first_t(s_t, v_t):
    m = jnp.max(s_t, axis=0, keepdims=True)
    p = jnp.exp2(s_t - m)
    return m, jnp.sum(p, axis=0, keepdims=True), jnp.dot(v_t, p.astype(BF16), preferred_element_type=F32)


def _softmax_next_t(s_t, shift, v_t, m, l, acc):
    m_new = jnp.maximum(m, jnp.max(s_t, axis=0, keepdims=True) + shift)
    alpha = jnp.exp2(m - m_new)
    p = jnp.exp2(s_t - (m_new - shift))
    l = alpha * l + jnp.sum(p, axis=0, keepdims=True)
    acc = alpha * acc + jnp.dot(v_t, p.astype(BF16), preferred_element_type=F32)
    return m_new, l, acc


def _meta_mask_t(s_t):
    row = lax.broadcasted_iota(jnp.int32, s_t.shape, 0)
    return jnp.where(row < N_META, s_t, NEG_BIG)


def _diff_attn_kernel(q_ref, k_ref, v_ref, km_ref, vm_ref, slope_ref, lq1_ref, lk1_ref, lq2_ref,
                      lk2_ref, subln_ref, out_ref, vt_scr, bias_scr, *, tk, lam_init):
    tq = q_ref.shape[0]
    ratio = tk // tq
    nk = k_ref.shape[0] // tk
    hd = DIFF_HEAD_DIM
    qi = pl.program_id(2)
    slope2 = slope_ref[...] * LOG2E

    @pl.when(qi == 0)
    def _():
        vt_scr[:, :META_PAD] = _transpose_bf16(vm_ref[...])
        for c in range(nk):
            vt_scr[:, META_PAD + c * tk:META_PAD + (c + 1) * tk] = _transpose_bf16(
                v_ref[c * tk:(c + 1) * tk, :])
        d = (lax.broadcasted_iota(jnp.int32, (tk, tq), 1)
             - lax.broadcasted_iota(jnp.int32, (tk, tq), 0)).astype(F32) * slope2
        bias_scr[0] = -d
        bias_scr[1] = d
        for o in range(ratio):
            bias_scr[2 + o] = -jnp.abs(d + (o * tq) * slope2)

    q1 = q_ref[:, :hd]
    q2 = q_ref[:, hd:]

    def qk(j):
        kb = k_ref[j * tk:(j + 1) * tk, :]
        return _dot_nt(kb[:, :hd], q1), _dot_nt(kb[:, hd:], q2)

    vtm = vt_scr[:, :META_PAD]
    st1 = _softmax_first_t(_meta_mask_t(_dot_nt(km_ref[:, :hd], q1)), vtm)
    st2 = _softmax_first_t(_meta_mask_t(_dot_nt(km_ref[:, hd:], q2)), vtm)

    s1, s2 = qk(0)
    for j in range(nk):
        nxt = qk(j + 1) if j + 1 < nk else None
        off = qi - j * ratio
        bias = bias_scr[jnp.where(off < 0, 1, jnp.where(off >= ratio, 0, 2 + off))]
        gap = jnp.where(off < 0, j * tk - qi * tq, jnp.where(off >= ratio, qi * tq - j * tk, 0))
        shift = -gap.astype(F32) * slope2
        vt = vt_scr[:, META_PAD + j * tk:META_PAD + (j + 1) * tk]
        st1 = _softmax_next_t(s1 + bias, shift, vt, *st1)
        st2 = _softmax_next_t(s2 + bias, shift, vt, *st2)
        if nxt is not None:
            s1, s2 = nxt

    lam = (jnp.exp(jnp.sum(lq1_ref[...] * lk1_ref[...], axis=-1, keepdims=True))
           - jnp.exp(jnp.sum(lq2_ref[...] * lk2_ref[...], axis=-1, keepdims=True)) + lam_init)
    o_t = st1[2] / st1[1] - lam * (st2[2] / st2[1])
    out_ref[...] = (_rms(o_t.T, subln_ref[...]) * (1.0 - lam_init)).astype(out_ref.dtype)


def _diff_attn(p, pm, slopes, lq1, lk1, lq2, lk2, subln, *, batch, seq, tq, tk, lam_init, name):
    assert tk % tq == 0 and seq % tk == 0
    m = p.shape[0]
    hw = 2 * DIFF_HEAD_DIM
    nq = seq // tq
    koff = DIFF_HEADS
    voff = 2 * DIFF_HEADS
    kern = functools.partial(_diff_attn_kernel, tk=tk, lam_init=lam_init)
    n_bias = 2 + tk // tq
    vec = pl.BlockSpec((1, DIFF_HEAD_DIM), lambda b, h, i: (0, 0))
    blocks = (2 * _nbytes((tq, hw), BF16) + 2 * _nbytes((seq, hw), BF16)
              + 2 * _nbytes((META_PAD, hw), BF16))
    scratch = _nbytes((hw, seq + META_PAD), BF16) + n_bias * _nbytes((tk, tq), F32)
    return pl.pallas_call(
        kern,
        grid=(batch, DIFF_HEADS, nq),
        in_specs=[pl.BlockSpec((tq, hw), lambda b, h, i: (b * nq + i, h)),
                  pl.BlockSpec((seq, hw), lambda b, h, i: (b, koff + h)),
                  pl.BlockSpec((seq, hw), lambda b, h, i: (b, voff + h)),
                  pl.BlockSpec((META_PAD, hw), lambda b, h, i: (0, koff + h)),
                  pl.BlockSpec((META_PAD, hw), lambda b, h, i: (0, voff + h)),
                  pl.BlockSpec((None, 1, 1), lambda b, h, i: (h, 0, 0)),
                  vec, vec, vec, vec,
                  pl.BlockSpec((1, hw), lambda b, h, i: (0, 0))],
        out_specs=pl.BlockSpec((tq, hw), lambda b, h, i: (b * nq + i, h)),
        out_shape=jax.ShapeDtypeStruct((m, DIFF_HEADS * hw), BF16),
        scratch_shapes=[pltpu.VMEM((hw, seq + META_PAD), BF16),
                        pltpu.VMEM((n_bias, tk, tq), F32)],
        compiler_params=_params(("parallel", "parallel", "arbitrary"), blocks, scratch,
                                12 * _nbytes((tk, tq), F32)),
        name=name,
    )(p, p, p, pm, pm, slopes, lq1, lk1, lq2, lk2, subln)


def _mla_attn_kernel(q_ref, kv_ref, kpe_ref, kvm_ref, kpem_ref, out_ref, k_scr, vt_scr, *, tk):
    nk = kv_ref.shape[0] // tk

    @pl.when(pl.program_id(2) == 0)
    def _():
        k_scr[:META_PAD, :NOPE_D] = kvm_ref[:, :NOPE_D]
        k_scr[:META_PAD, NOPE_D:] = kpem_ref[...]
        k_scr[META_PAD:, :NOPE_D] = kv_ref[:, :NOPE_D]
        k_scr[META_PAD:, NOPE_D:] = kpe_ref[...]
        vt_scr[:, :META_PAD] = _transpose_bf16(kvm_ref[:, NOPE_D:])
        for c in range(nk):
            vt_scr[:, META_PAD + c * tk:META_PAD + (c + 1) * tk] = _transpose_bf16(
                kv_ref[c * tk:(c + 1) * tk, NOPE_D:])

    q = q_ref[...]

    def qk(j):
        return _dot_nt(k_scr[META_PAD + j * tk:META_PAD + (j + 1) * tk, :], q)

    st = _softmax_first_t(_meta_mask_t(_dot_nt(k_scr[:META_PAD, :], q)), vt_scr[:, :META_PAD])

    s = qk(0)
    for j in range(nk):
        nxt = qk(j + 1) if j + 1 < nk else None
        st = _softmax_next_t(s, 0.0, vt_scr[:, META_PAD + j * tk:META_PAD + (j + 1) * tk], *st)
        s = nxt

    out_ref[...] = (st[2] / st[1]).T.astype(out_ref.dtype)


def _mla_attn(q, kv, kpe, kvm, kpem, *, batch, seq, tq, tk, name):
    mrows = q.shape[0]
    nq = seq // tq
    kern = functools.partial(_mla_attn_kernel, tk=tk)
    blocks = (_nbytes((tq, MLA_SLOT), BF16) + _nbytes((seq, MLA_SLOT), BF16)
              + _nbytes((seq, LANES), BF16) + _nbytes((META_PAD, MLA_SLOT + LANES), BF16)
              + _nbytes((tq, MLA_V_D), BF16))
    scratch = _nbytes((seq + META_PAD, MLA_SLOT), BF16) + _nbytes((MLA_V_D, seq + META_PAD), BF16)
    return pl.pallas_call(
        kern,
        grid=(batch, MLA_HEADS, nq),
        in_specs=[pl.BlockSpec((tq, MLA_SLOT), lambda b, h, i: (b * nq + i, h)),
                  pl.BlockSpec((seq, MLA_SLOT), lambda b, h, i: (b, h)),
                  pl.BlockSpec((seq, LANES), lambda b, h, i: (b, 0)),
                  pl.BlockSpec((META_PAD, MLA_SLOT), lambda b, h, i: (0, h)),
                  pl.BlockSpec((META_PAD, LANES), lambda b, h, i: (0, 0))],
        out_specs=pl.BlockSpec((tq, MLA_V_D), lambda b, h, i: (b * nq + i, h)),
        out_shape=jax.ShapeDtypeStruct((mrows, MLA_HEADS * MLA_V_D), BF16),
        scratch_shapes=[pltpu.VMEM((seq + META_PAD, MLA_SLOT), BF16),
                        pltpu.VMEM((MLA_V_D, seq + META_PAD), BF16)],
        compiler_params=_params(("parallel", "parallel", "arbitrary"), blocks, scratch,
                                6 * _nbytes((tk, tq), F32)),
        name=name,
    )(q, kv, kpe, kvm, kpem)


def _merge_kernel(od_ref, om_ref, wa_ref, wb_ref, gd_ref, gm_ref, side_ref, out_ref, side_out_ref):
    ya = jnp.dot(od_ref[...], wa_ref[...], preferred_element_type=F32)
    yb = jnp.dot(om_ref[...], wb_ref[...], preferred_element_type=F32)
    out_ref[...] = (gd_ref[...].astype(F32) * ya + gm_ref[...].astype(F32) * yb).astype(out_ref.dtype)
    _side_cast_run([side_ref], [side_out_ref])


def _merge(od, om, wa, wb, gates, side, *, tm, tn, name):
    m, k = od.shape
    n = wa.shape[1]
    nj = n // tn
    grid = (m // tm, nj)
    side_in, side_out, side_shape, side_bytes = _side_cast_specs(side, grid)
    blocks = (2 * _nbytes((tm, k), BF16) + 2 * _nbytes((k, tn), BF16) + 3 * _nbytes((tm, tn), BF16)
              + side_bytes)
    return pl.pallas_call(
        _merge_kernel,
        grid=grid,
        in_specs=[pl.BlockSpec((tm, k), lambda i, j: (i, 0)),
                  pl.BlockSpec((tm, k), lambda i, j: (i, 0)),
                  pl.BlockSpec((k, tn), lambda i, j: (0, j)),
                  pl.BlockSpec((k, tn), lambda i, j: (0, j)),
                  pl.BlockSpec((tm, tn), lambda i, j: (i, j)),
                  pl.BlockSpec((tm, tn), lambda i, j: (i, nj + j)),
                  side_in],
        out_specs=[pl.BlockSpec((tm, tn), lambda i, j: (i, j)), side_out],
        out_shape=[jax.ShapeDtypeStruct((m, n), BF16), side_shape],
        compiler_params=_params(("parallel", "arbitrary"), blocks, 0, 4 * _nbytes((tm, tn), F32)),
        name=name,
    )(od, om, wa, wb, gates, gates, side.w)


def _rope_tables(t):
    inv_freq = 1.0 / (ROPE_THETA ** (jnp.arange(0, ROPE_D, 2, dtype=F32) / ROPE_D))
    ang = jnp.arange(t).astype(F32)[:, None] * inv_freq[None, :]
    pad = jnp.zeros((t, LANES - ROPE_D), F32)
    cos = jnp.concatenate([jnp.cos(ang), jnp.cos(ang), pad], axis=-1)
    sin = jnp.concatenate([jnp.sin(ang), jnp.sin(ang), pad], axis=-1)
    return cos, sin


def _rotate_cols(w):
    half = ROPE_D // 2
    return jnp.concatenate([-w[..., half:], w[..., :half]], axis=-1)


def kernel(x, meta_tokens, ffn1_norm, ffn1_w_gate, ffn1_w_up, ffn1_w_down, mix_norm, w_in, diff_lambda_q1, diff_lambda_k1, diff_lambda_q2, diff_lambda_k2, diff_subln, mla_q_norm, mla_w_uq, mla_kv_norm, mla_w_ukv, w_gate, b_gate, w_branch_diff, w_branch_mla, w_out, ffn2_norm, ffn2_w_gate, ffn2_w_up, ffn2_w_down, final_norm):
    batch, seq, d = x.shape
    depth = ffn1_norm.shape[0]
    m = batch * seq
    dq_w = DIFF_HEADS * 2 * DIFF_HEAD_DIM
    dv_w = DIFF_HEADS * DIFF_V_DIM
    main_w = 2 * dq_w + dv_w + Q_LORA + KV_LORA
    cq_block = (2 * dq_w + dv_w) // Q_LORA
    ckv_block = (2 * dq_w + dv_w + Q_LORA) // KV_LORA
    proj_steps = (m // ROWS) * (main_w // PROJ_COLS)
    gate_steps = (m // ROWS) * (2 * d // GATE_COLS)
    merge_steps = (m // ROWS) * (d // MERGE_COLS)

    cos, sin = _rope_tables(N_META + seq)
    cos_m, sin_m, cos_r, sin_r = cos[:N_META], sin[:N_META], cos[N_META:], sin[N_META:]
    slopes = jnp.asarray([2.0 ** (-8.0 * (h + 1) / DIFF_HEADS) for h in range(DIFF_HEADS)],
                         F32).reshape(DIFF_HEADS, 1, 1)
    col_scale = jnp.concatenate([jnp.full((1, dq_w), DIFF_HEAD_DIM ** -0.5 * LOG2E, F32),
                                 jnp.ones((1, main_w - dq_w), F32)], axis=-1)
    row = lambda v: v.reshape(1, -1).astype(F32)

    xs = x.reshape(m, d)
    xm = meta_tokens.astype(x.dtype)

    for l in range(depth):
        lam_init = 0.8 - 0.6 * math.exp(-0.3 * l)
        w_main = _cast_cols(w_in, l, main_w, tr=min(CAST_ROWS, d), tc=CAST_COLS, name="cast_w_in")
        w_kr = _tail_cols(w_in, l, main_w, tr=min(CAST_ROWS, d), name="w_in_rope_cols")
        zpad = jnp.zeros((d, LANES - ROPE_D), F32)
        w_kpe = jnp.concatenate([w_kr, zpad, _rotate_cols(w_kr), zpad], axis=-1).astype(BF16)

        wuq = mla_w_uq[l].reshape(Q_LORA, MLA_HEADS, NOPE_D + ROPE_D)
        w_q = jnp.concatenate([wuq, _rotate_cols(wuq[..., NOPE_D:])], axis=-1)
        w_q = w_q.reshape(Q_LORA, MLA_HEADS * MLA_SLOT).astype(BF16)
        w_ukv = mla_w_ukv[l].astype(BF16)

        xm, wg1, wu1, wd1 = _ffn_meta(xm, row(ffn1_norm[l]), ffn1_w_gate, ffn1_w_up, ffn1_w_down, l,
                                      tf=FFN_CHUNK, name="ffn1_meta")
        xs = _ffn(xs, row(ffn1_norm[l]), wg1, wu1, wd1, tm=FFN_ROWS, tf=FFN_CHUNK, name="ffn1")

        hs, kpe_s = _norm_kpe(xs, row(mix_norm[l]), w_kpe, cos_r, sin_r, tm=FFN_ROWS, name="mix_norm")
        hm, kpe_m = _norm_kpe(xm, row(mix_norm[l]), w_kpe, cos_m, sin_m, tm=N_META,
                              name="mix_norm_meta")

        ps, wg2, w_g, w_o = _mm(hs, w_main, tm=ROWS, tn=PROJ_COLS, out_dtype=BF16, scale=col_scale,
                                name="proj", sides=(_side(ffn2_w_gate, l, proj_steps),
                                                    _side(w_gate, l, proj_steps),
                                                    _side(w_out, l, proj_steps)))
        pm = _mm(hm, w_main, tm=N_META, tn=PROJ_COLS, out_dtype=BF16, scale=col_scale, name="proj_meta")
        gates, wu2, w_pa, w_pb = _mm(hs, w_g, tm=ROWS, tn=GATE_COLS, out_dtype=BF16,
                                     bias=row(b_gate[l]), name="gates",
                                     sides=(_side(ffn2_w_up, l, gate_steps),
                                            _side(w_branch_diff, l, gate_steps),
                                            _side(w_branch_mla, l, gate_steps)))

        q_mla = _mlaq(ps, cq_block, row(mla_q_norm[l]), w_q, cos_r, sin_r, tm=ROWS, heads=MLAQ_HEADS,
                      scale=(NOPE_D + ROPE_D) ** -0.5 * LOG2E, name="mla_q")
        kv_s = _mlakv(ps, ckv_block, row(mla_kv_norm[l]), w_ukv, tm=ROWS, tn=MLAKV_COLS, name="mla_kv")
        kv_m = _mlakv(pm, ckv_block, row(mla_kv_norm[l]), w_ukv, tm=N_META, tn=MLAKV_COLS,
                      name="mla_kv_meta")

        pad_rows = lambda a: jnp.pad(a, ((0, META_PAD - N_META), (0, 0)))
        o_diff = _diff_attn(ps, pad_rows(pm), slopes, row(diff_lambda_q1[l]), row(diff_lambda_k1[l]),
                            row(diff_lambda_q2[l]), row(diff_lambda_k2[l]), row(diff_subln[l]),
                            batch=batch, seq=seq, tq=DIFF_TQ, tk=DIFF_TK, lam_init=lam_init,
                            name="diff_attn")
        o_mla = _mla_attn(q_mla, kv_s, kpe_s, pad_rows(kv_m), pad_rows(kpe_m),
                          batch=batch, seq=seq, tq=MLA_TQ, tk=MLA_TK, name="mla_attn")

        merged, wd2 = _merge(o_diff, o_mla, w_pa, w_pb, gates,
                             _side(ffn2_w_down, l, merge_steps, bc=MERGE_COLS),
                             tm=ROWS, tn=MERGE_COLS, name="merge")
        xs = _mm(merged, w_o, tm=ROWS, tn=OUT_COLS, out_dtype=F32, res=xs, name="out_proj")

        last = l == depth - 1
        xs = _ffn(xs, row(ffn2_norm[l]), wg2, wu2, wd2, tm=FFN_ROWS, tf=FFN_CHUNK,
                  final_gain=row(final_norm) if last else None, name="ffn2")
        if not last:
            raise NotImplementedError("DEPTH > 1 needs the meta-row query path")

    return xs.reshape(batch, seq, d)
```

```python
import functools
import math
from typing import NamedTuple

import numpy as np
import jax
import jax.numpy as jnp
from jax import lax
from jax.experimental import pallas as pl
from jax.experimental.pallas import tpu as pltpu

N_META = 16
EPS = 1e-6
DIFF_HEADS = 8
DIFF_HEAD_DIM = 128
DIFF_V_DIM = 256
MLA_HEADS = 16
Q_LORA = 1024
KV_LORA = 512
NOPE_D = 128
ROPE_D = 64
MLA_V_D = 128
ROPE_THETA = 10000.0

LANES = 128
MLA_SLOT = 256
META_PAD = 128
VMEM_CAP = 60000 * 1024
VMEM_PHYSICAL = 64 * 2**20
VMEM_RESERVE = 2 * 2**20
VMEM_FLOOR = 16 * 2**20
NEG_BIG = -1e30
LOG2E = math.log2(math.e)

ROWS = 1024
PROJ_COLS = 768
GATE_COLS = 1024
MERGE_COLS = 512
OUT_COLS = 512
FFN_ROWS = 512
FFN_CHUNK = 256
CAST_ROWS, CAST_COLS = 1024, 1536
MLAQ_HEADS = 8
MLAKV_COLS = 2048
DIFF_TQ, DIFF_TK = 256, 512
MLA_TQ, MLA_TK = 2048, 512

F32 = jnp.float32
BF16 = jnp.bfloat16


def _nbytes(shape, dtype):
    return int(np.prod(shape)) * jnp.dtype(dtype).itemsize


def _params(semantics, block_bytes, scratch_bytes, temp_bytes):
    need = 2 * block_bytes + scratch_bytes + temp_bytes
    return pltpu.CompilerParams(dimension_semantics=semantics,
                                vmem_limit_bytes=int(min(VMEM_CAP, max(need, VMEM_FLOOR))))


def _sigmoid(x):
    return 0.5 * jnp.tanh(0.5 * x) + 0.5


def _rms(x, gain):
    return x * lax.rsqrt(jnp.mean(x * x, axis=-1, keepdims=True) + EPS) * gain


def _ffn_kernel(x_ref, g_ref, wga_ref, wua_ref, wda_ref, wgb_ref, wub_ref, wdb_ref, *rest,
                final_norm, nchunks):
    out_ref, h_scr, x_scr, x_sem = rest[-4:]
    i = pl.program_id(0)
    f = pl.program_id(1)
    last = pl.num_programs(1) - 1
    tm = x_scr.shape[0]

    def x_copy(block):
        return pltpu.make_async_copy(x_ref.at[pl.ds(block * tm, tm), :], x_scr, x_sem)

    def act(wg_ref, wu_ref):
        h = h_scr[...]
        g = jnp.dot(h, wg_ref[...], preferred_element_type=F32)
        u = jnp.dot(h, wu_ref[...], preferred_element_type=F32)
        return (0.5 * (g * _sigmoid(g)) * u).astype(BF16)

    def both():
        a = act(wga_ref, wua_ref)
        b = act(wgb_ref, wub_ref)
        return (jnp.dot(a, wda_ref[...], preferred_element_type=F32)
                + jnp.dot(b, wdb_ref[...], preferred_element_type=F32))

    @pl.when((f == 0) & (i == 0))
    def _():
        x_copy(0).start()

    @pl.when(f == 0)
    def _():
        x_copy(i).wait()
        x = x_scr[...]
        h_scr[...] = _rms(x, g_ref[...]).astype(BF16)
        out_ref[...] = x + both()

    @pl.when((f == 1) & (i + 1 < pl.num_programs(0)))
    def _():
        x_copy(i + 1).start()

    @pl.when((f > 0) & (f < last))
    def _():
        out_ref[...] += both()

    @pl.when((f == last) & (f > 0))
    def _():
        if nchunks % 2:
            tail = jnp.dot(act(wga_ref, wua_ref), wda_ref[...], preferred_element_type=F32)
        else:
            tail = both()
        y = out_ref[...] + tail
        out_ref[...] = _rms(y, rest[0][...]) if final_norm else y


def _ffn(x, gain, wg, wu, wd, *, tm, tf, name, final_gain=None):
    m, d = x.shape
    nchunks = wg.shape[1] // tf
    assert nchunks > 2, "step 0 handles two chunks and step 1 prefetches the next rows"
    kern = functools.partial(_ffn_kernel, final_norm=final_gain is not None, nchunks=nchunks)
    vec = pl.BlockSpec((1, d), lambda i, f: (0, 0))
    finals = [] if final_gain is None else [final_gain]
    col_a = lambda i, f: (0, 2 * f)
    col_b = lambda i, f: (0, jnp.minimum(2 * f + 1, nchunks - 1))
    row_a = lambda i, f: (2 * f, 0)
    row_b = lambda i, f: (jnp.minimum(2 * f + 1, nchunks - 1), 0)
    weights = 2 * (2 * _nbytes((d, tf), BF16) + _nbytes((tf, d), BF16))
    need = (_nbytes((tm, d), F32) + 2 * _nbytes((tm, d), F32) + 2 * weights
            + _nbytes((tm, d), BF16) + 6 * _nbytes((tm, tf), F32) + _nbytes((tm, d), F32))
    return pl.pallas_call(
        kern,
        grid=(m // tm, (nchunks + 1) // 2),
        in_specs=[
            pl.BlockSpec(memory_space=pl.ANY),
            vec,
            pl.BlockSpec((d, tf), col_a), pl.BlockSpec((d, tf), col_a), pl.BlockSpec((tf, d), row_a),
            pl.BlockSpec((d, tf), col_b), pl.BlockSpec((d, tf), col_b), pl.BlockSpec((tf, d), row_b),
        ] + [vec] * len(finals),
        out_specs=pl.BlockSpec((tm, d), lambda i, f: (i, 0)),
        out_shape=jax.ShapeDtypeStruct((m, d), F32),
        scratch_shapes=[pltpu.VMEM((tm, d), BF16), pltpu.VMEM((tm, d), F32),
                        pltpu.SemaphoreType.DMA(())],
        compiler_params=pltpu.CompilerParams(
            dimension_semantics=("arbitrary", "arbitrary"),
            vmem_limit_bytes=int(min(VMEM_PHYSICAL - VMEM_RESERVE, max(need, VMEM_FLOOR)))),
        name=name,
    )(x, gain, wg, wu, wd, wg, wu, wd, *finals)


def _ffn_meta_kernel(x_ref, g_ref, wg_ref, wu_ref, wd_ref, out_ref, wg_out, wu_out, wd_out, h_scr):
    @pl.when(pl.program_id(0) == 0)
    def _():
        x = x_ref[...]
        h_scr[...] = _rms(x, g_ref[...]).astype(BF16)
        out_ref[...] = x

    wg = wg_ref[...].astype(BF16)
    wu = wu_ref[...].astype(BF16)
    wd = wd_ref[...].astype(BF16)
    wg_out[...] = wg
    wu_out[...] = wu
    wd_out[...] = wd
    h = h_scr[...]
    g = jnp.dot(h, wg, preferred_element_type=F32)
    u = jnp.dot(h, wu, preferred_element_type=F32)
    a = (0.5 * (g * _sigmoid(g)) * u).astype(BF16)
    out_ref[...] += jnp.dot(a, wd, preferred_element_type=F32)


def _ffn_meta(x, gain, wg, wu, wd, l, *, tf, name):
    m, d = x.shape
    ff = wg.shape[2]
    blocks = (2 * _nbytes((m, d), F32) + 3 * _nbytes((d, tf), F32) + 3 * _nbytes((d, tf), BF16))
    return pl.pallas_call(
        _ffn_meta_kernel,
        grid=(ff // tf,),
        in_specs=[pl.BlockSpec((m, d), lambda f: (0, 0)),
                  pl.BlockSpec((1, d), lambda f: (0, 0)),
                  pl.BlockSpec((None, d, tf), lambda f: (l, 0, f)),
                  pl.BlockSpec((None, d, tf), lambda f: (l, 0, f)),
                  pl.BlockSpec((None, tf, d), lambda f: (l, f, 0))],
        out_specs=[pl.BlockSpec((m, d), lambda f: (0, 0)),
                   pl.BlockSpec((d, tf), lambda f: (0, f)),
                   pl.BlockSpec((d, tf), lambda f: (0, f)),
                   pl.BlockSpec((tf, d), lambda f: (f, 0))],
        out_shape=[jax.ShapeDtypeStruct((m, d), F32),
                   jax.ShapeDtypeStruct((d, ff), BF16),
                   jax.ShapeDtypeStruct((d, ff), BF16),
                   jax.ShapeDtypeStruct((ff, d), BF16)],
        scratch_shapes=[pltpu.VMEM((m, d), BF16)],
        compiler_params=_params(("arbitrary",), blocks, _nbytes((m, d), BF16),
                                3 * _nbytes((d, tf), BF16)),
        name=name,
    )(x, gain, wg, wu, wd)


def _norm_kpe_kernel(x_ref, g_ref, w_ref, cos_ref, sin_ref, h_ref, kpe_ref):
    h = _rms(x_ref[...], g_ref[...]).astype(h_ref.dtype)
    h_ref[...] = h
    y = jnp.dot(h, w_ref[...], preferred_element_type=F32)
    kpe_ref[...] = (y[:, :LANES] * cos_ref[...] + y[:, LANES:] * sin_ref[...]).astype(kpe_ref.dtype)


def _norm_kpe(x, gain, w, cos, sin, *, tm, name):
    m, d = x.shape
    nt = cos.shape[0] // tm
    blocks = (_nbytes((tm, d), F32) + _nbytes((tm, d), BF16) + _nbytes((d, 2 * LANES), BF16)
              + 3 * _nbytes((tm, LANES), F32))
    return pl.pallas_call(
        _norm_kpe_kernel,
        grid=(m // tm,),
        in_specs=[pl.BlockSpec((tm, d), lambda i: (i, 0)),
                  pl.BlockSpec((1, d), lambda i: (0, 0)),
                  pl.BlockSpec((d, 2 * LANES), lambda i: (0, 0)),
                  pl.BlockSpec((tm, LANES), lambda i: (i % nt, 0)),
                  pl.BlockSpec((tm, LANES), lambda i: (i % nt, 0))],
        out_specs=[pl.BlockSpec((tm, d), lambda i: (i, 0)),
                   pl.BlockSpec((tm, LANES), lambda i: (i, 0))],
        out_shape=[jax.ShapeDtypeStruct((m, d), BF16), jax.ShapeDtypeStruct((m, LANES), BF16)],
        compiler_params=_params(("parallel",), blocks, 0, 2 * _nbytes((tm, d), F32)),
        name=name,
    )(x, gain, w, cos, sin)


def _cast_kernel(w_ref, out_ref):
    out_ref[...] = w_ref[...].astype(out_ref.dtype)


def _cast_cols(w, l, cols, *, tr, tc, name):
    rows = w.shape[1]
    blocks = _nbytes((tr, tc), w.dtype) + _nbytes((tr, tc), BF16)
    return pl.pallas_call(
        _cast_kernel,
        grid=(rows // tr, cols // tc),
        in_specs=[pl.BlockSpec((None, tr, tc), lambda i, j: (l, i, j))],
        out_specs=pl.BlockSpec((tr, tc), lambda i, j: (i, j)),
        out_shape=jax.ShapeDtypeStruct((rows, cols), BF16),
        compiler_params=_params(("parallel", "parallel"), blocks, 0, 0),
        name=name,
    )(w)


def _tail_kernel(w_ref, out_ref, *, valid):
    lane = lax.broadcasted_iota(jnp.int32, out_ref.shape, 1)
    out_ref[...] = jnp.where(lane < valid, w_ref[...], 0.0)


def _tail_cols(w, l, start, *, tr, name):
    rows, cols = w.shape[1:]
    assert start % LANES == 0 and cols - start <= LANES
    blocks = 2 * _nbytes((tr, LANES), F32)
    out = pl.pallas_call(
        functools.partial(_tail_kernel, valid=cols - start),
        grid=(rows // tr,),
        in_specs=[pl.BlockSpec((None, tr, LANES), lambda i: (l, i, start // LANES))],
        out_specs=pl.BlockSpec((tr, LANES), lambda i: (i, 0)),
        out_shape=jax.ShapeDtypeStruct((rows, LANES), F32),
        compiler_params=_params(("parallel",), blocks, 0, 0),
        name=name,
    )(w)
    return out[:, :cols - start]


class SideCast(NamedTuple):
    w: jax.Array
    l: int
    br: int
    bc: int


def _side(w, l, steps, bc=None):
    rows, cols = w.shape[1:]
    bc = cols if bc is None else bc
    sublanes = 16
    br = next(r for r in range(sublanes, rows + 1, sublanes)
              if rows % r == 0 and (rows // r) * (cols // bc) <= steps)
    return SideCast(w, l, br, bc)


def _side_cast_specs(side, grid):
    rows, cols = side.w.shape[1:]
    ncols = cols // side.bc
    nblocks = (rows // side.br) * ncols
    assert rows % side.br == 0 and cols % side.bc == 0 and nblocks <= math.prod(grid), (side.w.shape, grid)

    def block(*idx):
        step = 0
        for i, n in zip(idx, grid):
            step = step * n + i
        t = jnp.minimum(step, nblocks - 1)
        return t // ncols, t % ncols

    in_spec = pl.BlockSpec((None, side.br, side.bc), lambda *idx: (side.l, *block(*idx)))
    out_spec = pl.BlockSpec((side.br, side.bc), block)
    nbytes = _nbytes((side.br, side.bc), F32) + _nbytes((side.br, side.bc), BF16)
    return in_spec, out_spec, jax.ShapeDtypeStruct((rows, cols), BF16), nbytes


def _side_cast_run(side_in_refs, side_out_refs):
    for src, dst in zip(side_in_refs, side_out_refs):
        dst[...] = src[...].astype(dst.dtype)


def _mm_kernel(*refs, has_scale, has_bias, has_res, n_side):
    n_in = 2 + has_scale + has_bias + has_res + n_side
    a_ref, w_ref = refs[0], refs[1]
    extra = list(refs[2:n_in - n_side])
    out_ref = refs[n_in]
    y = jnp.dot(a_ref[...], w_ref[...], preferred_element_type=F32)
    if has_scale:
        y = y * extra.pop(0)[...]
    if has_bias:
        y = _sigmoid(y + extra.pop(0)[...])
    if has_res:
        y = y + extra.pop(0)[...]
    out_ref[...] = y.astype(out_ref.dtype)
    _side_cast_run(refs[n_in - n_side:n_in], refs[n_in + 1:])


def _mm(a, w, *, tm, tn, out_dtype, name, scale=None, bias=None, res=None, sides=()):
    m, k = a.shape
    n = w.shape[1]
    grid = (m // tm, n // tn)
    ins = [a, w]
    in_specs = [pl.BlockSpec((tm, k), lambda i, j: (i, 0)),
                pl.BlockSpec((k, tn), lambda i, j: (0, j))]
    blocks = _nbytes((tm, k), a.dtype) + _nbytes((k, tn), w.dtype) + _nbytes((tm, tn), out_dtype)
    for vec in (scale, bias):
        if vec is not None:
            ins.append(vec)
            in_specs.append(pl.BlockSpec((1, tn), lambda i, j: (0, j)))
    if res is not None:
        ins.append(res)
        in_specs.append(pl.BlockSpec((tm, tn), lambda i, j: (i, j)))
        blocks += _nbytes((tm, tn), res.dtype)
    out_specs = [pl.BlockSpec((tm, tn), lambda i, j: (i, j))]
    out_shape = [jax.ShapeDtypeStruct((m, n), out_dtype)]
    for side in sides:
        in_spec, out_spec, shape, nbytes = _side_cast_specs(side, grid)
        ins.append(side.w)
        in_specs.append(in_spec)
        out_specs.append(out_spec)
        out_shape.append(shape)
        blocks += nbytes
    kern = functools.partial(_mm_kernel, has_scale=scale is not None, has_bias=bias is not None,
                             has_res=res is not None, n_side=len(sides))
    outs = pl.pallas_call(
        kern,
        grid=grid,
        in_specs=in_specs,
        out_specs=out_specs,
        out_shape=out_shape,
        compiler_params=_params(("parallel", "arbitrary"), blocks, 0, 4 * _nbytes((tm, tn), F32)),
        name=name,
    )(*ins)
    return outs if sides else outs[0]


def _mlaq_kernel(c_ref, g_ref, w_ref, cos_ref, sin_ref, out_ref, cn_scr, *, heads, scale):
    @pl.when(pl.program_id(1) == 0)
    def _():
        cn_scr[...] = _rms(c_ref[...].astype(F32), g_ref[...]).astype(BF16)

    y = jnp.dot(cn_scr[...], w_ref[...], preferred_element_type=F32)
    cos = cos_ref[...]
    sin = sin_ref[...]
    for h in range(heads):
        s0 = h * MLA_SLOT
        out_ref[:, s0:s0 + NOPE_D] = (y[:, s0:s0 + NOPE_D] * scale).astype(out_ref.dtype)
        hi = y[:, s0 + NOPE_D:s0 + MLA_SLOT]
        pe = hi * cos + pltpu.roll(hi, LANES // 2, axis=1) * sin
        out_ref[:, s0 + NOPE_D:s0 + MLA_SLOT] = (pe * scale).astype(out_ref.dtype)


def _mlaq(p, col_block, gain, w, cos, sin, *, tm, heads, scale, name):
    m = p.shape[0]
    k = gain.shape[1]
    wn = heads * MLA_SLOT
    groups = w.shape[1] // wn
    nt = cos.shape[0] // tm
    kern = functools.partial(_mlaq_kernel, heads=heads, scale=scale)
    blocks = (_nbytes((tm, k), BF16) + _nbytes((k, wn), BF16) + 2 * _nbytes((tm, LANES), F32)
              + _nbytes((tm, heads * MLA_SLOT), BF16))
    return pl.pallas_call(
        kern,
        grid=(m // tm, groups),
        in_specs=[pl.BlockSpec((tm, k), lambda i, j: (i, col_block)),
                  pl.BlockSpec((1, k), lambda i, j: (0, 0)),
                  pl.BlockSpec((k, wn), lambda i, j: (0, j)),
                  pl.BlockSpec((tm, LANES), lambda i, j: (i % nt, 0)),
                  pl.BlockSpec((tm, LANES), lambda i, j: (i % nt, 0))],
        out_specs=pl.BlockSpec((tm, heads * MLA_SLOT), lambda i, j: (i, j)),
        out_shape=jax.ShapeDtypeStruct((m, groups * heads * MLA_SLOT), BF16),
        scratch_shapes=[pltpu.VMEM((tm, k), BF16)],
        compiler_params=_params(("parallel", "arbitrary"), blocks, _nbytes((tm, k), BF16),
                                2 * _nbytes((tm, wn), F32)),
        name=name,
    )(p, gain, w, cos, sin)


def _mlakv_kernel(c_ref, g_ref, w_ref, out_ref, cn_scr):
    @pl.when(pl.program_id(1) == 0)
    def _():
        cn_scr[...] = _rms(c_ref[...].astype(F32), g_ref[...]).astype(BF16)

    out_ref[...] = jnp.dot(cn_scr[...], w_ref[...], preferred_element_type=F32).astype(out_ref.dtype)


def _mlakv(p, col_block, gain, w, *, tm, tn, name):
    m = p.shape[0]
    k, n = w.shape
    blocks = _nbytes((tm, k), BF16) + _nbytes((k, tn), BF16) + _nbytes((tm, tn), BF16)
    return pl.pallas_call(
        _mlakv_kernel,
        grid=(m // tm, n // tn),
        in_specs=[pl.BlockSpec((tm, k), lambda i, j: (i, col_block)),
                  pl.BlockSpec((1, k), lambda i, j: (0, 0)),
                  pl.BlockSpec((k, tn), lambda i, j: (0, j))],
        out_specs=pl.BlockSpec((tm, tn), lambda i, j: (i, j)),
        out_shape=jax.ShapeDtypeStruct((m, n), BF16),
        scratch_shapes=[pltpu.VMEM((tm, k), BF16)],
        compiler_params=_params(("parallel", "arbitrary"), blocks, _nbytes((tm, k), BF16),
                                2 * _nbytes((tm, tn), F32)),
        name=name,
    )(p, gain, w)


def _dot_nt(a, b):
    return lax.dot_general(a, b, (((1,), (1,)), ((), ())), preferred_element_type=F32)


def _transpose_bf16(x):
    return x.astype(F32).T.astype(BF16)


def _softmax_first_t(s_t, v_t):
    m = jnp.max(s_t, axis=0, keepdims=True)
    p = jnp.exp2(s_t - m)
    return m, jnp.sum(p, axis=0, keepdims=True), jnp.dot(v_t, p.astype(BF16), preferred_element_type=F32)


def _softmax_next_t(s_t, shift, v_t, m, l, acc):
    m_new = jnp.maximum(m, jnp.max(s_t, axis=0, keepdims=True) + shift)
    alpha = jnp.exp2(m - m_new)
    p = jnp.exp2(s_t - (m_new - shift))
    l = alpha * l + jnp.sum(p, axis=0, keepdims=True)
    acc = alpha * acc + jnp.dot(v_t, p.astype(BF16), preferred_element_type=F32)
    return m_new, l, acc


def _meta_mask_t(s_t):
    row = lax.broadcasted_iota(jnp.int32, s_t.shape, 0)
    return jnp.where(row < N_META, s_t, NEG_BIG)


def _diff_attn_kernel(q_ref, k_ref, v_ref, km_ref, vm_ref, slope_ref, lq1_ref, lk1_ref, lq2_ref,
                      lk2_ref, subln_ref, out_ref, vt_scr, bias_scr, *, tk, lam_init):
    tq = q_ref.shape[0]
    ratio = tk // tq
    nk = k_ref.shape[0] // tk
    hd = DIFF_HEAD_DIM
    qi = pl.program_id(2)
    slope2 = slope_ref[...] * LOG2E

    @pl.when(qi == 0)
    def _():
        vt_scr[:, :META_PAD] = _transpose_bf16(vm_ref[...])
        for c in range(nk):
            vt_scr[:, META_PAD + c * tk:META_PAD + (c + 1) * tk] = _transpose_bf16(
                v_ref[c * tk:(c + 1) * tk, :])
        d = (lax.broadcasted_iota(jnp.int32, (tk, tq), 1)
             - lax.broadcasted_iota(jnp.int32, (tk, tq), 0)).astype(F32) * slope2
        bias_scr[0] = -d
        bias_scr[1] = d
        for o in range(ratio):
            bias_scr[2 + o] = -jnp.abs(d + (o * tq) * slope2)

    q1 = q_ref[:, :hd]
    q2 = q_ref[:, hd:]

    def qk(j):
        kb = k_ref[j * tk:(j + 1) * tk, :]
        return _dot_nt(kb[:, :hd], q1), _dot_nt(kb[:, hd:], q2)

    vtm = vt_scr[:, :META_PAD]
    st1 = _softmax_first_t(_meta_mask_t(_dot_nt(km_ref[:, :hd], q1)), vtm)
    st2 = _softmax_first_t(_meta_mask_t(_dot_nt(km_ref[:, hd:], q2)), vtm)

    s1, s2 = qk(0)
    for j in range(nk):
        nxt = qk(j + 1) if j + 1 < nk else None
        off = qi - j * ratio
        bias = bias_scr[jnp.where(off < 0, 1, jnp.where(off >= ratio, 0, 2 + off))]
        gap = jnp.where(off < 0, j * tk - qi * tq, jnp.where(off >= ratio, qi * tq - j * tk, 0))
        shift = -gap.astype(F32) * slope2
        vt = vt_scr[:, META_PAD + j * tk:META_PAD + (j + 1) * tk]
        st1 = _softmax_next_t(s1 + bias, shift, vt, *st1)
        st2 = _softmax_next_t(s2 + bias, shift, vt, *st2)
        if nxt is not None:
            s1, s2 = nxt

    lam = (jnp.exp(jnp.sum(lq1_ref[...] * lk1_ref[...], axis=-1, keepdims=True))
           - jnp.exp(jnp.sum(lq2_ref[...] * lk2_ref[...], axis=-1, keepdims=True)) + lam_init)
    o_t = st1[2] / st1[1] - lam * (st2[2] / st2[1])
    out_ref[...] = (_rms(o_t.T, subln_ref[...]) * (1.0 - lam_init)).astype(out_ref.dtype)


def _diff_attn(p, pm, slopes, lq1, lk1, lq2, lk2, subln, *, batch, seq, tq, tk, lam_init, name):
    assert tk % tq == 0 and seq % tk == 0
    m = p.shape[0]
    hw = 2 * DIFF_HEAD_DIM
    nq = seq // tq
    koff = DIFF_HEADS
    voff = 2 * DIFF_HEADS
    kern = functools.partial(_diff_attn_kernel, tk=tk, lam_init=lam_init)
    n_bias = 2 + tk // tq
    vec = pl.BlockSpec((1, DIFF_HEAD_DIM), lambda b, h, i: (0, 0))
    blocks = (2 * _nbytes((tq, hw), BF16) + 2 * _nbytes((seq, hw), BF16)
              + 2 * _nbytes((META_PAD, hw), BF16))
    scratch = _nbytes((hw, seq + META_PAD), BF16) + n_bias * _nbytes((tk, tq), F32)
    return pl.pallas_call(
        kern,
        grid=(batch, DIFF_HEADS, nq),
        in_specs=[pl.BlockSpec((tq, hw), lambda b, h, i: (b * nq + i, h)),
                  pl.BlockSpec((seq, hw), lambda b, h, i: (b, koff + h)),
                  pl.BlockSpec((seq, hw), lambda b, h, i: (b, voff + h)),
                  pl.BlockSpec((META_PAD, hw), lambda b, h, i: (0, koff + h)),
                  pl.BlockSpec((META_PAD, hw), lambda b, h, i: (0, voff + h)),
                  pl.BlockSpec((None, 1, 1), lambda b, h, i: (h, 0, 0)),
                  vec, vec, vec, vec,
                  pl.BlockSpec((1, hw), lambda b, h, i: (0, 0))],
        out_specs=pl.BlockSpec((tq, hw), lambda b, h, i: (b * nq + i, h)),
        out_shape=jax.ShapeDtypeStruct((m, DIFF_HEADS * hw), BF16),
        scratch_shapes=[pltpu.VMEM((hw, seq + META_PAD), BF16),
                        pltpu.VMEM((n_bias, tk, tq), F32)],
        compiler_params=_params(("parallel", "parallel", "arbitrary"), blocks, scratch,
                                12 * _nbytes((tk, tq), F32)),
        name=name,
    )(p, p, p, pm, pm, slopes, lq1, lk1, lq2, lk2, subln)


def _mla_attn_kernel(q_ref, kv_ref, kpe_ref, kvm_ref, kpem_ref, out_ref, k_scr, vt_scr, *, tk):
    nk = kv_ref.shape[0] // tk

    @pl.when(pl.program_id(2) == 0)
    def _():
        k_scr[:META_PAD, :NOPE_D] = kvm_ref[:, :NOPE_D]
        k_scr[:META_PAD, NOPE_D:] = kpem_ref[...]
        k_scr[META_PAD:, :NOPE_D] = kv_ref[:, :NOPE_D]
        k_scr[META_PAD:, NOPE_D:] = kpe_ref[...]
        vt_scr[:, :META_PAD] = _transpose_bf16(kvm_ref[:, NOPE_D:])
        for c in range(nk):
            vt_scr[:, META_PAD + c * tk:META_PAD + (c + 1) * tk] = _transpose_bf16(
                kv_ref[c * tk:(c + 1) * tk, NOPE_D:])

    q = q_ref[...]

    def qk(j):
        return _dot_nt(k_scr[META_PAD + j * tk:META_PAD + (j + 1) * tk, :], q)

    st = _softmax_first_t(_meta_mask_t(_dot_nt(k_scr[:META_PAD, :], q)), vt_scr[:, :META_PAD])

    s = qk(0)
    for j in range(nk):
        nxt = qk(j + 1) if j + 1 < nk else None
        st = _softmax_next_t(s, 0.0, vt_scr[:, META_PAD + j * tk:META_PAD + (j + 1) * tk], *st)
        s = nxt

    out_ref[...] = (st[2] / st[1]).T.astype(out_ref.dtype)


def _mla_attn(q, kv, kpe, kvm, kpem, *, batch, seq, tq, tk, name):
    mrows = q.shape[0]
    nq = seq // tq
    kern = functools.partial(_mla_attn_kernel, tk=tk)
    blocks = (_nbytes((tq, MLA_SLOT), BF16) + _nbytes((seq, MLA_SLOT), BF16)
              + _nbytes((seq, LANES), BF16) + _nbytes((META_PAD, MLA_SLOT + LANES), BF16)
              + _nbytes((tq, MLA_V_D), BF16))
    scratch = _nbytes((seq + META_PAD, MLA_SLOT), BF16) + _nbytes((MLA_V_D, seq + META_PAD), BF16)
    return pl.pallas_call(
        kern,
        grid=(batch, MLA_HEADS, nq),
        in_specs=[pl.BlockSpec((tq, MLA_SLOT), lambda b, h, i: (b * nq + i, h)),
                  pl.BlockSpec((seq, MLA_SLOT), lambda b, h, i: (b, h)),
                  pl.BlockSpec((seq, LANES), lambda b, h, i: (b, 0)),
                  pl.BlockSpec((META_PAD, MLA_SLOT), lambda b, h, i: (0, h)),
                  pl.BlockSpec((META_PAD, LANES), lambda b, h, i: (0, 0))],
        out_specs=pl.BlockSpec((tq, MLA_V_D), lambda b, h, i: (b * nq + i, h)),
        out_shape=jax.ShapeDtypeStruct((mrows, MLA_HEADS * MLA_V_D), BF16),
        scratch_shapes=[pltpu.VMEM((seq + META_PAD, MLA_SLOT), BF16),
                        pltpu.VMEM((MLA_V_D, seq + META_PAD), BF16)],
        compiler_params=_params(("parallel", "parallel", "arbitrary"), blocks, scratch,
                                6 * _nbytes((tk, tq), F32)),
        name=name,
    )(q, kv, kpe, kvm, kpem)


def _merge_kernel(od_ref, om_ref, wa_ref, wb_ref, gd_ref, gm_ref, side_ref, out_ref, side_out_ref):
    ya = jnp.dot(od_ref[...], wa_ref[...], preferred_element_type=F32)
    yb = jnp.dot(om_ref[...], wb_ref[...], preferred_element_type=F32)
    out_ref[...] = (gd_ref[...].astype(F32) * ya + gm_ref[...].astype(F32) * yb).astype(out_ref.dtype)
    _side_cast_run([side_ref], [side_out_ref])


def _merge(od, om, wa, wb, gates, side, *, tm, tn, name):
    m, k = od.shape
    n = wa.shape[1]
    nj = n // tn
    grid = (m // tm, nj)
    side_in, side_out, side_shape, side_bytes = _side_cast_specs(side, grid)
    blocks = (2 * _nbytes((tm, k), BF16) + 2 * _nbytes((k, tn), BF16) + 3 * _nbytes((tm, tn), BF16)
              + side_bytes)
    return pl.pallas_call(
        _merge_kernel,
        grid=grid,
        in_specs=[pl.BlockSpec((tm, k), lambda i, j: (i, 0)),
                  pl.BlockSpec((tm, k), lambda i, j: (i, 0)),
                  pl.BlockSpec((k, tn), lambda i, j: (0, j)),
                  pl.BlockSpec((k, tn), lambda i, j: (0, j)),
                  pl.BlockSpec((tm, tn), lambda i, j: (i, j)),
                  pl.BlockSpec((tm, tn), lambda i, j: (i, nj + j)),
                  side_in],
        out_specs=[pl.BlockSpec((tm, tn), lambda i, j: (i, j)), side_out],
        out_shape=[jax.ShapeDtypeStruct((m, n), BF16), side_shape],
        compiler_params=_params(("parallel", "arbitrary"), blocks, 0, 4 * _nbytes((tm, tn), F32)),
        name=name,
    )(od, om, wa, wb, gates, gates, side.w)


def _rope_tables(t):
    inv_freq = 1.0 / (ROPE_THETA ** (jnp.arange(0, ROPE_D, 2, dtype=F32) / ROPE_D))
    ang = jnp.arange(t).astype(F32)[:, None] * inv_freq[None, :]
    pad = jnp.zeros((t, LANES - ROPE_D), F32)
    cos = jnp.concatenate([jnp.cos(ang), jnp.cos(ang), pad], axis=-1)
    sin = jnp.concatenate([jnp.sin(ang), jnp.sin(ang), pad], axis=-1)
    return cos, sin


def _rotate_cols(w):
    half = ROPE_D // 2
    return jnp.concatenate([-w[..., half:], w[..., :half]], axis=-1)


def kernel(x, meta_tokens, ffn1_norm, ffn1_w_gate, ffn1_w_up, ffn1_w_down, mix_norm, w_in, diff_lambda_q1, diff_lambda_k1, diff_lambda_q2, diff_lambda_k2, diff_subln, mla_q_norm, mla_w_uq, mla_kv_norm, mla_w_ukv, w_gate, b_gate, w_branch_diff, w_branch_mla, w_out, ffn2_norm, ffn2_w_gate, ffn2_w_up, ffn2_w_down, final_norm):
    batch, seq, d = x.shape
    depth = ffn1_norm.shape[0]
    m = batch * seq
    dq_w = DIFF_HEADS * 2 * DIFF_HEAD_DIM
    dv_w = DIFF_HEADS * DIFF_V_DIM
    main_w = 2 * dq_w + dv_w + Q_LORA + KV_LORA
    cq_block = (2 * dq_w + dv_w) // Q_LORA
    ckv_block = (2 * dq_w + dv_w + Q_LORA) // KV_LORA
    proj_steps = (m // ROWS) * (main_w // PROJ_COLS)
    gate_steps = (m // ROWS) * (2 * d // GATE_COLS)
    merge_steps = (m // ROWS) * (d // MERGE_COLS)

    cos, sin = _rope_tables(N_META + seq)
    cos_m, sin_m, cos_r, sin_r = cos[:N_META], sin[:N_META], cos[N_META:], sin[N_META:]
    slopes = jnp.asarray([2.0 ** (-8.0 * (h + 1) / DIFF_HEADS) for h in range(DIFF_HEADS)],
                         F32).reshape(DIFF_HEADS, 1, 1)
    col_scale = jnp.concatenate([jnp.full((1, dq_w), DIFF_HEAD_DIM ** -0.5 * LOG2E, F32),
                                 jnp.ones((1, main_w - dq_w), F32)], axis=-1)
    row = lambda v: v.reshape(1, -1).astype(F32)

    xs = x.reshape(m, d)
    xm = meta_tokens.astype(x.dtype)

    for l in range(depth):
        lam_init = 0.8 - 0.6 * math.exp(-0.3 * l)
        w_main = _cast_cols(w_in, l, main_w, tr=min(CAST_ROWS, d), tc=CAST_COLS, name="cast_w_in")
        w_kr = _tail_cols(w_in, l, main_w, tr=min(CAST_ROWS, d), name="w_in_rope_cols")
        zpad = jnp.zeros((d, LANES - ROPE_D), F32)
        w_kpe = jnp.concatenate([w_kr, zpad, _rotate_cols(w_kr), zpad], axis=-1).astype(BF16)

        wuq = mla_w_uq[l].reshape(Q_LORA, MLA_HEADS, NOPE_D + ROPE_D)
        w_q = jnp.concatenate([wuq, _rotate_cols(wuq[..., NOPE_D:])], axis=-1)
        w_q = w_q.reshape(Q_LORA, MLA_HEADS * MLA_SLOT).astype(BF16)
        w_ukv = mla_w_ukv[l].astype(BF16)

        xm, wg1, wu1, wd1 = _ffn_meta(xm, row(ffn1_norm[l]), ffn1_w_gate, ffn1_w_up, ffn1_w_down, l,
                                      tf=FFN_CHUNK, name="ffn1_meta")
        xs = _ffn(xs, row(ffn1_norm[l]), wg1, wu1, wd1, tm=FFN_ROWS, tf=FFN_CHUNK, name="ffn1")

        hs, kpe_s = _norm_kpe(xs, row(mix_norm[l]), w_kpe, cos_r, sin_r, tm=FFN_ROWS, name="mix_norm")
        hm, kpe_m = _norm_kpe(xm, row(mix_norm[l]), w_kpe, cos_m, sin_m, tm=N_META,
                              name="mix_norm_meta")

        ps, wg2, w_g, w_o = _mm(hs, w_main, tm=ROWS, tn=PROJ_COLS, out_dtype=BF16, scale=col_scale,
                                name="proj", sides=(_side(ffn2_w_gate, l, proj_steps),
                                                    _side(w_gate, l, proj_steps),
                                                    _side(w_out, l, proj_steps)))
        pm = _mm(hm, w_main, tm=N_META, tn=PROJ_COLS, out_dtype=BF16, scale=col_scale, name="proj_meta")
        gates, wu2, w_pa, w_pb = _mm(hs, w_g, tm=ROWS, tn=GATE_COLS, out_dtype=BF16,
                                     bias=row(b_gate[l]), name="gates",
                                     sides=(_side(ffn2_w_up, l, gate_steps),
                                            _side(w_branch_diff, l, gate_steps),
                                            _side(w_branch_mla, l, gate_steps)))

        q_mla = _mlaq(ps, cq_block, row(mla_q_norm[l]), w_q, cos_r, sin_r, tm=ROWS, heads=MLAQ_HEADS,
                      scale=(NOPE_D + ROPE_D) ** -0.5 * LOG2E, name="mla_q")
        kv_s = _mlakv(ps, ckv_block, row(mla_kv_norm[l]), w_ukv, tm=ROWS, tn=MLAKV_COLS, name="mla_kv")
        kv_m = _mlakv(pm, ckv_block, row(mla_kv_norm[l]), w_ukv, tm=N_META, tn=MLAKV_COLS,
                      name="mla_kv_meta")

        pad_rows = lambda a: jnp.pad(a, ((0, META_PAD - N_META), (0, 0)))
        o_diff = _diff_attn(ps, pad_rows(pm), slopes, row(diff_lambda_q1[l]), row(diff_lambda_k1[l]),
                            row(diff_lambda_q2[l]), row(diff_lambda_k2[l]), row(diff_subln[l]),
                            batch=batch, seq=seq, tq=DIFF_TQ, tk=DIFF_TK, lam_init=lam_init,
                            name="diff_attn")
        o_mla = _mla_attn(q_mla, kv_s, kpe_s, pad_rows(kv_m), pad_rows(kpe_m),
                          batch=batch, seq=seq, tq=MLA_TQ, tk=MLA_TK, name="mla_attn")

        merged, wd2 = _merge(o_diff, o_mla, w_pa, w_pb, gates,
                             _side(ffn2_w_down, l, merge_steps, bc=MERGE_COLS),
                             tm=ROWS, tn=MERGE_COLS, name="merge")
        xs = _mm(merged, w_o, tm=ROWS, tn=OUT_COLS, out_dtype=F32, res=xs, name="out_proj")

        last = l == depth - 1
        xs = _ffn(xs, row(ffn2_norm[l]), wg2, wu2, wd2, tm=FFN_ROWS, tf=FFN_CHUNK,
                  final_gain=row(final_norm) if last else None, name="ffn2")
        if not last:
            raise NotImplementedError("DEPTH > 1 needs the meta-row query path")

    return xs.reshape(batch, seq, d)
```
